```python
import math
import jax, jax.numpy as jnp
from jax import lax
import numpy as np

D_MODEL = 1024
BATCH = 4
SEQ = 4096
DEPTH = 2
DEC_BATCH = 32
DEC_SEQ = 8
PAST_LEN = 8192
PAGE_SIZE = 128

N_A_LAYERS = DEPTH // 2
N_B_LAYERS = DEPTH - N_A_LAYERS
D_FF = 2816
CONV_WIDTH = 31
CONV_INNER = D_MODEL
GROUPS = ((128, 1), (512, 4), (2048, 16))
N_GROUPS = 3
HEADS_PER_GROUP = 8
HEAD_DIM = 64
ATTN_WIDTH = HEADS_PER_GROUP * HEAD_DIM
Q_WIDTH = N_GROUPS * ATTN_WIDTH
KV_WIDTH = N_GROUPS * 2 * ATTN_WIDTH
N_BUCKETS = 32
MAX_DISTANCE = 2048
QBLOCK = 128
EPS = 1e-6
NEG_INF = -1e30

kernel_name = "yoco_conformer_dilated_attn_step"


def rmsnorm(x, g):
    xf = x.astype(jnp.float32)
    y = xf * lax.rsqrt(jnp.mean(xf * xf, axis=-1, keepdims=True) + EPS)
    return (y * g.astype(jnp.float32)).astype(x.dtype)


def layernorm(x, g, b):
    xf = x.astype(jnp.float32)
    mu = jnp.mean(xf, axis=-1, keepdims=True)
    var = jnp.mean(jnp.square(xf - mu), axis=-1, keepdims=True)
    y = (xf - mu) * lax.rsqrt(var + EPS)
    return (y * g.astype(jnp.float32) + b.astype(jnp.float32)).astype(x.dtype)


def swiglu_ffn(h, w_in, w_out):
    gate, up = jnp.split(h @ w_in, 2, axis=-1)
    return (jax.nn.silu(gate) * up) @ w_out


def conformer_conv(h, prev, w_pw1, b_pw1, w_dw, b_dw, ln_g, ln_b, w_pw2, b_pw2):
    a, g = jnp.split(h @ w_pw1 + b_pw1, 2, axis=-1)
    u = a * jax.nn.sigmoid(g)
    u_ext = jnp.concatenate([prev.astype(u.dtype), u], axis=1)
    c = lax.conv_general_dilated(u_ext, w_dw[:, None, :].astype(u.dtype), window_strides=(1,),
                                 padding='VALID', dimension_numbers=('NWC', 'WIO', 'NWC'),
                                 feature_group_count=CONV_INNER) + b_dw
    c = jax.nn.silu(layernorm(c, ln_g, ln_b))
    return c @ w_pw2 + b_pw2, u_ext[:, -(CONV_WIDTH - 1):]


def t5_bucket(dist):
    max_exact = N_BUCKETS // 2
    d_f = jnp.maximum(dist, 1).astype(jnp.float32)
    large = max_exact + (jnp.log(d_f / max_exact) / math.log(MAX_DISTANCE / max_exact)
                         * (N_BUCKETS - max_exact)).astype(jnp.int32)
    large = jnp.minimum(large, N_BUCKETS - 1)
    return jnp.where(dist < max_exact, dist, large)


def group_bias(rel_bias, g):
    w, d = GROUPS[g]
    n_keys = w // d + 1
    dist = d * jnp.arange(n_keys, dtype=jnp.int32)
    b = jnp.take(rel_bias, t5_bucket(dist), axis=0)
    return b[:, g * HEADS_PER_GROUP:(g + 1) * HEADS_PER_GROUP].T.astype(jnp.float32)


def dilated_attend(q, k_ext, v_ext, q_idx, min_valid, dilation, n_keys, bias):
    offs = dilation * jnp.arange(n_keys, dtype=jnp.int32)
    idx = q_idx[:, None] - offs[None, :]
    valid = idx >= min_valid
    idx = jnp.maximum(idx, 0)
    kg = jnp.take(k_ext, idx, axis=1)
    vg = jnp.take(v_ext, idx, axis=1)
    logits = jnp.einsum('bthd,btnhd->bhtn', q, kg).astype(jnp.float32) * (HEAD_DIM ** -0.5)
    logits = logits + bias[None, :, None, :]
    logits = jnp.where(valid[None, None], logits, NEG_INF)
    m = jnp.max(logits, axis=-1, keepdims=True)
    p = jnp.exp(logits - m)
    den = jnp.sum(p, axis=-1, keepdims=True)
    o = jnp.einsum('bhtn,btnhd->bthd', (p / den).astype(v_ext.dtype), vg)
    lse = (m + jnp.log(den))[..., 0]
    return o, jnp.transpose(lse, (0, 2, 1))


def prompt_dilated(q, kv, g, bias):
    w, d = GROUPS[g]
    n_keys = w // d + 1
    B, S = q.shape[0], q.shape[1]
    kv_ext = jnp.concatenate([jnp.zeros((B, w) + kv.shape[2:], kv.dtype), kv], axis=1)
    k_ext, v_ext = kv_ext[:, :, 0], kv_ext[:, :, 1]
    nb = S // QBLOCK
    qb = jnp.swapaxes(q.reshape(B, nb, QBLOCK, HEADS_PER_GROUP, HEAD_DIM), 0, 1)
    ib = (w + jnp.arange(S, dtype=jnp.int32)).reshape(nb, QBLOCK)
    o, lse = lax.map(lambda a: dilated_attend(a[0], k_ext, v_ext, a[1], w, d, n_keys, bias), (qb, ib))
    o = jnp.swapaxes(o, 0, 1).reshape(B, S, HEADS_PER_GROUP, HEAD_DIM)
    lse = jnp.swapaxes(lse, 0, 1).reshape(B, S, HEADS_PER_GROUP)
    return o, lse


def sample_dilated(q, kv_new, buf, g, bias):
    w, d = GROUPS[g]
    n_keys = w // d + 1
    B, T = q.shape[0], q.shape[1]
    buf_len = buf.shape[1]
    kv_ext = jnp.concatenate([jnp.zeros((B, w) + kv_new.shape[2:], kv_new.dtype),
                              buf.astype(kv_new.dtype), kv_new], axis=1)
    k_ext, v_ext = kv_ext[:, :, 0], kv_ext[:, :, 1]
    q_idx = w + buf_len + jnp.arange(T, dtype=jnp.int32)
    min_valid = w + buf_len - PAST_LEN
    return dilated_attend(q, k_ext, v_ext, q_idx, min_valid, d, n_keys, bias)


def merge_groups(outs, lses):
    wts = jax.nn.softmax(jnp.stack(lses, axis=0), axis=0)
    return jnp.einsum('gbth,gbthd->bthd', wts.astype(outs[0].dtype), jnp.stack(outs, axis=0))


def trunk(x, conv_prev, kv_bufs, norms, ffn_w_in, ffn_w_out, conv_w_pw1, conv_b_pw1, conv_w_dw, conv_b_dw,
          conv_ln_g, conv_ln_b, conv_w_pw2, conv_b_pw2, norm_kv, w_kv, attn_w_q, attn_w_o, rel_bias, norm_final):
    B, T = x.shape[0], x.shape[1]
    biases = [group_bias(rel_bias, g) for g in range(N_GROUPS)]
    new_conv = []
    new_kv = []
    shared = None
    for li in range(DEPTH):
        x = x + 0.5 * swiglu_ffn(rmsnorm(x, norms[li, 0]), ffn_w_in[li, 0], ffn_w_out[li, 0])
        h = rmsnorm(x, norms[li, 1])
        if li < N_A_LAYERS:
            y, st = conformer_conv(h, conv_prev[li], conv_w_pw1[li], conv_b_pw1[li], conv_w_dw[li], conv_b_dw[li],
                                   conv_ln_g[li], conv_ln_b[li], conv_w_pw2[li], conv_b_pw2[li])
            new_conv.append(st)
        else:
            lb = li - N_A_LAYERS
            q = (h @ attn_w_q[lb]).reshape(B, T, N_GROUPS, HEADS_PER_GROUP, HEAD_DIM)
            outs, lses = [], []
            for g in range(N_GROUPS):
                if kv_bufs is None:
                    o, l = prompt_dilated(q[:, :, g], shared[:, :, g], g, biases[g])
                else:
                    o, l = sample_dilated(q[:, :, g], shared[:, :, g], kv_bufs[g], g, biases[g])
                outs.append(o)
                lses.append(l)
            y = merge_groups(outs, lses).reshape(B, T, ATTN_WIDTH) @ attn_w_o[lb]
        x = x + y
        x = x + 0.5 * swiglu_ffn(rmsnorm(x, norms[li, 2]), ffn_w_in[li, 1], ffn_w_out[li, 1])
        if li == N_A_LAYERS - 1:
            shared = (rmsnorm(x, norm_kv) @ w_kv).reshape(B, T, N_GROUPS, 2, HEADS_PER_GROUP, HEAD_DIM)
            for g in range(N_GROUPS):
                kv_g = shared[:, :, g]
                if kv_bufs is None:
                    keep = min(GROUPS[g][0], T)
                    new_kv.append(kv_g[:, T - keep:])
                else:
                    buf_len = kv_bufs[g].shape[1]
                    new_kv.append(jnp.concatenate([kv_bufs[g].astype(kv_g.dtype), kv_g], axis=1)[:, -buf_len:])
    return rmsnorm(x, norm_final), jnp.stack(new_conv, axis=0), new_kv


def setup_inputs(seed: int = 0) -> dict:
    key = jax.random.key(seed)
    ks = jax.random.split(key, 24)

    def nrm(k, shape, scale):
        return jax.random.normal(k, shape, jnp.float32) * scale

    def buf(w):
        return min(w, PAST_LEN)

    kvshape = (2, HEADS_PER_GROUP, HEAD_DIM)
    return {
        "x_prompt": nrm(ks[0], (BATCH, SEQ, D_MODEL), 1.0),
        "x_sample": nrm(ks[1], (DEC_BATCH, DEC_SEQ, D_MODEL), 1.0),
        "state_conv": nrm(ks[2], (N_A_LAYERS, DEC_BATCH, CONV_WIDTH - 1, CONV_INNER), 0.5),
        "cache_kv_w128": nrm(ks[3], (DEC_BATCH, buf(GROUPS[0][0])) + kvshape, 1.0),
        "cache_kv_w512": nrm(ks[4], (DEC_BATCH, buf(GROUPS[1][0])) + kvshape, 1.0),
        "cache_kv_w2048": nrm(ks[5], (DEC_BATCH, buf(GROUPS[2][0])) + kvshape, 1.0),
        "norms": 1.0 + nrm(ks[6], (DEPTH, 3, D_MODEL), 0.02),
        "ffn_w_in": nrm(ks[7], (DEPTH, 2, D_MODEL, 2 * D_FF), D_MODEL ** -0.5),
        "ffn_w_out": nrm(ks[8], (DEPTH, 2, D_FF, D_MODEL), D_FF ** -0.5),
        "conv_w_pw1": nrm(ks[9], (N_A_LAYERS, D_MODEL, 2 * CONV_INNER), D_MODEL ** -0.5),
        "conv_b_pw1": nrm(ks[10], (N_A_LAYERS, 2 * CONV_INNER), 0.02),
        "conv_w_dw": nrm(ks[11], (N_A_LAYERS, CONV_WIDTH, CONV_INNER), CONV_WIDTH ** -0.5),
        "conv_b_dw": nrm(ks[12], (N_A_LAYERS, CONV_INNER), 0.02),
        "conv_ln_g": 1.0 + nrm(ks[13], (N_A_LAYERS, CONV_INNER), 0.02),
        "conv_ln_b": nrm(ks[14], (N_A_LAYERS, CONV_INNER), 0.02),
        "conv_w_pw2": nrm(ks[15], (N_A_LAYERS, CONV_INNER, D_MODEL), CONV_INNER ** -0.5),
        "conv_b_pw2": nrm(ks[16], (N_A_LAYERS, D_MODEL), 0.02),
        "norm_kv": 1.0 + nrm(ks[17], (D_MODEL,), 0.02),
        "w_kv": nrm(ks[18], (D_MODEL, KV_WIDTH), D_MODEL ** -0.5),
        "attn_w_q": nrm(ks[19], (N_B_LAYERS, D_MODEL, Q_WIDTH), D_MODEL ** -0.5),
        "attn_w_o": nrm(ks[20], (N_B_LAYERS, ATTN_WIDTH, D_MODEL), ATTN_WIDTH ** -0.5),
        "rel_bias": nrm(ks[21], (N_BUCKETS, N_GROUPS * HEADS_PER_GROUP), 0.5),
        "norm_final": 1.0 + nrm(ks[22], (D_MODEL,), 0.02),
    }


def reference(x_prompt, x_sample, state_conv, cache_kv_w128, cache_kv_w512, cache_kv_w2048, norms, ffn_w_in,
              ffn_w_out, conv_w_pw1, conv_b_pw1, conv_w_dw, conv_b_dw, conv_ln_g, conv_ln_b, conv_w_pw2, conv_b_pw2,
              norm_kv, w_kv, attn_w_q, attn_w_o, rel_bias, norm_final):
    zeros_conv = jnp.zeros((N_A_LAYERS, x_prompt.shape[0], CONV_WIDTH - 1, CONV_INNER), x_prompt.dtype)
    y_prompt, conv_prompt, kv_prompt = trunk(
        x_prompt, zeros_conv, None, norms, ffn_w_in, ffn_w_out, conv_w_pw1, conv_b_pw1, conv_w_dw, conv_b_dw,
        conv_ln_g, conv_ln_b, conv_w_pw2, conv_b_pw2, norm_kv, w_kv, attn_w_q, attn_w_o, rel_bias, norm_final)
    y_sample, conv_sample, kv_sample = trunk(
        x_sample, state_conv, (cache_kv_w128, cache_kv_w512, cache_kv_w2048), norms, ffn_w_in, ffn_w_out,
        conv_w_pw1, conv_b_pw1, conv_w_dw, conv_b_dw, conv_ln_g, conv_ln_b, conv_w_pw2, conv_b_pw2, norm_kv, w_kv,
        attn_w_q, attn_w_o, rel_bias, norm_final)
    kv128_prompt, kv512_prompt, kv2048_prompt = kv_prompt
    kv128_sample, kv512_sample, kv2048_sample = kv_sample
    return (y_prompt, y_sample, conv_prompt, conv_sample, kv128_prompt, kv128_sample,
            kv512_prompt, kv512_sample, kv2048_prompt, kv2048_sample)
```

```python
import functools
import math

import jax
import jax.numpy as jnp
from jax import lax
from jax.experimental import pallas as pl
from jax.experimental.pallas import tpu as pltpu

F32 = jnp.float32
BF16 = jnp.bfloat16

EPS = 1e-6
NEG_INF = -1e30
GROUPS = ((128, 1), (512, 4), (2048, 16))
N_GROUPS = len(GROUPS)
HEADS_PER_GROUP = 8
HEAD_DIM = 64
ATTN_WIDTH = HEADS_PER_GROUP * HEAD_DIM
N_KEYS = 129
N_BUCKETS = 32
MAX_DISTANCE = 2048
PAST_LEN = 8192
CONV_WIDTH = 31
CONV_HIST = 32
HALF = 256
HEADS_PER_HALF = HALF // HEAD_DIM
QBLOCK = 128
VMEM_LIMIT = 56 * 1024 * 1024


def _params(n_axes):
    return pltpu.CompilerParams(dimension_semantics=("arbitrary",) * n_axes, vmem_limit_bytes=VMEM_LIMIT)


def _const_spec(shape):
    zeros = (0,) * len(shape)
    return pl.BlockSpec(shape, lambda *_: zeros, pipeline_mode=pl.Buffered(1))


def _rms(x, g):
    return x * lax.rsqrt(jnp.mean(x * x, axis=-1, keepdims=True) + EPS) * g


def _silu(x):
    return x * jax.nn.sigmoid(x)


def _ffn_kernel(*refs, d_ff, n_chunk, has_final):
    if has_final:
        x_ref, g_ref, win_ref, wout_ref, gf_ref, o_ref, act_ref = refs
    else:
        x_ref, g_ref, win_ref, wout_ref, o_ref, act_ref = refs
    x = x_ref[...]
    h = _rms(x, g_ref[...]).astype(BF16)
    for c in range(d_ff // n_chunk):
        lo = c * n_chunk
        gate = jnp.dot(h, win_ref[:, lo:lo + n_chunk], preferred_element_type=F32)
        up = jnp.dot(h, win_ref[:, d_ff + lo:d_ff + lo + n_chunk], preferred_element_type=F32)
        act_ref[:, lo:lo + n_chunk] = (_silu(gate) * up).astype(BF16)
    y = jnp.dot(act_ref[...], wout_ref[...], preferred_element_type=F32)
    o = x + 0.5 * y
    if has_final:
        o = _rms(o, gf_ref[...])
    o_ref[...] = o


def _ffn(x, g, w_in, w_out, g_final=None, tm=512):
    n, d = x.shape
    d_ff = w_out.shape[0]
    tm = min(tm, n)
    has_final = g_final is not None
    in_specs = [pl.BlockSpec((tm, d), lambda i: (i, 0)), _const_spec((1, d)),
                _const_spec((d, 2 * d_ff)), _const_spec((d_ff, d))]
    args = [x, g.reshape(1, d), w_in, w_out]
    if has_final:
        in_specs.append(_const_spec((1, d)))
        args.append(g_final.reshape(1, d))
    return pl.pallas_call(
        functools.partial(_ffn_kernel, d_ff=d_ff, n_chunk=256, has_final=has_final),
        grid=(n // tm,),
        in_specs=in_specs,
        out_specs=pl.BlockSpec((tm, d), lambda i: (i, 0)),
        out_shape=jax.ShapeDtypeStruct((n, d), F32),
        scratch_shapes=[pltpu.VMEM((tm, d_ff), BF16)],
        compiler_params=_params(1),
        name="ffn",
    )(*args)


def _proj_kernel(x_ref, g_ref, w_ref, o_ref, *, scale):
    h = _rms(x_ref[...], g_ref[...]).astype(BF16)
    y = jnp.dot(h, w_ref[...], preferred_element_type=F32)
    if scale != 1.0:
        y = y * scale
    o_ref[...] = y.astype(o_ref.dtype)


def _proj(x, g, w, out_dtype, scale=1.0, tm=512):
    n, d = x.shape
    n_out = w.shape[1]
    tm = min(tm, n)
    return pl.pallas_call(
        functools.partial(_proj_kernel, scale=scale),
        grid=(n // tm,),
        in_specs=[pl.BlockSpec((tm, d), lambda i: (i, 0)), _const_spec((1, d)), _const_spec((d, n_out))],
        out_specs=pl.BlockSpec((tm, n_out), lambda i: (i, 0)),
        out_shape=jax.ShapeDtypeStruct((n, n_out), out_dtype),
        compiler_params=_params(1),
        name="proj",
    )(x, g.reshape(1, d), w)


def _conv_kernel(*refs, bb, ts, n_t, rc, has_prev):
    if has_prev:
        (x_ref, prev_ref, g_ref, w1_ref, b1_ref, wdw_ref, bdw_ref, lng_ref, lnb_ref, w2_ref, b2_ref,
         o_ref, st_ref, uext_ref, ush_ref, c_ref) = refs
    else:
        (x_ref, g_ref, w1_ref, b1_ref, wdw_ref, bdw_ref, lng_ref, lnb_ref, w2_ref, b2_ref,
         o_ref, st_ref, uext_ref, ush_ref, c_ref) = refs
    it = pl.program_id(1)
    d = x_ref.shape[-1]
    c_in = w2_ref.shape[0]
    x = x_ref[...].reshape(bb * ts, d)
    h = _rms(x, g_ref[...]).astype(BF16)
    ag = jnp.dot(h, w1_ref[...], preferred_element_type=F32) + b1_ref[...]
    u = ag[:, :c_in] * jax.nn.sigmoid(ag[:, c_in:])

    @pl.when(it == 0)
    def _():
        if has_prev:
            uext_ref[:, 0:CONV_HIST, :] = prev_ref[...]
        else:
            uext_ref[:, 0:CONV_HIST, :] = jnp.zeros((bb, CONV_HIST, c_in), F32)

    uext_ref[:, CONV_HIST:CONV_HIST + ts, :] = u.reshape(bb, ts, c_in)

    lead = CONV_HIST - (CONV_WIDTH - 1)
    for s in range(8):
        n_sh = ts + 8 * ((CONV_WIDTH - 1 - s) // 8)
        ush_ref[s, :, 0:n_sh, :] = uext_ref[:, lead + s:lead + s + n_sh, :]

    def chunk(ci, carry):
        r0 = pl.multiple_of(ci * rc, rc)
        acc = jnp.broadcast_to(bdw_ref[...].reshape(1, 1, c_in), (bb, rc, c_in))
        for k in range(CONV_WIDTH):
            win = ush_ref[k % 8, :, pl.ds(r0 + 8 * (k // 8), rc), :]
            acc = acc + win * wdw_ref[k:k + 1, :].reshape(1, 1, c_in)
        mu = jnp.mean(acc, axis=-1, keepdims=True)
        cen = acc - mu
        var = jnp.mean(cen * cen, axis=-1, keepdims=True)
        y = cen * lax.rsqrt(var + EPS) * lng_ref[...].reshape(1, 1, c_in) + lnb_ref[...].reshape(1, 1, c_in)
        c_ref[:, pl.ds(r0, rc), :] = _silu(y).astype(BF16)
        return carry

    lax.fori_loop(0, ts // rc, chunk, 0)

    y = jnp.dot(c_ref[...].reshape(bb * ts, c_in), w2_ref[...], preferred_element_type=F32) + b2_ref[...]
    o_ref[...] = (x + y).reshape(bb, ts, d)

    @pl.when(it == n_t - 1)
    def _():
        st_ref[...] = uext_ref[:, ts:ts + CONV_HIST, :]

    if n_t > 1:
        uext_ref[:, 0:CONV_HIST, :] = uext_ref[:, ts:ts + CONV_HIST, :]


def _conv_module(x, prev, g, w1, b1, wdw, bdw, lng, lnb, w2, b2, bb, ts):
    b, t, d = x.shape
    c_in = w2.shape[0]
    n_t = t // ts
    rc = min(ts, 16)
    has_prev = prev is not None
    row = lambda v: v.reshape(1, -1)
    in_specs = [pl.BlockSpec((bb, ts, d), lambda i, j: (i, j, 0))]
    args = [x]
    if has_prev:
        pad = CONV_HIST - prev.shape[1]
        in_specs.append(pl.BlockSpec((bb, CONV_HIST, c_in), lambda i, j: (i, 0, 0)))
        args.append(jnp.pad(prev, ((0, 0), (pad, 0), (0, 0))))
    in_specs += [_const_spec((1, d)), _const_spec((d, 2 * c_in)), _const_spec((1, 2 * c_in)),
                 _const_spec((CONV_WIDTH, c_in)), _const_spec((1, c_in)), _const_spec((1, c_in)),
                 _const_spec((1, c_in)), _const_spec((c_in, d)), _const_spec((1, d))]
    args += [row(g), w1, row(b1), wdw, row(bdw), row(lng), row(lnb), w2, row(b2)]
    out, st = pl.pallas_call(
        functools.partial(_conv_kernel, bb=bb, ts=ts, n_t=n_t, rc=rc, has_prev=has_prev),
        grid=(b // bb, n_t),
        in_specs=in_specs,
        out_specs=[pl.BlockSpec((bb, ts, d), lambda i, j: (i, j, 0)),
                   pl.BlockSpec((bb, CONV_HIST, c_in), lambda i, j: (i, 0, 0))],
        out_shape=[jax.ShapeDtypeStruct((b, t, d), F32), jax.ShapeDtypeStruct((b, CONV_HIST, c_in), F32)],
        scratch_shapes=[pltpu.VMEM((bb, ts + CONV_HIST, c_in), F32),
                        pltpu.VMEM((8, bb, ts + 24, c_in), F32),
                        pltpu.VMEM((bb, ts, c_in), BF16)],
        compiler_params=_params(2),
        name="conv_module",
    )(*args)
    return out, st[:, CONV_HIST - (CONV_WIDTH - 1):]


def _t5_bucket(dist):
    max_exact = N_BUCKETS // 2
    d_f = jnp.maximum(dist, 1).astype(F32)
    large = max_exact + (jnp.log(d_f / max_exact) / math.log(MAX_DISTANCE / max_exact)
                         * (N_BUCKETS - max_exact)).astype(jnp.int32)
    large = jnp.minimum(large, N_BUCKETS - 1)
    return jnp.where(dist < max_exact, dist, large)


def _group_bias(rel_bias, g):
    _, d = GROUPS[g]
    dist = d * jnp.arange(N_KEYS, dtype=jnp.int32)
    b = jnp.take(rel_bias, _t5_bucket(dist), axis=0)
    return b[:, g * HEADS_PER_GROUP:(g + 1) * HEADS_PER_GROUP].T.astype(F32)


def _bias_rows(bias_g, j, valid):
    jc = jnp.clip(j, 0, N_KEYS - 1)
    t = jnp.where(valid[None], bias_g[:, jc], NEG_INF)
    return t.reshape(2, HEADS_PER_HALF * j.shape[0], j.shape[1])


def _prompt_bias_tiles(bias_g):
    a = jnp.arange(QBLOCK, dtype=jnp.int32)[:, None]
    c = jnp.arange(2 * QBLOCK, dtype=jnp.int32)[None, :]
    j = a + QBLOCK - c
    valid = (j >= 0) & (j < N_KEYS)
    regular = _bias_rows(bias_g, j, valid)
    first = _bias_rows(bias_g, j, valid & (c >= QBLOCK))
    return jnp.stack([regular, first], axis=0)


def _stack_heads(x, lane_head):
    return jnp.concatenate([jnp.where(lane_head == h, x, jnp.zeros_like(x)) for h in range(HEADS_PER_HALF)], axis=0)


def _unstack_heads(x, lane_head, q):
    out = jnp.where(lane_head == 0, x[0:q], 0.0)
    for h in range(1, HEADS_PER_HALF):
        out = out + jnp.where(lane_head == h, x[h * q:(h + 1) * q], 0.0)
    return out


def _attn_prompt_kernel(q_ref, kp_ref, kc_ref, vp_ref, vc_ref, bias_ref, o_ref, lse_ref):
    qb = q_ref.shape[1]
    q = q_ref[0]
    k = jnp.concatenate([kp_ref[0], kc_ref[0]], axis=0).astype(BF16)
    v = jnp.concatenate([vp_ref[0], vc_ref[0]], axis=0).astype(BF16)
    lane_head = lax.broadcasted_iota(jnp.int32, (qb, HALF), 1) // HEAD_DIM
    lane = lax.broadcasted_iota(jnp.int32, (qb, 128), 1)
    lse_tile = jnp.zeros((qb, 128), F32)
    for half in range(2):
        lanes = slice(half * HALF, (half + 1) * HALF)
        qs = _stack_heads(q[:, lanes], lane_head)
        s = lax.dot_general(qs, k[:, lanes], (((1,), (1,)), ((), ())), preferred_element_type=F32)
        s = s + bias_ref[0, half]
        m = jnp.max(s, axis=-1, keepdims=True)
        p = jnp.exp(s - m)
        den = jnp.sum(p, axis=-1, keepdims=True)
        ov = jnp.dot((p / den).astype(BF16), v[:, lanes], preferred_element_type=F32)
        o_ref[0, :, lanes] = _unstack_heads(ov, lane_head, qb).astype(o_ref.dtype)
        lse = m + jnp.log(den)
        for h in range(HEADS_PER_HALF):
            lse_tile = jnp.where(lane == half * HEADS_PER_HALF + h, lse[h * qb:(h + 1) * qb], lse_tile)
    lse_ref[0] = lse_tile


def _attn_prompt(q, kv, bias_tiles, g):
    b, s, _ = q.shape
    _, d = GROUPS[g]
    sd = s // d
    qb = QBLOCK
    blk = ATTN_WIDTH
    qv = q.reshape(b, sd, d * N_GROUPS * blk)
    kvv = kv.reshape(b, sd, d * N_GROUPS * 2 * blk)
    prev = lambda j: jnp.maximum(j - 1, 0)
    o, lse = pl.pallas_call(
        _attn_prompt_kernel,
        grid=(b, d, sd // qb),
        in_specs=[
            pl.BlockSpec((1, qb, blk), lambda i, r, j: (i, j, r * N_GROUPS + g)),
            pl.BlockSpec((1, qb, blk), lambda i, r, j: (i, prev(j), r * 2 * N_GROUPS + 2 * g)),
            pl.BlockSpec((1, qb, blk), lambda i, r, j: (i, j, r * 2 * N_GROUPS + 2 * g)),
            pl.BlockSpec((1, qb, blk), lambda i, r, j: (i, prev(j), r * 2 * N_GROUPS + 2 * g + 1)),
            pl.BlockSpec((1, qb, blk), lambda i, r, j: (i, j, r * 2 * N_GROUPS + 2 * g + 1)),
            pl.BlockSpec((1, 2, HEADS_PER_HALF * qb, 2 * qb), lambda i, r, j: (jnp.where(j == 0, 1, 0), 0, 0, 0)),
        ],
        out_specs=[pl.BlockSpec((1, qb, blk), lambda i, r, j: (i, j, r)),
                   pl.BlockSpec((1, qb, 128), lambda i, r, j: (i, j, r))],
        out_shape=[jax.ShapeDtypeStruct((b, sd, d * blk), BF16), jax.ShapeDtypeStruct((b, sd, d * 128), F32)],
        compiler_params=_params(3),
        name=f"attn_prompt_g{g}",
    )(qv, kvv, kvv, kvv, kvv, bias_tiles)
    return o.reshape(b * s, blk), lse.reshape(b * s, 128)


def _expand_heads(w, e_ref):
    hi = w.astype(BF16)
    lo = (w - hi.astype(F32)).astype(BF16)
    e = e_ref[...]
    return jnp.dot(hi, e, preferred_element_type=F32) + jnp.dot(lo, e, preferred_element_type=F32)


def _merge_kernel(x_ref, o0_ref, o1_ref, o2_ref, l0_ref, l1_ref, l2_ref, e_ref, wo_ref, out_ref):
    ls = [l0_ref[...], l1_ref[...], l2_ref[...]]
    mx = jnp.maximum(jnp.maximum(ls[0], ls[1]), ls[2])
    es = [jnp.exp(l - mx) for l in ls]
    tot = es[0] + es[1] + es[2]
    merged = None
    for e, o_ref in zip(es, (o0_ref, o1_ref, o2_ref)):
        term = _expand_heads(e / tot, e_ref) * o_ref[...].astype(F32)
        merged = term if merged is None else merged + term
    y = jnp.dot(merged.astype(BF16), wo_ref[...], preferred_element_type=F32)
    out_ref[...] = x_ref[...] + y


def _head_expander():
    head = jnp.arange(128, dtype=jnp.int32)[:, None]
    lane = jnp.arange(ATTN_WIDTH, dtype=jnp.int32)[None, :]
    return (lane // HEAD_DIM == head).astype(BF16)


def _merge(x, outs, lses, w_o, tm=512):
    n, d = x.shape
    tm = min(tm, n)
    tok = lambda width: pl.BlockSpec((tm, width), lambda i: (i, 0))
    return pl.pallas_call(
        _merge_kernel,
        grid=(n // tm,),
        in_specs=[tok(d)] + [tok(ATTN_WIDTH)] * 3 + [tok(128)] * 3 + [_const_spec((128, ATTN_WIDTH)),
                                                                       _const_spec((ATTN_WIDTH, d))],
        out_specs=tok(d),
        out_shape=jax.ShapeDtypeStruct((n, d), F32),
        compiler_params=_params(1),
        name="merge",
    )(x, *outs, *lses, _head_expander(), w_o)


def _cache_update_kernel(buf_ref, new_ref, out_ref):
    w = buf_ref.shape[1]
    t = new_ref.shape[1]
    out_ref[0, 0:w - t] = buf_ref[0, t:w]
    out_ref[0, w - t:w] = new_ref[0]


def _cache_update(buf, kv_new, g):
    b, w, c = buf.shape
    t = kv_new.shape[1]
    return pl.pallas_call(
        _cache_update_kernel,
        grid=(b,),
        in_specs=[pl.BlockSpec((1, w, c), lambda i: (i, 0, 0)), pl.BlockSpec((1, t, c), lambda i: (i, 0, g))],
        out_specs=pl.BlockSpec((1, w, c), lambda i: (i, 0, 0)),
        out_shape=jax.ShapeDtypeStruct((b, w, c), F32),
        compiler_params=_params(1),
        name=f"cache_update_g{g}",
    )(buf, kv_new)


def _sample_classes(g):
    return min(GROUPS[g][1], 8)


def _sample_bias_tiles(bias_g, g, t_new):
    w, d = GROUPS[g]
    t = jnp.arange(t_new, dtype=jnp.int32)[:, None]
    i = jnp.arange(128, dtype=jnp.int32)[None, :]
    min_valid = 2 * w - PAST_LEN
    tiles = []
    for r in range(_sample_classes(g)):
        row = d * i + r
        dist = w + t - row
        valid = (dist % d == 0) & (dist // d < N_KEYS) & (w + row >= min_valid)
        tiles.append(_bias_rows(bias_g, dist // d, valid))
    dist = t - i
    valid = (i < t_new) & (dist >= 0) & (dist % d == 0) & (dist // d < N_KEYS)
    return jnp.stack(tiles, axis=0), _bias_rows(bias_g, dist // d, valid)


def _attn_sample_kernel(q_ref, kvn_ref, c0_ref, c1_ref, c2_ref, ba0_ref, ba1_ref, ba2_ref, bn0_ref, bn1_ref, bn2_ref,
                        o_ref):
    t = q_ref.shape[1]
    lane_head = lax.broadcasted_iota(jnp.int32, (t, HALF), 1) // HEAD_DIM
    q = q_ref[0]
    kvn = kvn_ref[0]
    pad = jnp.zeros((128 - t, HALF), F32)
    caches = (c0_ref, c1_ref, c2_ref)
    bias_a = (ba0_ref, ba1_ref, ba2_ref)
    bias_n = (bn0_ref, bn1_ref, bn2_ref)
    for half in range(2):
        outs, lses = [], []
        for g in range(N_GROUPS):
            q_lo = g * ATTN_WIDTH + half * HALF
            qs = _stack_heads(q[:, q_lo:q_lo + HALF], lane_head)
            k_lo = g * 2 * ATTN_WIDTH + half * HALF
            v_lo = k_lo + ATTN_WIDTH
            keys = [jnp.concatenate([kvn[:, k_lo:k_lo + HALF], pad], axis=0).astype(BF16)]
            vals = [jnp.concatenate([kvn[:, v_lo:v_lo + HALF], pad], axis=0).astype(BF16)]
            biases = [bias_n[g][half]]
            for r in range(_sample_classes(g)):
                base = r * 2 * ATTN_WIDTH + half * HALF
                keys.append(caches[g][0, :, base:base + HALF].astype(BF16))
                vals.append(caches[g][0, :, base + ATTN_WIDTH:base + ATTN_WIDTH + HALF].astype(BF16))
                biases.append(bias_a[g][r, half])
            ss = [lax.dot_general(qs, kk, (((1,), (1,)), ((), ())), preferred_element_type=F32) + bb
                  for kk, bb in zip(keys, biases)]
            m = ss[0].max(axis=-1, keepdims=True)
            for s in ss[1:]:
                m = jnp.maximum(m, s.max(axis=-1, keepdims=True))
            ps = [jnp.exp(s - m) for s in ss]
            den = ps[0].sum(axis=-1, keepdims=True)
            for p in ps[1:]:
                den = den + p.sum(axis=-1, keepdims=True)
            ov = None
            for p, vv in zip(ps, vals):
                term = jnp.dot((p / den).astype(BF16), vv, preferred_element_type=F32)
                ov = term if ov is None else ov + term
            outs.append(ov)
            lses.append(m + jnp.log(den))
        mx = jnp.maximum(jnp.maximum(lses[0], lses[1]), lses[2])
        es = [jnp.exp(l - mx) for l in lses]
        tot = es[0] + es[1] + es[2]
        merged = (es[0] / tot) * outs[0] + (es[1] / tot) * outs[1] + (es[2] / tot) * outs[2]
        o_ref[0, :, half * HALF:(half + 1) * HALF] = _unstack_heads(merged, lane_head, t).astype(o_ref.dtype)


def _attn_sample(q, kv_new, caches, bias_gs):
    b, t, _ = q.shape
    in_specs = [pl.BlockSpec((1, t, q.shape[2]), lambda i: (i, 0, 0)),
                pl.BlockSpec((1, t, kv_new.shape[2]), lambda i: (i, 0, 0))]
    args = [q, kv_new]
    tiles_a, tiles_n = [], []
    for g in range(N_GROUPS):
        w, d = GROUPS[g]
        n_cls = _sample_classes(g)
        row_w = 2 * ATTN_WIDTH
        in_specs.append(pl.BlockSpec((1, w // d, n_cls * row_w), lambda i: (i, 0, 0)))
        args.append(caches[g].reshape(b, w // d, d * row_w))
        ta, tn = _sample_bias_tiles(bias_gs[g], g, t)
        tiles_a.append(ta)
        tiles_n.append(tn)
    for tile in tiles_a + tiles_n:
        in_specs.append(_const_spec(tile.shape))
        args.append(tile)
    o = pl.pallas_call(
        _attn_sample_kernel,
        grid=(b,),
        in_specs=in_specs,
        out_specs=pl.BlockSpec((1, t, ATTN_WIDTH), lambda i: (i, 0, 0)),
        out_shape=jax.ShapeDtypeStruct((b, t, ATTN_WIDTH), BF16),
        compiler_params=_params(1),
        name="attn_sample",
    )(*args)
    return o.reshape(b * t, ATTN_WIDTH)


def _out_proj_kernel(x_ref, a_ref, wo_ref, out_ref):
    out_ref[...] = x_ref[...] + jnp.dot(a_ref[...], wo_ref[...], preferred_element_type=F32)


def _out_proj(x, a, w_o):
    n, d = x.shape
    return pl.pallas_call(
        _out_proj_kernel,
        out_shape=jax.ShapeDtypeStruct((n, d), F32),
        compiler_params=pltpu.CompilerParams(vmem_limit_bytes=VMEM_LIMIT),
        name="out_proj",
    )(x, a, w_o)


def _trunk(x, conv_prev, kv_bufs, wts, conv_tile):
    b, t, d = x.shape
    n = b * t
    xf = x.reshape(n, d)
    norms = wts["norms"]
    xf = _ffn(xf, norms[0, 0], wts["ffn_w_in"][0, 0], wts["ffn_w_out"][0, 0])
    bb, ts = conv_tile
    x3, conv_state = _conv_module(
        xf.reshape(b, t, d), conv_prev, norms[0, 1], wts["conv_w_pw1"][0], wts["conv_b_pw1"][0], wts["conv_w_dw"][0],
        wts["conv_b_dw"][0], wts["conv_ln_g"][0], wts["conv_ln_b"][0], wts["conv_w_pw2"][0], wts["conv_b_pw2"][0],
        bb, ts)
    xf = x3.reshape(n, d)
    xf = _ffn(xf, norms[0, 2], wts["ffn_w_in"][0, 1], wts["ffn_w_out"][0, 1])
    shared = _proj(xf, wts["norm_kv"], wts["w_kv"], F32).reshape(b, t, -1)
    xf = _ffn(xf, norms[1, 0], wts["ffn_w_in"][1, 0], wts["ffn_w_out"][1, 0])
    q = _proj(xf, norms[1, 1], wts["attn_w_q"][0], BF16, scale=HEAD_DIM ** -0.5).reshape(b, t, -1)
    bias_gs = [_group_bias(wts["rel_bias"], g) for g in range(N_GROUPS)]
    kv_shape = (2, HEADS_PER_GROUP, HEAD_DIM)
    row_w = 2 * ATTN_WIDTH
    if kv_bufs is None:
        outs, lses = [], []
        for g in range(N_GROUPS):
            o, lse = _attn_prompt(q, shared, _prompt_bias_tiles(bias_gs[g]), g)
            outs.append(o)
            lses.append(lse)
        xf = _merge(xf, outs, lses, wts["attn_w_o"][0])
        new_kv = [shared[:, t - min(GROUPS[g][0], t):, g * row_w:(g + 1) * row_w].reshape(b, -1, *kv_shape)
                  for g in range(N_GROUPS)]
    else:
        caches = [buf.reshape(b, buf.shape[1], row_w) for buf in kv_bufs]
        attn = _attn_sample(q, shared, caches, bias_gs)
        xf = _out_proj(xf, attn, wts["attn_w_o"][0])
        new_kv = [_cache_update(caches[g], shared, g).reshape(b, -1, *kv_shape) for g in range(N_GROUPS)]
    xf = _ffn(xf, norms[1, 2], wts["ffn_w_in"][1, 1], wts["ffn_w_out"][1, 1], g_final=wts["norm_final"])
    return xf.reshape(b, t, d), conv_state[None], new_kv


def kernel(x_prompt, x_sample, state_conv, cache_kv_w128, cache_kv_w512, cache_kv_w2048, norms, ffn_w_in, ffn_w_out,
           conv_w_pw1, conv_b_pw1, conv_w_dw, conv_b_dw, conv_ln_g, conv_ln_b, conv_w_pw2, conv_b_pw2, norm_kv, w_kv,
           attn_w_q, attn_w_o, rel_bias, norm_final):
    wts = dict(
        norms=norms, ffn_w_in=ffn_w_in.astype(BF16), ffn_w_out=ffn_w_out.astype(BF16),
        conv_w_pw1=conv_w_pw1.astype(BF16), conv_b_pw1=conv_b_pw1, conv_w_dw=conv_w_dw, conv_b_dw=conv_b_dw,
        conv_ln_g=conv_ln_g, conv_ln_b=conv_ln_b, conv_w_pw2=conv_w_pw2.astype(BF16), conv_b_pw2=conv_b_pw2,
        norm_kv=norm_kv, w_kv=w_kv.astype(BF16), attn_w_q=attn_w_q.astype(BF16), attn_w_o=attn_w_o.astype(BF16),
        rel_bias=rel_bias, norm_final=norm_final)
    y_prompt, conv_prompt, kv_prompt = _trunk(x_prompt, None, None, wts, conv_tile=(1, 512))
    y_sample, conv_sample, kv_sample = _trunk(
        x_sample, state_conv[0], (cache_kv_w128, cache_kv_w512, cache_kv_w2048), wts,
        conv_tile=(8, x_sample.shape[1]))
    return (y_prompt, y_sample, conv_prompt, conv_sample, kv_prompt[0], kv_sample[0],
            kv_prompt[1], kv_sample[1], kv_prompt[2], kv_sample[2])
```

```python
import functools
import math

import jax
import jax.numpy as jnp
from jax import lax
from jax.experimental import pallas as pl
from jax.experimental.pallas import tpu as pltpu

F32 = jnp.float32
BF16 = jnp.bfloat16

EPS = 1e-6
NEG_INF = -1e30
GROUPS = ((128, 1), (512, 4), (2048, 16))
N_GROUPS = len(GROUPS)
HEADS_PER_GROUP = 8
HEAD_DIM = 64
ATTN_WIDTH = HEADS_PER_GROUP * HEAD_DIM
ROW_WIDTH = 2 * ATTN_WIDTH
N_KEYS = 129
N_BUCKETS = 32
MAX_DISTANCE = 2048
PAST_LEN = 8192
CONV_WIDTH = 31
CONV_HIST = 32
LANES = 128
SUBLANES = 8
HALF = 256
HEADS_PER_HALF = HALF // HEAD_DIM
QBLOCK = 128
TOKEN_TILE = 512
CONV_ROWS = 64
VMEM_LIMIT = 56 * 1024 * 1024


def _params(n_axes):
    return pltpu.CompilerParams(dimension_semantics=("arbitrary",) * n_axes, vmem_limit_bytes=VMEM_LIMIT)


def _const_spec(shape):
    zeros = (0,) * len(shape)
    return pl.BlockSpec(shape, lambda *_: zeros, pipeline_mode=pl.Buffered(1))


def _rms(x, g):
    return x * lax.rsqrt(jnp.mean(x * x, axis=-1, keepdims=True) + EPS) * g


def _silu(x):
    return x * jax.nn.sigmoid(x)


def _ffn_kernel(*refs, d_ff, n_chunk, has_final):
    if has_final:
        x_ref, g_ref, win_ref, wout_ref, gf_ref, o_ref, act_ref = refs
    else:
        x_ref, g_ref, win_ref, wout_ref, o_ref, act_ref = refs
    x = x_ref[...]
    h = _rms(x, g_ref[...]).astype(BF16)
    for c in range(d_ff // n_chunk):
        lo = c * n_chunk
        gate = jnp.dot(h, win_ref[:, lo:lo + n_chunk], preferred_element_type=F32)
        up = jnp.dot(h, win_ref[:, d_ff + lo:d_ff + lo + n_chunk], preferred_element_type=F32)
        act_ref[:, lo:lo + n_chunk] = (_silu(gate) * up).astype(BF16)
    y = jnp.dot(act_ref[...], wout_ref[...], preferred_element_type=F32)
    o = x + 0.5 * y
    if has_final:
        o = _rms(o, gf_ref[...])
    o_ref[...] = o


def _ffn(x, g, w_in, w_out, g_final=None):
    n, d = x.shape
    d_ff = w_out.shape[0]
    tm = min(TOKEN_TILE, n)
    has_final = g_final is not None
    in_specs = [pl.BlockSpec((tm, d), lambda i: (i, 0)), _const_spec((1, d)),
                _const_spec((d, 2 * d_ff)), _const_spec((d_ff, d))]
    args = [x, g.reshape(1, d), w_in, w_out]
    if has_final:
        in_specs.append(_const_spec((1, d)))
        args.append(g_final.reshape(1, d))
    return pl.pallas_call(
        functools.partial(_ffn_kernel, d_ff=d_ff, n_chunk=HALF, has_final=has_final),
        grid=(n // tm,),
        in_specs=in_specs,
        out_specs=pl.BlockSpec((tm, d), lambda i: (i, 0)),
        out_shape=jax.ShapeDtypeStruct((n, d), F32),
        scratch_shapes=[pltpu.VMEM((tm, d_ff), BF16)],
        compiler_params=_params(1),
        name="ffn",
    )(*args)


def _proj_kernel(x_ref, g_ref, w_ref, o_ref, *, scale):
    h = _rms(x_ref[...], g_ref[...]).astype(BF16)
    y = jnp.dot(h, w_ref[...], preferred_element_type=F32)
    if scale != 1.0:
        y = y * scale
    o_ref[...] = y.astype(o_ref.dtype)


def _proj(x, g, w, out_dtype, scale=1.0):
    n, d = x.shape
    n_out = w.shape[1]
    tm = min(TOKEN_TILE, n)
    return pl.pallas_call(
        functools.partial(_proj_kernel, scale=scale),
        grid=(n // tm,),
        in_specs=[pl.BlockSpec((tm, d), lambda i: (i, 0)), _const_spec((1, d)), _const_spec((d, n_out))],
        out_specs=pl.BlockSpec((tm, n_out), lambda i: (i, 0)),
        out_shape=jax.ShapeDtypeStruct((n, n_out), out_dtype),
        compiler_params=_params(1),
        name="proj",
    )(x, g.reshape(1, d), w)


def _kv_prompt_kernel(x_ref, g_ref, w_ref, sh_ref, t0_ref, t1_ref, t2_ref, *, n_t, tm):
    j = pl.program_id(1)
    h = _rms(x_ref[0], g_ref[...]).astype(BF16)
    y = jnp.dot(h, w_ref[...], preferred_element_type=F32)
    sh_ref[0] = y.astype(sh_ref.dtype)
    for g, t_ref in enumerate((t0_ref, t1_ref, t2_ref)):
        keep = min(GROUPS[g][0], n_t * tm)
        rows = min(keep, tm)

        @pl.when(j >= n_t - max(keep // tm, 1))
        def _(g=g, t_ref=t_ref, rows=rows):
            t_ref[0] = y[tm - rows:, g * ROW_WIDTH:(g + 1) * ROW_WIDTH].T


def _kv_prompt(x, g, w):
    b, s, d = x.shape
    n_out = w.shape[1]
    tm = min(TOKEN_TILE, s)
    n_t = s // tm
    out_specs = [pl.BlockSpec((1, tm, n_out), lambda i, j: (i, j, 0))]
    out_shape = [jax.ShapeDtypeStruct((b, s, n_out), BF16)]
    for grp in range(N_GROUPS):
        keep = min(GROUPS[grp][0], s)
        first = n_t - max(keep // tm, 1)
        out_specs.append(pl.BlockSpec((1, ROW_WIDTH, min(keep, tm)),
                                      lambda i, j, first=first: (i, 0, jnp.maximum(j - first, 0))))
        out_shape.append(jax.ShapeDtypeStruct((b, ROW_WIDTH, keep), F32))
    return pl.pallas_call(
        functools.partial(_kv_prompt_kernel, n_t=n_t, tm=tm),
        grid=(b, n_t),
        in_specs=[pl.BlockSpec((1, tm, d), lambda i, j: (i, j, 0)), _const_spec((1, d)), _const_spec((d, n_out))],
        out_specs=out_specs,
        out_shape=out_shape,
        compiler_params=_params(2),
        name="kv_prompt",
    )(x, g.reshape(1, d), w)


def _conv_kernel(*refs, bb, ts, n_t, rc, has_prev):
    if has_prev:
        (x_ref, prev_ref, g_ref, w1_ref, b1_ref, wdw_ref, bdw_ref, lng_ref, lnb_ref, w2_ref, b2_ref,
         o_ref, st_ref, uext_ref, ush_ref, conv_ref, c_ref) = refs
    else:
        (x_ref, g_ref, w1_ref, b1_ref, wdw_ref, bdw_ref, lng_ref, lnb_ref, w2_ref, b2_ref,
         o_ref, st_ref, uext_ref, ush_ref, conv_ref, c_ref) = refs
    it = pl.program_id(1)
    d = x_ref.shape[-1]
    c_in = w2_ref.shape[0]
    x = x_ref[...].reshape(bb * ts, d)
    h = _rms(x, g_ref[...]).astype(BF16)
    ag = jnp.dot(h, w1_ref[...], preferred_element_type=F32) + b1_ref[...]
    u = ag[:, :c_in] * jax.nn.sigmoid(ag[:, c_in:])

    @pl.when(it == 0)
    def _():
        if has_prev:
            uext_ref[:, 0:CONV_HIST, :] = prev_ref[...]
        else:
            uext_ref[:, 0:CONV_HIST, :] = jnp.zeros((bb, CONV_HIST, c_in), F32)

    uext_ref[:, CONV_HIST:CONV_HIST + ts, :] = u.reshape(bb, ts, c_in)

    lead = CONV_HIST - (CONV_WIDTH - 1)
    n_lb = c_in // LANES
    for s in range(SUBLANES):
        n_sh = ts + SUBLANES * ((CONV_WIDTH - 1 - s) // SUBLANES)
        for lb in range(n_lb):
            ush_ref[s, lb, :, 0:n_sh, :] = uext_ref[:, lead + s:lead + s + n_sh, lb * LANES:(lb + 1) * LANES]

    reps = rc // SUBLANES
    for lb in range(n_lb):
        lanes = slice(lb * LANES, (lb + 1) * LANES)
        taps = [wdw_ref[k, :, lanes] for k in range(CONV_WIDTH)]
        bias = bdw_ref[:, lanes]

        def chunk(ci, carry, lb=lb, lanes=lanes, taps=taps, bias=bias):
            r0 = pl.multiple_of(ci * rc, rc)
            acc = jnp.broadcast_to(bias.reshape(1, 1, LANES), (bb, rc, LANES))
            for k in range(CONV_WIDTH):
                win = ush_ref[k % SUBLANES, lb, :, pl.ds(r0 + SUBLANES * (k // SUBLANES), rc), :]
                wk = jnp.concatenate([taps[k]] * reps, axis=0) if reps > 1 else taps[k]
                acc = acc + win * wk[None]
            conv_ref[:, pl.ds(r0, rc), lanes] = acc
            return carry

        lax.fori_loop(0, ts // rc, chunk, 0)

    ln_rows = min(ts, 32)

    def ln_chunk(ci, carry):
        r0 = pl.multiple_of(ci * ln_rows, ln_rows)
        acc = conv_ref[:, pl.ds(r0, ln_rows), :]
        mu = jnp.mean(acc, axis=-1, keepdims=True)
        cen = acc - mu
        var = jnp.mean(cen * cen, axis=-1, keepdims=True)
        y = cen * lax.rsqrt(var + EPS) * lng_ref[...].reshape(1, 1, c_in) + lnb_ref[...].reshape(1, 1, c_in)
        c_ref[:, pl.ds(r0, ln_rows), :] = _silu(y).astype(BF16)
        return carry

    lax.fori_loop(0, ts // ln_rows, ln_chunk, 0)

    y = jnp.dot(c_ref[...].reshape(bb * ts, c_in), w2_ref[...], preferred_element_type=F32) + b2_ref[...]
    o_ref[...] = (x + y).reshape(bb, ts, d)

    @pl.when(it == n_t - 1)
    def _():
        st_ref[...] = uext_ref[:, ts:ts + CONV_HIST, :]

    if n_t > 1:
        uext_ref[:, 0:CONV_HIST, :] = uext_ref[:, ts:ts + CONV_HIST, :]


def _conv_module(x, prev, g, w1, b1, wdw, bdw, lng, lnb, w2, b2, bb, ts):
    b, t, d = x.shape
    c_in = w2.shape[0]
    n_t = t // ts
    rc = min(ts, CONV_ROWS)
    has_prev = prev is not None
    row = lambda v: v.reshape(1, -1)
    in_specs = [pl.BlockSpec((bb, ts, d), lambda i, j: (i, j, 0))]
    args = [x]
    if has_prev:
        pad = CONV_HIST - prev.shape[1]
        in_specs.append(pl.BlockSpec((bb, CONV_HIST, c_in), lambda i, j: (i, 0, 0)))
        args.append(jnp.pad(prev, ((0, 0), (pad, 0), (0, 0))))
    in_specs += [_const_spec((1, d)), _const_spec((d, 2 * c_in)), _const_spec((1, 2 * c_in)),
                 _const_spec((CONV_WIDTH, SUBLANES, c_in)), _const_spec((1, c_in)), _const_spec((1, c_in)),
                 _const_spec((1, c_in)), _const_spec((c_in, d)), _const_spec((1, d))]
    wdw_rep = jnp.broadcast_to(wdw[:, None, :], (CONV_WIDTH, SUBLANES, c_in))
    args += [row(g), w1, row(b1), wdw_rep, row(bdw), row(lng), row(lnb), w2, row(b2)]
    out, st = pl.pallas_call(
        functools.partial(_conv_kernel, bb=bb, ts=ts, n_t=n_t, rc=rc, has_prev=has_prev),
        grid=(b // bb, n_t),
        in_specs=in_specs,
        out_specs=[pl.BlockSpec((bb, ts, d), lambda i, j: (i, j, 0)),
                   pl.BlockSpec((bb, CONV_HIST, c_in), lambda i, j: (i, 0, 0))],
        out_shape=[jax.ShapeDtypeStruct((b, t, d), F32), jax.ShapeDtypeStruct((b, CONV_HIST, c_in), F32)],
        scratch_shapes=[pltpu.VMEM((bb, ts + CONV_HIST, c_in), F32),
                        pltpu.VMEM((SUBLANES, c_in // LANES, bb, ts + 24, LANES), F32),
                        pltpu.VMEM((bb, ts, c_in), F32),
                        pltpu.VMEM((bb, ts, c_in), BF16)],
        compiler_params=_params(2),
        name="conv_module",
    )(*args)
    return out, st[:, CONV_HIST - (CONV_WIDTH - 1):]


def _t5_bucket(dist):
    max_exact = N_BUCKETS // 2
    d_f = jnp.maximum(dist, 1).astype(F32)
    large = max_exact + (jnp.log(d_f / max_exact) / math.log(MAX_DISTANCE / max_exact)
                         * (N_BUCKETS - max_exact)).astype(jnp.int32)
    large = jnp.minimum(large, N_BUCKETS - 1)
    return jnp.where(dist < max_exact, dist, large)


def _group_bias(rel_bias, g):
    _, d = GROUPS[g]
    dist = d * jnp.arange(N_KEYS, dtype=jnp.int32)
    b = jnp.take(rel_bias, _t5_bucket(dist), axis=0)
    return b[:, g * HEADS_PER_GROUP:(g + 1) * HEADS_PER_GROUP].T.astype(F32)


def _toeplitz(p, n_rows, n_cols):
    length = p.shape[1]
    stride = length - 1
    flat = jnp.tile(p, (1, n_rows))[:, :n_rows * stride]
    return flat.reshape(p.shape[0], n_rows, stride)[:, :, :n_cols]


def _stack_rows(t):
    return t.reshape(2, HEADS_PER_HALF * t.shape[1], t.shape[2])


def _prompt_bias_tiles(bias_g):
    neg = jnp.full((bias_g.shape[0], 2 * QBLOCK + 1 - N_KEYS), NEG_INF, F32)
    p = jnp.concatenate([bias_g[:, ::-1], neg], axis=1)
    regular = _toeplitz(p, QBLOCK, 2 * QBLOCK)
    c = jnp.arange(2 * QBLOCK, dtype=jnp.int32)[None, None, :]
    first = jnp.where(c >= QBLOCK, regular, NEG_INF)
    return jnp.stack([_stack_rows(regular), _stack_rows(first)], axis=0)


def _sample_bias_tiles(bias_g, g, t_new):
    w, d = GROUPS[g]
    n_heads = bias_g.shape[0]
    vals = bias_g[:, N_KEYS - 1:0:-1]
    strided = jnp.concatenate([vals[:, :, None], jnp.full((n_heads, N_KEYS - 1, d - 1), NEG_INF, F32)], axis=2)
    p = jnp.concatenate([strided.reshape(n_heads, w), jnp.full((n_heads, t_new), NEG_INF, F32)], axis=1)
    cached = _toeplitz(p, t_new, w)
    row = jnp.arange(w, dtype=jnp.int32)[None, None, :]
    min_valid = 2 * w - PAST_LEN
    cached = jnp.where(w + row >= min_valid, cached, NEG_INF)
    t = jnp.arange(t_new, dtype=jnp.int32)[None, :, None]
    i = jnp.arange(LANES, dtype=jnp.int32)[None, None, :] - (LANES - t_new)
    dist = t - i
    new = jnp.full((n_heads, t_new, LANES), NEG_INF, F32)
    for j in range((t_new - 1) // d + 1):
        new = jnp.where((i >= 0) & (dist == d * j), bias_g[:, j][:, None, None], new)
    return _stack_rows(cached), _stack_rows(new)


def _stack_heads(x, lane_head):
    return jnp.concatenate([jnp.where(lane_head == h, x, jnp.zeros_like(x)) for h in range(HEADS_PER_HALF)], axis=0)


def _unstack_heads(x, lane_head, q):
    out = jnp.where(lane_head == 0, x[0:q], 0.0)
    for h in range(1, HEADS_PER_HALF):
        out = out + jnp.where(lane_head == h, x[h * q:(h + 1) * q], 0.0)
    return out


def _attn_prompt_kernel(q_ref, kp_ref, kc_ref, vp_ref, vc_ref, bias_ref, o_ref, lse_ref):
    qb = q_ref.shape[1]
    q = q_ref[0]
    k = jnp.concatenate([kp_ref[0], kc_ref[0]], axis=0)
    v = jnp.concatenate([vp_ref[0], vc_ref[0]], axis=0)
    lane_head = lax.broadcasted_iota(jnp.int32, (qb, HALF), 1) // HEAD_DIM
    lane = lax.broadcasted_iota(jnp.int32, (qb, LANES), 1)
    lse_tile = jnp.zeros((qb, LANES), F32)
    for half in range(2):
        lanes = slice(half * HALF, (half + 1) * HALF)
        qs = _stack_heads(q[:, lanes], lane_head)
        s = lax.dot_general(qs, k[:, lanes], (((1,), (1,)), ((), ())), preferred_element_type=F32)
        s = s + bias_ref[0, half]
        m = jnp.max(s, axis=-1, keepdims=True)
        p = jnp.exp(s - m)
        den = jnp.sum(p, axis=-1, keepdims=True)
        ov = jnp.dot((p / den).astype(BF16), v[:, lanes], preferred_element_type=F32)
        o_ref[0, :, lanes] = _unstack_heads(ov, lane_head, qb).astype(o_ref.dtype)
        lse = m + jnp.log(den)
        for h in range(HEADS_PER_HALF):
            lse_tile = jnp.where(lane == half * HEADS_PER_HALF + h, lse[h * qb:(h + 1) * qb], lse_tile)
    lse_ref[0] = lse_tile


def _attn_prompt(q, kv, bias_tiles, g):
    b, s, _ = q.shape
    _, d = GROUPS[g]
    sd = s // d
    qb = QBLOCK
    blk = ATTN_WIDTH
    qv = q.reshape(b, sd, d * N_GROUPS * blk)
    kvv = kv.reshape(b, sd, d * N_GROUPS * 2 * blk)
    prev = lambda j: jnp.maximum(j - 1, 0)
    o, lse = pl.pallas_call(
        _attn_prompt_kernel,
        grid=(b, d, sd // qb),
        in_specs=[
            pl.BlockSpec((1, qb, blk), lambda i, r, j: (i, j, r * N_GROUPS + g)),
            pl.BlockSpec((1, qb, blk), lambda i, r, j: (i, prev(j), r * 2 * N_GROUPS + 2 * g)),
            pl.BlockSpec((1, qb, blk), lambda i, r, j: (i, j, r * 2 * N_GROUPS + 2 * g)),
            pl.BlockSpec((1, qb, blk), lambda i, r, j: (i, prev(j), r * 2 * N_GROUPS + 2 * g + 1)),
            pl.BlockSpec((1, qb, blk), lambda i, r, j: (i, j, r * 2 * N_GROUPS + 2 * g + 1)),
            pl.BlockSpec((1, 2, HEADS_PER_HALF * qb, 2 * qb), lambda i, r, j: (jnp.where(j == 0, 1, 0), 0, 0, 0)),
        ],
        out_specs=[pl.BlockSpec((1, qb, blk), lambda i, r, j: (i, j, r)),
                   pl.BlockSpec((1, qb, LANES), lambda i, r, j: (i, j, r))],
        out_shape=[jax.ShapeDtypeStruct((b, sd, d * blk), BF16), jax.ShapeDtypeStruct((b, sd, d * LANES), F32)],
        compiler_params=_params(3),
        name=f"attn_prompt_g{g}",
    )(qv, kvv, kvv, kvv, kvv, bias_tiles)
    return o.reshape(b * s, blk), lse.reshape(b * s, LANES)


def _expand_heads(w, e_ref):
    hi = w.astype(BF16)
    lo = (w - hi.astype(F32)).astype(BF16)
    e = e_ref[...]
    return jnp.dot(hi, e, preferred_element_type=F32) + jnp.dot(lo, e, preferred_element_type=F32)


def _merge_kernel(x_ref, o0_ref, o1_ref, o2_ref, l0_ref, l1_ref, l2_ref, e_ref, wo_ref, out_ref):
    ls = [l0_ref[...], l1_ref[...], l2_ref[...]]
    mx = jnp.maximum(jnp.maximum(ls[0], ls[1]), ls[2])
    es = [jnp.exp(l - mx) for l in ls]
    tot = es[0] + es[1] + es[2]
    merged = None
    for e, o_ref in zip(es, (o0_ref, o1_ref, o2_ref)):
        term = _expand_heads(e / tot, e_ref) * o_ref[...].astype(F32)
        merged = term if merged is None else merged + term
    y = jnp.dot(merged.astype(BF16), wo_ref[...], preferred_element_type=F32)
    out_ref[...] = x_ref[...] + y


def _head_expander():
    head = jnp.arange(LANES, dtype=jnp.int32)[:, None]
    lane = jnp.arange(ATTN_WIDTH, dtype=jnp.int32)[None, :]
    return (lane // HEAD_DIM == head).astype(BF16)


def _merge(x, outs, lses, w_o):
    n, d = x.shape
    tm = min(TOKEN_TILE, n)
    tok = lambda width: pl.BlockSpec((tm, width), lambda i: (i, 0))
    return pl.pallas_call(
        _merge_kernel,
        grid=(n // tm,),
        in_specs=[tok(d)] + [tok(ATTN_WIDTH)] * 3 + [tok(LANES)] * 3 + [_const_spec((LANES, ATTN_WIDTH)),
                                                                         _const_spec((ATTN_WIDTH, d))],
        out_specs=tok(d),
        out_shape=jax.ShapeDtypeStruct((n, d), F32),
        compiler_params=_params(1),
        name="merge",
    )(x, *outs, *lses, _head_expander(), w_o)


def _attn_sample_kernel(*refs):
    (q0_ref, q1_ref, q2_ref, kn0_ref, kn1_ref, kn2_ref, vn0_ref, vn1_ref, vn2_ref, c0_ref, c1_ref, c2_ref,
     bc0_ref, bc1_ref, bc2_ref, bn0_ref, bn1_ref, bn2_ref, o_ref, n0_ref, n1_ref, n2_ref) = refs
    half = pl.program_id(1)
    t = q0_ref.shape[1]
    lane_head = lax.broadcasted_iota(jnp.int32, (t, HALF), 1) // HEAD_DIM
    is_new = lax.broadcasted_iota(jnp.int32, (HALF, LANES), 1) >= LANES - t
    zpad = jnp.zeros((LANES - t, HALF), F32)
    outs, lses = [], []
    for q_ref, kn_ref, vn_ref, c_ref, bc_ref, bn_ref, n_ref in (
            (q0_ref, kn0_ref, vn0_ref, c0_ref, bc0_ref, bn0_ref, n0_ref),
            (q1_ref, kn1_ref, vn1_ref, c1_ref, bc1_ref, bn1_ref, n1_ref),
            (q2_ref, kn2_ref, vn2_ref, c2_ref, bc2_ref, bn2_ref, n2_ref)):
        w = c_ref.shape[3]
        qs = _stack_heads(q_ref[0], lane_head)
        k_t = c_ref[0, 0]
        v_t = c_ref[0, 1]
        k_new = jnp.concatenate([zpad, kn_ref[0]], axis=0)
        v_new = jnp.concatenate([zpad, vn_ref[0]], axis=0)
        k_new_t = k_new.T
        v_new_t = v_new.T
        s_c = jnp.dot(qs, k_t.astype(BF16), preferred_element_type=F32) + bc_ref[half]
        s_n = jnp.dot(qs, k_new_t.astype(BF16), preferred_element_type=F32) + bn_ref[half]
        m = jnp.maximum(s_c.max(axis=-1, keepdims=True), s_n.max(axis=-1, keepdims=True))
        p_c = jnp.exp(s_c - m)
        p_n = jnp.exp(s_n - m)
        den = p_c.sum(axis=-1, keepdims=True) + p_n.sum(axis=-1, keepdims=True)
        ov = lax.dot_general((p_c / den).astype(BF16), v_t.astype(BF16), (((1,), (1,)), ((), ())),
                             preferred_element_type=F32)
        ov = ov + jnp.dot((p_n / den).astype(BF16), v_new.astype(BF16), preferred_element_type=F32)
        outs.append(ov)
        lses.append(m + jnp.log(den))
        for kv, (old_t, new_t) in enumerate(((k_t, k_new_t), (v_t, v_new_t))):
            rolled = pltpu.roll(old_t, w - t, axis=1)
            if w > LANES:
                n_ref[0, kv, :, 0:w - LANES] = rolled[:, 0:w - LANES]
            n_ref[0, kv, :, w - LANES:w] = jnp.where(is_new, new_t, rolled[:, w - LANES:w])
    mx = jnp.maximum(jnp.maximum(lses[0], lses[1]), lses[2])
    es = [jnp.exp(l - mx) for l in lses]
    tot = es[0] + es[1] + es[2]
    merged = (es[0] / tot) * outs[0] + (es[1] / tot) * outs[1] + (es[2] / tot) * outs[2]
    o_ref[0] = _unstack_heads(merged, lane_head, t).astype(o_ref.dtype)


def _attn_sample(q, kv_new, caches_t, bias_gs):
    b, t, _ = q.shape
    n_half = ATTN_WIDTH // HALF
    tok = lambda col: pl.BlockSpec((1, t, HALF), lambda i, hf, col=col: (i, 0, col + hf))
    in_specs = [tok(g * n_half) for g in range(N_GROUPS)]
    in_specs += [tok(g * 2 * n_half) for g in range(N_GROUPS)]
    in_specs += [tok(g * 2 * n_half + n_half) for g in range(N_GROUPS)]
    args = [q] * N_GROUPS + [kv_new] * (2 * N_GROUPS)
    cache_specs = [pl.BlockSpec((1, 2, HALF, GROUPS[g][0]), lambda i, hf: (i, 0, hf, 0)) for g in range(N_GROUPS)]
    in_specs += cache_specs
    args += list(caches_t)
    tiles = [_sample_bias_tiles(bias_gs[g], g, t) for g in range(N_GROUPS)]
    for part in range(2):
        for g in range(N_GROUPS):
            in_specs.append(_const_spec(tiles[g][part].shape))
            args.append(tiles[g][part])
    res = pl.pallas_call(
        _attn_sample_kernel,
        grid=(b, n_half),
        in_specs=in_specs,
        out_specs=[pl.BlockSpec((1, t, HALF), lambda i, hf: (i, 0, hf))] + cache_specs,
        out_shape=[jax.ShapeDtypeStruct((b, t, ATTN_WIDTH), BF16)]
        + [jax.ShapeDtypeStruct(c.shape, F32) for c in caches_t],
        compiler_params=_params(2),
        name="attn_sample",
    )(*args)
    return res[0].reshape(b * t, ATTN_WIDTH), res[1:]


def _out_proj_kernel(x_ref, a_ref, wo_ref, out_ref):
    out_ref[...] = x_ref[...] + jnp.dot(a_ref[...], wo_ref[...], preferred_element_type=F32)


def _out_proj(x, a, w_o):
    n, d = x.shape
    return pl.pallas_call(
        _out_proj_kernel,
        out_shape=jax.ShapeDtypeStruct((n, d), F32),
        compiler_params=pltpu.CompilerParams(vmem_limit_bytes=VMEM_LIMIT),
        name="out_proj",
    )(x, a, w_o)


def _cache_view(c_t, b):
    w = c_t.shape[-1]
    return jnp.transpose(c_t.reshape(b, 2, HEADS_PER_GROUP, HEAD_DIM, w), (0, 4, 1, 2, 3))


def _trunk(x, conv_prev, kv_bufs, wts, conv_tile):
    b, t, d = x.shape
    n = b * t
    xf = x.reshape(n, d)
    norms = wts["norms"]
    xf = _ffn(xf, norms[0, 0], wts["ffn_w_in"][0, 0], wts["ffn_w_out"][0, 0])
    bb, ts = conv_tile
    x3, conv_state = _conv_module(
        xf.reshape(b, t, d), conv_prev, norms[0, 1], wts["conv_w_pw1"][0], wts["conv_b_pw1"][0], wts["conv_w_dw"][0],
        wts["conv_b_dw"][0], wts["conv_ln_g"][0], wts["conv_ln_b"][0], wts["conv_w_pw2"][0], wts["conv_b_pw2"][0],
        bb, ts)
    xf = x3.reshape(n, d)
    xf = _ffn(xf, norms[0, 2], wts["ffn_w_in"][0, 1], wts["ffn_w_out"][0, 1])
    bias_gs = [_group_bias(wts["rel_bias"], g) for g in range(N_GROUPS)]
    if kv_bufs is None:
        shared, *kv_t = _kv_prompt(xf.reshape(b, t, d), wts["norm_kv"], wts["w_kv"])
        new_kv = [_cache_view(c, b) for c in kv_t]
    else:
        shared = _proj(xf, wts["norm_kv"], wts["w_kv"], F32).reshape(b, t, -1)
    xf = _ffn(xf, norms[1, 0], wts["ffn_w_in"][1, 0], wts["ffn_w_out"][1, 0])
    q = _proj(xf, norms[1, 1], wts["attn_w_q"][0], BF16, scale=HEAD_DIM ** -0.5).reshape(b, t, -1)
    if kv_bufs is None:
        outs, lses = [], []
        for g in range(N_GROUPS):
            o, lse = _attn_prompt(q, shared, _prompt_bias_tiles(bias_gs[g]), g)
            outs.append(o)
            lses.append(lse)
        xf = _merge(xf, outs, lses, wts["attn_w_o"][0])
    else:
        caches_t = [jnp.transpose(buf, (0, 2, 3, 4, 1)).reshape(b, 2, ATTN_WIDTH, buf.shape[1]) for buf in kv_bufs]
        attn, new_t = _attn_sample(q, shared, caches_t, bias_gs)
        xf = _out_proj(xf, attn, wts["attn_w_o"][0])
        new_kv = [_cache_view(c, b) for c in new_t]
    xf = _ffn(xf, norms[1, 2], wts["ffn_w_in"][1, 1], wts["ffn_w_out"][1, 1], g_final=wts["norm_final"])
    return xf.reshape(b, t, d), conv_state[None], new_kv


def kernel(x_prompt, x_sample, state_conv, cache_kv_w128, cache_kv_w512, cache_kv_w2048, norms, ffn_w_in, ffn_w_out,
           conv_w_pw1, conv_b_pw1, conv_w_dw, conv_b_dw, conv_ln_g, conv_ln_b, conv_w_pw2, conv_b_pw2, norm_kv, w_kv,
           attn_w_q, attn_w_o, rel_bias, norm_final):
    wts = dict(
        norms=norms, ffn_w_in=ffn_w_in.astype(BF16), ffn_w_out=ffn_w_out.astype(BF16),
        conv_w_pw1=conv_w_pw1.astype(BF16), conv_b_pw1=conv_b_pw1, conv_w_dw=conv_w_dw, conv_b_dw=conv_b_dw,
        conv_ln_g=conv_ln_g, conv_ln_b=conv_ln_b, conv_w_pw2=conv_w_pw2.astype(BF16), conv_b_pw2=conv_b_pw2,
        norm_kv=norm_kv, w_kv=w_kv.astype(BF16), attn_w_q=attn_w_q.astype(BF16), attn_w_o=attn_w_o.astype(BF16),
        rel_bias=rel_bias, norm_final=norm_final)
    y_prompt, conv_prompt, kv_prompt = _trunk(x_prompt, None, None, wts, conv_tile=(1, TOKEN_TILE))
    y_sample, conv_sample, kv_sample = _trunk(
        x_sample, state_conv[0], (cache_kv_w128, cache_kv_w512, cache_kv_w2048), wts,
        conv_tile=(SUBLANES, x_sample.shape[1]))
    return (y_prompt, y_sample, conv_prompt, conv_sample, kv_prompt[0], kv_sample[0],
            kv_prompt[1], kv_sample[1], kv_prompt[2], kv_sample[2])
```

```python
import functools
import math

import jax
import jax.numpy as jnp
from jax import lax
from jax.experimental import pallas as pl
from jax.experimental.pallas import tpu as pltpu

F32 = jnp.float32
BF16 = jnp.bfloat16

EPS = 1e-6
NEG_INF = -1e30
GROUPS = ((128, 1), (512, 4), (2048, 16))
N_GROUPS = len(GROUPS)
HEADS_PER_GROUP = 8
HEAD_DIM = 64
ATTN_WIDTH = HEADS_PER_GROUP * HEAD_DIM
ROW_WIDTH = 2 * ATTN_WIDTH
N_KEYS = 129
N_BUCKETS = 32
MAX_DISTANCE = 2048
PAST_LEN = 8192
CONV_WIDTH = 31
CONV_HIST = 32
LANES = 128
SUBLANES = 8
HALF = 256
HEADS_PER_HALF = HALF // HEAD_DIM
QBLOCK = 128
ATTN_QUERIES = (1024, 256, 128)
TOKEN_TILE = 512
CONV_ROWS = 64
VMEM_LIMIT = 56 * 1024 * 1024


def _params(n_axes):
    return pltpu.CompilerParams(dimension_semantics=("arbitrary",) * n_axes, vmem_limit_bytes=VMEM_LIMIT)


def _const_spec(shape):
    zeros = (0,) * len(shape)
    return pl.BlockSpec(shape, lambda *_: zeros, pipeline_mode=pl.Buffered(1))


def _rms(x, g):
    return x * lax.rsqrt(jnp.mean(x * x, axis=-1, keepdims=True) + EPS) * g


def _silu(x):
    return x * jax.nn.sigmoid(x)


def _ffn_kernel(*refs, d_ff, n_chunk, has_final):
    if has_final:
        x_ref, g_ref, win_ref, wout_ref, gf_ref, o_ref, act_ref = refs
    else:
        x_ref, g_ref, win_ref, wout_ref, o_ref, act_ref = refs
    x = x_ref[...]
    h = _rms(x, g_ref[...]).astype(BF16)
    for c in range(d_ff // n_chunk):
        lo = c * n_chunk
        gate = jnp.dot(h, win_ref[:, lo:lo + n_chunk], preferred_element_type=F32)
        up = jnp.dot(h, win_ref[:, d_ff + lo:d_ff + lo + n_chunk], preferred_element_type=F32)
        act_ref[:, lo:lo + n_chunk] = (_silu(gate) * up).astype(BF16)
    y = jnp.dot(act_ref[...], wout_ref[...], preferred_element_type=F32)
    o = x + 0.5 * y
    if has_final:
        o = _rms(o, gf_ref[...])
    o_ref[...] = o


def _ffn(x, g, w_in, w_out, g_final=None):
    n, d = x.shape
    d_ff = w_out.shape[0]
    tm = min(TOKEN_TILE, n)
    has_final = g_final is not None
    in_specs = [pl.BlockSpec((tm, d), lambda i: (i, 0)), _const_spec((1, d)),
                _const_spec((d, 2 * d_ff)), _const_spec((d_ff, d))]
    args = [x, g.reshape(1, d), w_in, w_out]
    if has_final:
        in_specs.append(_const_spec((1, d)))
        args.append(g_final.reshape(1, d))
    return pl.pallas_call(
        functools.partial(_ffn_kernel, d_ff=d_ff, n_chunk=HALF, has_final=has_final),
        grid=(n // tm,),
        in_specs=in_specs,
        out_specs=pl.BlockSpec((tm, d), lambda i: (i, 0)),
        out_shape=jax.ShapeDtypeStruct((n, d), F32),
        scratch_shapes=[pltpu.VMEM((tm, d_ff), BF16)],
        compiler_params=_params(1),
        name="ffn",
    )(*args)


def _proj_kernel(x_ref, g_ref, w_ref, o_ref, *, scale):
    h = _rms(x_ref[...], g_ref[...]).astype(BF16)
    y = jnp.dot(h, w_ref[...], preferred_element_type=F32)
    if scale != 1.0:
        y = y * scale
    o_ref[...] = y.astype(o_ref.dtype)


def _proj(x, g, w, out_dtype, scale=1.0):
    n, d = x.shape
    n_out = w.shape[1]
    tm = min(TOKEN_TILE, n)
    return pl.pallas_call(
        functools.partial(_proj_kernel, scale=scale),
        grid=(n // tm,),
        in_specs=[pl.BlockSpec((tm, d), lambda i: (i, 0)), _const_spec((1, d)), _const_spec((d, n_out))],
        out_specs=pl.BlockSpec((tm, n_out), lambda i: (i, 0)),
        out_shape=jax.ShapeDtypeStruct((n, n_out), out_dtype),
        compiler_params=_params(1),
        name="proj",
    )(x, g.reshape(1, d), w)


def _store_slabs(slab_ref, y):
    for sl in range(slab_ref.shape[0]):
        slab_ref[sl] = y[:, sl * LANES:(sl + 1) * LANES]


def _proj_slab_kernel(x_ref, g_ref, w_ref, o_ref, *, scale):
    h = _rms(x_ref[...], g_ref[...]).astype(BF16)
    _store_slabs(o_ref, jnp.dot(h, w_ref[...], preferred_element_type=F32) * scale)


def _proj_slabs(x, g, w, scale):
    n, d = x.shape
    n_out = w.shape[1]
    tm = min(TOKEN_TILE, n)
    return pl.pallas_call(
        functools.partial(_proj_slab_kernel, scale=scale),
        grid=(n // tm,),
        in_specs=[pl.BlockSpec((tm, d), lambda i: (i, 0)), _const_spec((1, d)), _const_spec((d, n_out))],
        out_specs=pl.BlockSpec((n_out // LANES, tm, LANES), lambda i: (0, i, 0)),
        out_shape=jax.ShapeDtypeStruct((n_out // LANES, n, LANES), F32),
        compiler_params=_params(1),
        name="proj_slabs",
    )(x, g.reshape(1, d), w)


def _kv_prompt_kernel(x_ref, g_ref, w_ref, sh_ref, t0_ref, t1_ref, t2_ref, *, n_t, tm):
    j = pl.program_id(1)
    h = _rms(x_ref[0], g_ref[...]).astype(BF16)
    y = jnp.dot(h, w_ref[...], preferred_element_type=F32)
    _store_slabs(sh_ref, y)
    for g, t_ref in enumerate((t0_ref, t1_ref, t2_ref)):
        keep = min(GROUPS[g][0], n_t * tm)
        rows = min(keep, tm)

        @pl.when(j >= n_t - max(keep // tm, 1))
        def _(g=g, t_ref=t_ref, rows=rows):
            t_ref[0] = y[tm - rows:, g * ROW_WIDTH:(g + 1) * ROW_WIDTH].T


def _kv_prompt(x, g, w):
    b, s, d = x.shape
    n_out = w.shape[1]
    tm = min(TOKEN_TILE, s)
    n_t = s // tm
    out_specs = [pl.BlockSpec((n_out // LANES, tm, LANES), lambda i, j: (0, i * n_t + j, 0))]
    out_shape = [jax.ShapeDtypeStruct((n_out // LANES, b * s, LANES), F32)]
    for grp in range(N_GROUPS):
        keep = min(GROUPS[grp][0], s)
        first = n_t - max(keep // tm, 1)
        out_specs.append(pl.BlockSpec((1, ROW_WIDTH, min(keep, tm)),
                                      lambda i, j, first=first: (i, 0, jnp.maximum(j - first, 0))))
        out_shape.append(jax.ShapeDtypeStruct((b, ROW_WIDTH, keep), F32))
    return pl.pallas_call(
        functools.partial(_kv_prompt_kernel, n_t=n_t, tm=tm),
        grid=(b, n_t),
        in_specs=[pl.BlockSpec((1, tm, d), lambda i, j: (i, j, 0)), _const_spec((1, d)), _const_spec((d, n_out))],
        out_specs=out_specs,
        out_shape=out_shape,
        compiler_params=_params(2),
        name="kv_prompt",
    )(x, g.reshape(1, d), w)


def _conv_kernel(*refs, bb, ts, n_t, rc, has_prev):
    if has_prev:
        (x_ref, prev_ref, g_ref, w1_ref, b1_ref, wdw_ref, bdw_ref, lng_ref, lnb_ref, w2_ref, b2_ref,
         o_ref, st_ref, uext_ref, ush_ref, conv_ref) = refs
    else:
        (x_ref, g_ref, w1_ref, b1_ref, wdw_ref, bdw_ref, lng_ref, lnb_ref, w2_ref, b2_ref,
         o_ref, st_ref, uext_ref, ush_ref, conv_ref) = refs
    it = pl.program_id(1)
    d = x_ref.shape[-1]
    c_in = w2_ref.shape[0]
    x = x_ref[...].reshape(bb * ts, d)
    h = _rms(x, g_ref[...]).astype(BF16)
    ag = jnp.dot(h, w1_ref[...], preferred_element_type=F32) + b1_ref[...]
    u = ag[:, :c_in] * jax.nn.sigmoid(ag[:, c_in:])

    @pl.when(it == 0)
    def _():
        if has_prev:
            uext_ref[:, 0:CONV_HIST, :] = prev_ref[...]
        else:
            uext_ref[:, 0:CONV_HIST, :] = jnp.zeros((bb, CONV_HIST, c_in), F32)

    uext_ref[:, CONV_HIST:CONV_HIST + ts, :] = u.reshape(bb, ts, c_in)

    lead = CONV_HIST - (CONV_WIDTH - 1)
    n_lb = c_in // LANES
    for s in range(SUBLANES):
        n_sh = ts + SUBLANES * ((CONV_WIDTH - 1 - s) // SUBLANES)
        for lb in range(n_lb):
            ush_ref[s, lb, :, 0:n_sh, :] = uext_ref[:, lead + s:lead + s + n_sh, lb * LANES:(lb + 1) * LANES]

    reps = rc // SUBLANES
    for lb in range(n_lb):
        lanes = slice(lb * LANES, (lb + 1) * LANES)
        taps = [wdw_ref[k, :, lanes] for k in range(CONV_WIDTH)]
        bias = bdw_ref[:, lanes]

        def chunk(ci, carry, lb=lb, lanes=lanes, taps=taps, bias=bias):
            r0 = pl.multiple_of(ci * rc, rc)
            acc = jnp.broadcast_to(bias.reshape(1, 1, LANES), (bb, rc, LANES))
            for k in range(CONV_WIDTH):
                win = ush_ref[k % SUBLANES, lb, :, pl.ds(r0 + SUBLANES * (k // SUBLANES), rc), :]
                wk = jnp.concatenate([taps[k]] * reps, axis=0) if reps > 1 else taps[k]
                acc = acc + win * wk[None]
            conv_ref[:, pl.ds(r0, rc), lanes] = acc
            return carry

        lax.fori_loop(0, ts // rc, chunk, 0)

    acc = conv_ref[...].reshape(bb * ts, c_in)
    mu = jnp.mean(acc, axis=-1, keepdims=True)
    cen = acc - mu
    var = jnp.mean(cen * cen, axis=-1, keepdims=True)
    c = _silu(cen * lax.rsqrt(var + EPS) * lng_ref[...] + lnb_ref[...]).astype(BF16)
    y = jnp.dot(c, w2_ref[...], preferred_element_type=F32) + b2_ref[...]
    o_ref[...] = (x + y).reshape(bb, ts, d)

    @pl.when(it == n_t - 1)
    def _():
        st_ref[...] = uext_ref[:, ts:ts + CONV_HIST, :]

    if n_t > 1:
        uext_ref[:, 0:CONV_HIST, :] = uext_ref[:, ts:ts + CONV_HIST, :]


def _conv_module(x, prev, g, w1, b1, wdw, bdw, lng, lnb, w2, b2, bb, ts):
    b, t, d = x.shape
    c_in = w2.shape[0]
    n_t = t // ts
    rc = min(ts, CONV_ROWS)
    has_prev = prev is not None
    row = lambda v: v.reshape(1, -1)
    in_specs = [pl.BlockSpec((bb, ts, d), lambda i, j: (i, j, 0))]
    args = [x]
    if has_prev:
        pad = CONV_HIST - prev.shape[1]
        in_specs.append(pl.BlockSpec((bb, CONV_HIST, c_in), lambda i, j: (i, 0, 0)))
        args.append(jnp.pad(prev, ((0, 0), (pad, 0), (0, 0))))
    in_specs += [_const_spec((1, d)), _const_spec((d, 2 * c_in)), _const_spec((1, 2 * c_in)),
                 _const_spec((CONV_WIDTH, SUBLANES, c_in)), _const_spec((1, c_in)), _const_spec((1, c_in)),
                 _const_spec((1, c_in)), _const_spec((c_in, d)), _const_spec((1, d))]
    wdw_rep = jnp.broadcast_to(wdw[:, None, :], (CONV_WIDTH, SUBLANES, c_in))
    args += [row(g), w1, row(b1), wdw_rep, row(bdw), row(lng), row(lnb), w2, row(b2)]
    out, st = pl.pallas_call(
        functools.partial(_conv_kernel, bb=bb, ts=ts, n_t=n_t, rc=rc, has_prev=has_prev),
        grid=(b // bb, n_t),
        in_specs=in_specs,
        out_specs=[pl.BlockSpec((bb, ts, d), lambda i, j: (i, j, 0)),
                   pl.BlockSpec((bb, CONV_HIST, c_in), lambda i, j: (i, 0, 0))],
        out_shape=[jax.ShapeDtypeStruct((b, t, d), F32), jax.ShapeDtypeStruct((b, CONV_HIST, c_in), F32)],
        scratch_shapes=[pltpu.VMEM((bb, ts + CONV_HIST, c_in), F32),
                        pltpu.VMEM((SUBLANES, c_in // LANES, bb, ts + 24, LANES), F32),
                        pltpu.VMEM((bb, ts, c_in), F32)],
        compiler_params=_params(2),
        name="conv_module",
    )(*args)
    return out, st[:, CONV_HIST - (CONV_WIDTH - 1):]


def _t5_bucket(dist):
    max_exact = N_BUCKETS // 2
    d_f = jnp.maximum(dist, 1).astype(F32)
    large = max_exact + (jnp.log(d_f / max_exact) / math.log(MAX_DISTANCE / max_exact)
                         * (N_BUCKETS - max_exact)).astype(jnp.int32)
    large = jnp.minimum(large, N_BUCKETS - 1)
    return jnp.where(dist < max_exact, dist, large)


def _group_bias(rel_bias, g):
    _, d = GROUPS[g]
    dist = d * jnp.arange(N_KEYS, dtype=jnp.int32)
    b = jnp.take(rel_bias, _t5_bucket(dist), axis=0)
    return b[:, g * HEADS_PER_GROUP:(g + 1) * HEADS_PER_GROUP].T.astype(F32)


def _toeplitz(p, n_rows, n_cols):
    length = p.shape[1]
    stride = length - 1
    flat = jnp.tile(p, (1, n_rows))[:, :n_rows * stride]
    return flat.reshape(p.shape[0], n_rows, stride)[:, :, :n_cols]


def _stack_rows(t):
    return t.reshape(2, HEADS_PER_HALF * t.shape[1], t.shape[2])


def _prompt_bias_tiles(bias_g):
    neg = jnp.full((bias_g.shape[0], 2 * QBLOCK + 1 - N_KEYS), NEG_INF, F32)
    p = jnp.concatenate([bias_g[:, ::-1], neg], axis=1)
    regular = _toeplitz(p, QBLOCK, 2 * QBLOCK)
    c = jnp.arange(2 * QBLOCK, dtype=jnp.int32)[None, None, :]
    first = jnp.where(c >= QBLOCK, regular, NEG_INF)
    return jnp.stack([_stack_rows(regular), _stack_rows(first)], axis=0)


def _sample_bias_tiles(bias_g, g, t_new):
    w, d = GROUPS[g]
    n_heads = bias_g.shape[0]
    vals = bias_g[:, N_KEYS - 1:0:-1]
    strided = jnp.concatenate([vals[:, :, None], jnp.full((n_heads, N_KEYS - 1, d - 1), NEG_INF, F32)], axis=2)
    p = jnp.concatenate([strided.reshape(n_heads, w), jnp.full((n_heads, t_new), NEG_INF, F32)], axis=1)
    cached = _toeplitz(p, t_new, w)
    row = jnp.arange(w, dtype=jnp.int32)[None, None, :]
    min_valid = 2 * w - PAST_LEN
    cached = jnp.where(w + row >= min_valid, cached, NEG_INF)
    t = jnp.arange(t_new, dtype=jnp.int32)[None, :, None]
    i = jnp.arange(LANES, dtype=jnp.int32)[None, None, :] - (LANES - t_new)
    dist = t - i
    new = jnp.full((n_heads, t_new, LANES), NEG_INF, F32)
    for j in range((t_new - 1) // d + 1):
        new = jnp.where((i >= 0) & (dist == d * j), bias_g[:, j][:, None, None], new)
    return _stack_rows(cached), _stack_rows(new)


def _stack_heads(x, lane_head):
    return jnp.concatenate([jnp.where(lane_head == h, x, jnp.zeros_like(x)) for h in range(HEADS_PER_HALF)], axis=0)


def _unstack_heads(x, lane_head, q):
    out = jnp.where(lane_head == 0, x[0:q], 0.0)
    for h in range(1, HEADS_PER_HALF):
        out = out + jnp.where(lane_head == h, x[h * q:(h + 1) * q], 0.0)
    return out


def _residue_rows(start, d):
    return pl.ds(start, QBLOCK, stride=d) if d > 1 else pl.ds(start, QBLOCK)


def _gather_rows(ref, start, d):
    return jnp.concatenate([ref[sl, _residue_rows(start, d), :] for sl in range(ref.shape[0])], axis=1)


def _attn_prompt_kernel(q_ref, kp_ref, kc_ref, vp_ref, vc_ref, bias_ref, o_ref, lse_ref, *, d, n_sub):
    qb = QBLOCK
    j = pl.program_id(1)
    half = pl.program_id(2)
    lane_head = lax.broadcasted_iota(jnp.int32, (qb, HALF), 1) // HEAD_DIM
    lane = lax.broadcasted_iota(jnp.int32, (qb, LANES), 1)
    first = jnp.where(j == 0, 1, 0)
    for r in range(d):
        k_prev = _gather_rows(kp_ref, r, d).astype(BF16)
        v_prev = _gather_rows(vp_ref, r, d).astype(BF16)
        for sub in range(n_sub):
            start = r + d * sub * qb
            q = _gather_rows(q_ref, start, d).astype(BF16)
            k_cur = _gather_rows(kc_ref, start, d).astype(BF16)
            v_cur = _gather_rows(vc_ref, start, d).astype(BF16)
            k = jnp.concatenate([k_prev, k_cur], axis=0)
            v = jnp.concatenate([v_prev, v_cur], axis=0)
            qs = _stack_heads(q, lane_head)
            s = lax.dot_general(qs, k, (((1,), (1,)), ((), ())), preferred_element_type=F32)
            s = s + (bias_ref[first, half] if sub == 0 else bias_ref[0, half])
            m = jnp.max(s, axis=-1, keepdims=True)
            p = jnp.exp(s - m)
            den = jnp.sum(p, axis=-1, keepdims=True)
            ov = jnp.dot((p / den).astype(BF16), v, preferred_element_type=F32)
            o = _unstack_heads(ov, lane_head, qb)
            rows = _residue_rows(start, d)
            for sl in range(o_ref.shape[0]):
                o_ref[sl, rows, :] = o[:, sl * LANES:(sl + 1) * LANES]
            lse = m + jnp.log(den)
            lse_tile = jnp.zeros((qb, LANES), F32)
            for h in range(HEADS_PER_HALF):
                lse_tile = jnp.where(lane == half * HEADS_PER_HALF + h, lse[h * qb:(h + 1) * qb], lse_tile)
            lse_ref[0, rows, :] = lse_tile
            k_prev, v_prev = k_cur, v_cur


def _attn_prompt(q_sl, kv_sl, bias_tiles, g, b, s):
    w, d = GROUPS[g]
    tb = d * ATTN_QUERIES[g]
    pb = d * QBLOCK
    n_tb = s // tb
    sph = HALF // LANES
    n_half = ATTN_WIDTH // HALF
    cur = lambda i, j: i * n_tb + j
    prv = lambda i, j: jnp.maximum(i * (s // pb) + j * (tb // pb) - 1, 0)
    return pl.pallas_call(
        functools.partial(_attn_prompt_kernel, d=d, n_sub=ATTN_QUERIES[g] // QBLOCK),
        grid=(b, n_tb, n_half),
        in_specs=[
            pl.BlockSpec((sph, tb, LANES), lambda i, j, hf: (g * n_half + hf, cur(i, j), 0)),
            pl.BlockSpec((sph, pb, LANES), lambda i, j, hf: (g * 2 * n_half + hf, prv(i, j), 0)),
            pl.BlockSpec((sph, tb, LANES), lambda i, j, hf: (g * 2 * n_half + hf, cur(i, j), 0)),
            pl.BlockSpec((sph, pb, LANES), lambda i, j, hf: (g * 2 * n_half + n_half + hf, prv(i, j), 0)),
            pl.BlockSpec((sph, tb, LANES), lambda i, j, hf: (g * 2 * n_half + n_half + hf, cur(i, j), 0)),
            _const_spec(bias_tiles.shape),
        ],
        out_specs=[pl.BlockSpec((sph, tb, LANES), lambda i, j, hf: (hf, cur(i, j), 0)),
                   pl.BlockSpec((1, tb, LANES), lambda i, j, hf: (hf, cur(i, j), 0))],
        out_shape=[jax.ShapeDtypeStruct((n_half * sph, b * s, LANES), F32),
                   jax.ShapeDtypeStruct((n_half, b * s, LANES), F32)],
        compiler_params=_params(3),
        name=f"attn_prompt_g{g}",
    )(q_sl, kv_sl, kv_sl, kv_sl, kv_sl, bias_tiles)


def _expand_heads(w, e_ref):
    hi = w.astype(BF16)
    lo = (w - hi.astype(F32)).astype(BF16)
    e = e_ref[...]
    return jnp.dot(hi, e, preferred_element_type=F32) + jnp.dot(lo, e, preferred_element_type=F32)


def _merge_kernel(x_ref, o0_ref, o1_ref, o2_ref, l0_ref, l1_ref, l2_ref, e_ref, wo_ref, out_ref):
    ls = [l_ref[0] + l_ref[1] for l_ref in (l0_ref, l1_ref, l2_ref)]
    mx = jnp.maximum(jnp.maximum(ls[0], ls[1]), ls[2])
    es = [jnp.exp(l - mx) for l in ls]
    tot = es[0] + es[1] + es[2]
    merged = None
    for e, o_ref in zip(es, (o0_ref, o1_ref, o2_ref)):
        o = jnp.concatenate([o_ref[sl] for sl in range(o_ref.shape[0])], axis=1)
        term = _expand_heads(e / tot, e_ref) * o
        merged = term if merged is None else merged + term
    y = jnp.dot(merged.astype(BF16), wo_ref[...], preferred_element_type=F32)
    out_ref[...] = x_ref[...] + y


def _head_expander():
    head = jnp.arange(LANES, dtype=jnp.int32)[:, None]
    lane = jnp.arange(ATTN_WIDTH, dtype=jnp.int32)[None, :]
    return (lane // HEAD_DIM == head).astype(BF16)


def _merge(x, outs, lses, w_o):
    n, d = x.shape
    tm = min(TOKEN_TILE, n)
    tok = lambda width: pl.BlockSpec((tm, width), lambda i: (i, 0))
    slabs = lambda count: pl.BlockSpec((count, tm, LANES), lambda i: (0, i, 0))
    return pl.pallas_call(
        _merge_kernel,
        grid=(n // tm,),
        in_specs=[tok(d)] + [slabs(ATTN_WIDTH // LANES)] * 3 + [slabs(ATTN_WIDTH // HALF)] * 3
        + [_const_spec((LANES, ATTN_WIDTH)), _const_spec((ATTN_WIDTH, d))],
        out_specs=tok(d),
        out_shape=jax.ShapeDtypeStruct((n, d), F32),
        compiler_params=_params(1),
        name="merge",
    )(x, *outs, *lses, _head_expander(), w_o)


def _attn_sample_kernel(*refs):
    (q0_ref, q1_ref, q2_ref, kn0_ref, kn1_ref, kn2_ref, vn0_ref, vn1_ref, vn2_ref, c0_ref, c1_ref, c2_ref,
     bc0_ref, bc1_ref, bc2_ref, bn0_ref, bn1_ref, bn2_ref, o_ref, n0_ref, n1_ref, n2_ref) = refs
    half = pl.program_id(1)
    t = q0_ref.shape[1]
    lane_head = lax.broadcasted_iota(jnp.int32, (t, HALF), 1) // HEAD_DIM
    is_new = lax.broadcasted_iota(jnp.int32, (HALF, LANES), 1) >= LANES - t
    zpad = jnp.zeros((LANES - t, HALF), F32)
    outs, lses = [], []
    for q_ref, kn_ref, vn_ref, c_ref, bc_ref, bn_ref, n_ref in (
            (q0_ref, kn0_ref, vn0_ref, c0_ref, bc0_ref, bn0_ref, n0_ref),
            (q1_ref, kn1_ref, vn1_ref, c1_ref, bc1_ref, bn1_ref, n1_ref),
            (q2_ref, kn2_ref, vn2_ref, c2_ref, bc2_ref, bn2_ref, n2_ref)):
        w = c_ref.shape[3]
        qs = _stack_heads(q_ref[0], lane_head)
        k_t = c_ref[0, 0]
        v_t = c_ref[0, 1]
        k_new = jnp.concatenate([zpad, kn_ref[0]], axis=0)
        v_new = jnp.concatenate([zpad, vn_ref[0]], axis=0)
        k_new_t = k_new.T
        v_new_t = v_new.T
        s_c = jnp.dot(qs, k_t.astype(BF16), preferred_element_type=F32) + bc_ref[half]
        s_n = jnp.dot(qs, k_new_t.astype(BF16), preferred_element_type=F32) + bn_ref[half]
        m = jnp.maximum(s_c.max(axis=-1, keepdims=True), s_n.max(axis=-1, keepdims=True))
        p_c = jnp.exp(s_c - m)
        p_n = jnp.exp(s_n - m)
        den = p_c.sum(axis=-1, keepdims=True) + p_n.sum(axis=-1, keepdims=True)
        ov = lax.dot_general((p_c / den).astype(BF16), v_t.astype(BF16), (((1,), (1,)), ((), ())),
                             preferred_element_type=F32)
        ov = ov + jnp.dot((p_n / den).astype(BF16), v_new.astype(BF16), preferred_element_type=F32)
        outs.append(ov)
        lses.append(m + jnp.log(den))
        for kv, (old_t, new_t) in enumerate(((k_t, k_new_t), (v_t, v_new_t))):
            rolled = pltpu.roll(old_t, w - t, axis=1)
            if w > LANES:
                n_ref[0, kv, :, 0:w - LANES] = rolled[:, 0:w - LANES]
            n_ref[0, kv, :, w - LANES:w] = jnp.where(is_new, new_t, rolled[:, w - LANES:w])
    mx = jnp.maximum(jnp.maximum(lses[0], lses[1]), lses[2])
    es = [jnp.exp(l - mx) for l in lses]
    tot = es[0] + es[1] + es[2]
    merged = (es[0] / tot) * outs[0] + (es[1] / tot) * outs[1] + (es[2] / tot) * outs[2]
    o_ref[0] = _unstack_heads(merged, lane_head, t).astype(o_ref.dtype)


def _attn_sample(q, kv_new, caches_t, bias_gs):
    b, t, _ = q.shape
    n_half = ATTN_WIDTH // HALF
    tok = lambda col: pl.BlockSpec((1, t, HALF), lambda i, hf, col=col: (i, 0, col + hf))
    in_specs = [tok(g * n_half) for g in range(N_GROUPS)]
    in_specs += [tok(g * 2 * n_half) for g in range(N_GROUPS)]
    in_specs += [tok(g * 2 * n_half + n_half) for g in range(N_GROUPS)]
    args = [q] * N_GROUPS + [kv_new] * (2 * N_GROUPS)
    cache_specs = [pl.BlockSpec((1, 2, HALF, GROUPS[g][0]), lambda i, hf: (i, 0, hf, 0)) for g in range(N_GROUPS)]
    in_specs += cache_specs
    args += list(caches_t)
    tiles = [_sample_bias_tiles(bias_gs[g], g, t) for g in range(N_GROUPS)]
    for part in range(2):
        for g in range(N_GROUPS):
            in_specs.append(_const_spec(tiles[g][part].shape))
            args.append(tiles[g][part])
    res = pl.pallas_call(
        _attn_sample_kernel,
        grid=(b, n_half),
        in_specs=in_specs,
        out_specs=[pl.BlockSpec((1, t, HALF), lambda i, hf: (i, 0, hf))] + cache_specs,
        out_shape=[jax.ShapeDtypeStruct((b, t, ATTN_WIDTH), BF16)]
        + [jax.ShapeDtypeStruct(c.shape, F32) for c in caches_t],
        compiler_params=_params(2),
        name="attn_sample",
    )(*args)
    return res[0].reshape(b * t, ATTN_WIDTH), res[1:]


def _out_proj_kernel(x_ref, a_ref, wo_ref, out_ref):
    out_ref[...] = x_ref[...] + jnp.dot(a_ref[...], wo_ref[...], preferred_element_type=F32)


def _out_proj(x, a, w_o):
    n, d = x.shape
    return pl.pallas_call(
        _out_proj_kernel,
        out_shape=jax.ShapeDtypeStruct((n, d), F32),
        compiler_params=pltpu.CompilerParams(vmem_limit_bytes=VMEM_LIMIT),
        name="out_proj",
    )(x, a, w_o)


def _cache_view(c_t, b):
    w = c_t.shape[-1]
    return jnp.transpose(c_t.reshape(b, 2, HEADS_PER_GROUP, HEAD_DIM, w), (0, 4, 1, 2, 3))


def _trunk(x, conv_prev, kv_bufs, wts, conv_tile):
    b, t, d = x.shape
    n = b * t
    xf = x.reshape(n, d)
    norms = wts["norms"]
    xf = _ffn(xf, norms[0, 0], wts["ffn_w_in"][0, 0], wts["ffn_w_out"][0, 0])
    bb, ts = conv_tile
    x3, conv_state = _conv_module(
        xf.reshape(b, t, d), conv_prev, norms[0, 1], wts["conv_w_pw1"][0], wts["conv_b_pw1"][0], wts["conv_w_dw"][0],
        wts["conv_b_dw"][0], wts["conv_ln_g"][0], wts["conv_ln_b"][0], wts["conv_w_pw2"][0], wts["conv_b_pw2"][0],
        bb, ts)
    xf = x3.reshape(n, d)
    xf = _ffn(xf, norms[0, 2], wts["ffn_w_in"][0, 1], wts["ffn_w_out"][0, 1])
    bias_gs = [_group_bias(wts["rel_bias"], g) for g in range(N_GROUPS)]
    if kv_bufs is None:
        shared, *kv_t = _kv_prompt(xf.reshape(b, t, d), wts["norm_kv"], wts["w_kv"])
        new_kv = [_cache_view(c, b) for c in kv_t]
    else:
        shared = _proj(xf, wts["norm_kv"], wts["w_kv"], F32).reshape(b, t, -1)
    xf = _ffn(xf, norms[1, 0], wts["ffn_w_in"][1, 0], wts["ffn_w_out"][1, 0])
    scale = HEAD_DIM ** -0.5
    if kv_bufs is None:
        q_sl = _proj_slabs(xf, norms[1, 1], wts["attn_w_q"][0], scale)
        outs, lses = [], []
        for g in range(N_GROUPS):
            o, lse = _attn_prompt(q_sl, shared, _prompt_bias_tiles(bias_gs[g]), g, b, t)
            outs.append(o)
            lses.append(lse)
        xf = _merge(xf, outs, lses, wts["attn_w_o"][0])
    else:
        q = _proj(xf, norms[1, 1], wts["attn_w_q"][0], BF16, scale=scale).reshape(b, t, -1)
        caches_t = [jnp.transpose(buf, (0, 2, 3, 4, 1)).reshape(b, 2, ATTN_WIDTH, buf.shape[1]) for buf in kv_bufs]
        attn, new_t = _attn_sample(q, shared, caches_t, bias_gs)
        xf = _out_proj(xf, attn, wts["attn_w_o"][0])
        new_kv = [_cache_view(c, b) for c in new_t]
    xf = _ffn(xf, norms[1, 2], wts["ffn_w_in"][1, 1], wts["ffn_w_out"][1, 1], g_final=wts["norm_final"])
    return xf.reshape(b, t, d), conv_state[None], new_kv


def kernel(x_prompt, x_sample, state_conv, cache_kv_w128, cache_kv_w512, cache_kv_w2048, norms, ffn_w_in, ffn_w_out,
           conv_w_pw1, conv_b_pw1, conv_w_dw, conv_b_dw, conv_ln_g, conv_ln_b, conv_w_pw2, conv_b_pw2, norm_kv, w_kv,
           attn_w_q, attn_w_o, rel_bias, norm_final):
    wts = dict(
        norms=norms, ffn_w_in=ffn_w_in.astype(BF16), ffn_w_out=ffn_w_out.astype(BF16),
        conv_w_pw1=conv_w_pw1.astype(BF16), conv_b_pw1=conv_b_pw1, conv_w_dw=conv_w_dw, conv_b_dw=conv_b_dw,
        conv_ln_g=conv_ln_g, conv_ln_b=conv_ln_b, conv_w_pw2=conv_w_pw2.astype(BF16), conv_b_pw2=conv_b_pw2,
        norm_kv=norm_kv, w_kv=w_kv.astype(BF16), attn_w_q=attn_w_q.astype(BF16), attn_w_o=attn_w_o.astype(BF16),
        rel_bias=rel_bias, norm_final=norm_final)
    y_prompt, conv_prompt, kv_prompt = _trunk(x_prompt, None, None, wts, conv_tile=(1, TOKEN_TILE))
    y_sample, conv_sample, kv_sample = _trunk(
        x_sample, state_conv[0], (cache_kv_w128, cache_kv_w512, cache_kv_w2048), wts,
        conv_tile=(SUBLANES, x_sample.shape[1]))
    return (y_prompt, y_sample, conv_prompt, conv_sample, kv_prompt[0], kv_sample[0],
            kv_prompt[1], kv_sample[1], kv_prompt[2], kv_sample[2])
```

```python
import functools
import math

import jax
import jax.numpy as jnp
from jax import lax
from jax.experimental import pallas as pl
from jax.experimental.pallas import tpu as pltpu

F32 = jnp.float32
BF16 = jnp.bfloat16

EPS = 1e-6
NEG_INF = -1e30
GROUPS = ((128, 1), (512, 4), (2048, 16))
N_GROUPS = len(GROUPS)
HEADS_PER_GROUP = 8
HEAD_DIM = 64
ATTN_WIDTH = HEADS_PER_GROUP * HEAD_DIM
ROW_WIDTH = 2 * ATTN_WIDTH
N_KEYS = 129
N_BUCKETS = 32
MAX_DISTANCE = 2048
PAST_LEN = 8192
CONV_WIDTH = 31
CONV_HIST = 32
LANES = 128
SUBLANES = 8
HALF = 256
HEADS_PER_HALF = HALF // HEAD_DIM
QBLOCK = 128
ATTN_QUERIES = (1024, 256, 128)
TOKEN_TILE = 512
CONV_ROWS = 64
VMEM_LIMIT = 56 * 1024 * 1024


def _params(n_axes):
    return pltpu.CompilerParams(dimension_semantics=("arbitrary",) * n_axes, vmem_limit_bytes=VMEM_LIMIT)


def _const_spec(shape):
    zeros = (0,) * len(shape)
    return pl.BlockSpec(shape, lambda *_: zeros, pipeline_mode=pl.Buffered(1))


def _rms(x, g):
    return x * lax.rsqrt(jnp.mean(x * x, axis=-1, keepdims=True) + EPS) * g


def _silu(x):
    return x * jax.nn.sigmoid(x)


def _ffn_kernel(*refs, d_ff, n_chunk, has_merge, has_final, q_scale):
    refs = list(refs)
    x_ref = refs.pop(0)
    if has_merge:
        attn_refs = [refs.pop(0) for _ in range(2 * N_GROUPS)]
        e_ref, wo_ref = refs.pop(0), refs.pop(0)
    g_ref, win_ref, wout_ref = refs.pop(0), refs.pop(0), refs.pop(0)
    if has_final:
        gf_ref = refs.pop(0)
    if q_scale is not None:
        gq_ref, wq_ref = refs.pop(0), refs.pop(0)
    o_ref = refs.pop(0)
    if q_scale is not None:
        q_ref = refs.pop(0)
    act_ref = refs.pop(0)

    x = x_ref[...]
    if has_merge:
        x = x + _merged_attention(attn_refs[:N_GROUPS], attn_refs[N_GROUPS:], e_ref, wo_ref)
    h = _rms(x, g_ref[...]).astype(BF16)
    for c in range(d_ff // n_chunk):
        lo = c * n_chunk
        gate = jnp.dot(h, win_ref[:, lo:lo + n_chunk], preferred_element_type=F32)
        up = jnp.dot(h, win_ref[:, d_ff + lo:d_ff + lo + n_chunk], preferred_element_type=F32)
        act_ref[:, lo:lo + n_chunk] = (_silu(gate) * up).astype(BF16)
    y = jnp.dot(act_ref[...], wout_ref[...], preferred_element_type=F32)
    o = x + 0.5 * y
    if q_scale is not None:
        hq = _rms(o, gq_ref[...]).astype(BF16)
        _store_slabs(q_ref, jnp.dot(hq, wq_ref[...], preferred_element_type=F32) * q_scale)
    if has_final:
        o = _rms(o, gf_ref[...])
    o_ref[...] = o


def _ffn(x, g, w_in, w_out, idx, *, merge=None, g_final=None, q_proj=None):
    n, d = x.shape
    d_ff = w_out.shape[-2]
    tm = min(TOKEN_TILE, n)
    tok = lambda width: pl.BlockSpec((tm, width), lambda i: (i, 0))
    slabs = lambda count: pl.BlockSpec((count, tm, LANES), lambda i: (0, i, 0))
    stacked = lambda rows, cols: pl.BlockSpec((None, None, rows, cols), lambda i: (*idx, 0, 0),
                                              pipeline_mode=pl.Buffered(1))
    in_specs = [tok(d)]
    args = [x]
    if merge is not None:
        outs, lses, w_o = merge
        in_specs += [slabs(ATTN_WIDTH // LANES)] * N_GROUPS + [slabs(ATTN_WIDTH // HALF)] * N_GROUPS
        in_specs += [_const_spec((LANES, ATTN_WIDTH)), _const_spec((ATTN_WIDTH, d))]
        args += [*outs, *lses, _head_expander(), w_o]
    in_specs += [_const_spec((1, d)), stacked(d, 2 * d_ff), stacked(d_ff, d)]
    args += [g.reshape(1, d), w_in, w_out]
    if g_final is not None:
        in_specs.append(_const_spec((1, d)))
        args.append(g_final.reshape(1, d))
    out_specs = [tok(d)]
    out_shape = [jax.ShapeDtypeStruct((n, d), F32)]
    q_scale = None
    if q_proj is not None:
        g_q, w_q, q_scale = q_proj
        n_q = w_q.shape[1]
        in_specs += [_const_spec((1, d)), _const_spec((d, n_q))]
        args += [g_q.reshape(1, d), w_q]
        out_specs.append(slabs(n_q // LANES))
        out_shape.append(jax.ShapeDtypeStruct((n_q // LANES, n, LANES), F32))
    res = pl.pallas_call(
        functools.partial(_ffn_kernel, d_ff=d_ff, n_chunk=HALF, has_merge=merge is not None,
                          has_final=g_final is not None, q_scale=q_scale),
        grid=(n // tm,),
        in_specs=in_specs,
        out_specs=out_specs,
        out_shape=out_shape,
        scratch_shapes=[pltpu.VMEM((tm, d_ff), BF16)],
        compiler_params=_params(1),
        name="ffn",
    )(*args)
    return res[0] if q_proj is None else res


def _proj_kernel(x_ref, g_ref, w_ref, o_ref, *, scale):
    h = _rms(x_ref[...], g_ref[...]).astype(BF16)
    y = jnp.dot(h, w_ref[...], preferred_element_type=F32)
    if scale != 1.0:
        y = y * scale
    o_ref[...] = y.astype(o_ref.dtype)


def _proj(x, g, w, out_dtype, scale=1.0):
    n, d = x.shape
    n_out = w.shape[1]
    tm = min(TOKEN_TILE, n)
    return pl.pallas_call(
        functools.partial(_proj_kernel, scale=scale),
        grid=(n // tm,),
        in_specs=[pl.BlockSpec((tm, d), lambda i: (i, 0)), _const_spec((1, d)), _const_spec((d, n_out))],
        out_specs=pl.BlockSpec((tm, n_out), lambda i: (i, 0)),
        out_shape=jax.ShapeDtypeStruct((n, n_out), out_dtype),
        compiler_params=_params(1),
        name="proj",
    )(x, g.reshape(1, d), w)


def _store_slabs(slab_ref, y):
    for sl in range(slab_ref.shape[0]):
        slab_ref[sl] = y[:, sl * LANES:(sl + 1) * LANES]


def _kv_prompt_kernel(x_ref, g_ref, w_ref, sh_ref, t0_ref, t1_ref, t2_ref, *, n_t, tm):
    j = pl.program_id(1)
    h = _rms(x_ref[0], g_ref[...]).astype(BF16)
    y = jnp.dot(h, w_ref[...], preferred_element_type=F32)
    _store_slabs(sh_ref, y)
    for g, t_ref in enumerate((t0_ref, t1_ref, t2_ref)):
        keep = min(GROUPS[g][0], n_t * tm)
        rows = min(keep, tm)

        @pl.when(j >= n_t - max(keep // tm, 1))
        def _(g=g, t_ref=t_ref, rows=rows):
            t_ref[0] = y[tm - rows:, g * ROW_WIDTH:(g + 1) * ROW_WIDTH].T


def _kv_prompt(x, g, w):
    b, s, d = x.shape
    n_out = w.shape[1]
    tm = min(TOKEN_TILE, s)
    n_t = s // tm
    out_specs = [pl.BlockSpec((n_out // LANES, tm, LANES), lambda i, j: (0, i * n_t + j, 0))]
    out_shape = [jax.ShapeDtypeStruct((n_out // LANES, b * s, LANES), F32)]
    for grp in range(N_GROUPS):
        keep = min(GROUPS[grp][0], s)
        first = n_t - max(keep // tm, 1)
        out_specs.append(pl.BlockSpec((1, ROW_WIDTH, min(keep, tm)),
                                      lambda i, j, first=first: (i, 0, jnp.maximum(j - first, 0))))
        out_shape.append(jax.ShapeDtypeStruct((b, ROW_WIDTH, keep), F32))
    return pl.pallas_call(
        functools.partial(_kv_prompt_kernel, n_t=n_t, tm=tm),
        grid=(b, n_t),
        in_specs=[pl.BlockSpec((1, tm, d), lambda i, j: (i, j, 0)), _const_spec((1, d)), _const_spec((d, n_out))],
        out_specs=out_specs,
        out_shape=out_shape,
        compiler_params=_params(2),
        name="kv_prompt",
    )(x, g.reshape(1, d), w)


def _conv_kernel(*refs, bb, ts, n_t, rc, has_prev):
    if has_prev:
        (x_ref, prev_ref, g_ref, w1_ref, b1_ref, wdw_ref, bdw_ref, lng_ref, lnb_ref, w2_ref, b2_ref,
         o_ref, st_ref, uext_ref, ush_ref, conv_ref) = refs
    else:
        (x_ref, g_ref, w1_ref, b1_ref, wdw_ref, bdw_ref, lng_ref, lnb_ref, w2_ref, b2_ref,
         o_ref, st_ref, uext_ref, ush_ref, conv_ref) = refs
    it = pl.program_id(1)
    d = x_ref.shape[-1]
    c_in = w2_ref.shape[0]
    x = x_ref[...].reshape(bb * ts, d)
    h = _rms(x, g_ref[...]).astype(BF16)
    ag = jnp.dot(h, w1_ref[...], preferred_element_type=F32) + b1_ref[...]
    u = ag[:, :c_in] * jax.nn.sigmoid(ag[:, c_in:])

    @pl.when(it == 0)
    def _():
        if has_prev:
            uext_ref[:, 0:CONV_HIST, :] = prev_ref[...]
        else:
            uext_ref[:, 0:CONV_HIST, :] = jnp.zeros((bb, CONV_HIST, c_in), F32)

    uext_ref[:, CONV_HIST:CONV_HIST + ts, :] = u.reshape(bb, ts, c_in)

    lead = CONV_HIST - (CONV_WIDTH - 1)
    n_lb = c_in // LANES
    for s in range(SUBLANES):
        n_sh = ts + SUBLANES * ((CONV_WIDTH - 1 - s) // SUBLANES)
        for lb in range(n_lb):
            ush_ref[s, lb, :, 0:n_sh, :] = uext_ref[:, lead + s:lead + s + n_sh, lb * LANES:(lb + 1) * LANES]

    reps = rc // SUBLANES
    for lb in range(n_lb):
        lanes = slice(lb * LANES, (lb + 1) * LANES)
        taps = [wdw_ref[k, :, lanes] for k in range(CONV_WIDTH)]
        bias = bdw_ref[:, lanes]

        def chunk(ci, carry, lb=lb, lanes=lanes, taps=taps, bias=bias):
            r0 = pl.multiple_of(ci * rc, rc)
            acc = jnp.broadcast_to(bias.reshape(1, 1, LANES), (bb, rc, LANES))
            for k in range(CONV_WIDTH):
                win = ush_ref[k % SUBLANES, lb, :, pl.ds(r0 + SUBLANES * (k // SUBLANES), rc), :]
                wk = jnp.concatenate([taps[k]] * reps, axis=0) if reps > 1 else taps[k]
                acc = acc + win * wk[None]
            conv_ref[:, pl.ds(r0, rc), lanes] = acc
            return carry

        lax.fori_loop(0, ts // rc, chunk, 0)

    acc = conv_ref[...].reshape(bb * ts, c_in)
    mu = jnp.mean(acc, axis=-1, keepdims=True)
    cen = acc - mu
    var = jnp.mean(cen * cen, axis=-1, keepdims=True)
    c = _silu(cen * lax.rsqrt(var + EPS) * lng_ref[...] + lnb_ref[...]).astype(BF16)
    y = jnp.dot(c, w2_ref[...], preferred_element_type=F32) + b2_ref[...]
    o_ref[...] = (x + y).reshape(bb, ts, d)

    @pl.when(it == n_t - 1)
    def _():
        st_ref[...] = uext_ref[:, ts:ts + CONV_HIST, :]

    if n_t > 1:
        uext_ref[:, 0:CONV_HIST, :] = uext_ref[:, ts:ts + CONV_HIST, :]


def _conv_module(x, prev, g, w1, b1, wdw, bdw, lng, lnb, w2, b2, bb, ts):
    b, t, d = x.shape
    c_in = w2.shape[0]
    n_t = t // ts
    rc = min(ts, CONV_ROWS)
    has_prev = prev is not None
    row = lambda v: v.reshape(1, -1)
    in_specs = [pl.BlockSpec((bb, ts, d), lambda i, j: (i, j, 0))]
    args = [x]
    if has_prev:
        pad = CONV_HIST - prev.shape[1]
        in_specs.append(pl.BlockSpec((bb, CONV_HIST, c_in), lambda i, j: (i, 0, 0)))
        args.append(jnp.pad(prev, ((0, 0), (pad, 0), (0, 0))))
    in_specs += [_const_spec((1, d)), _const_spec((d, 2 * c_in)), _const_spec((1, 2 * c_in)),
                 _const_spec((CONV_WIDTH, SUBLANES, c_in)), _const_spec((1, c_in)), _const_spec((1, c_in)),
                 _const_spec((1, c_in)), _const_spec((c_in, d)), _const_spec((1, d))]
    wdw_rep = jnp.broadcast_to(wdw[:, None, :], (CONV_WIDTH, SUBLANES, c_in))
    args += [row(g), w1, row(b1), wdw_rep, row(bdw), row(lng), row(lnb), w2, row(b2)]
    out, st = pl.pallas_call(
        functools.partial(_conv_kernel, bb=bb, ts=ts, n_t=n_t, rc=rc, has_prev=has_prev),
        grid=(b // bb, n_t),
        in_specs=in_specs,
        out_specs=[pl.BlockSpec((bb, ts, d), lambda i, j: (i, j, 0)),
                   pl.BlockSpec((bb, CONV_HIST, c_in), lambda i, j: (i, 0, 0))],
        out_shape=[jax.ShapeDtypeStruct((b, t, d), F32), jax.ShapeDtypeStruct((b, CONV_HIST, c_in), F32)],
        scratch_shapes=[pltpu.VMEM((bb, ts + CONV_HIST, c_in), F32),
                        pltpu.VMEM((SUBLANES, c_in // LANES, bb, ts + 24, LANES), F32),
                        pltpu.VMEM((bb, ts, c_in), F32)],
        compiler_params=_params(2),
        name="conv_module",
    )(*args)
    return out, st[:, CONV_HIST - (CONV_WIDTH - 1):]


def _conv_pipe_kernel(xc_ref, xp_ref, g_ref, w1_ref, b1_ref, wdw_ref, bdw_ref, lng_ref, lnb_ref, w2_ref, b2_ref,
                      o_ref, st_ref, h_ref, u0_ref, u1_ref, ush_ref, conv_ref, *, ts, n_t, rc):
    i = pl.program_id(0)
    n_lb = conv_ref.shape[0]
    lead = CONV_HIST - (CONV_WIDTH - 1)
    reps = rc // SUBLANES

    @pl.when(i == 0)
    def _():
        u1_ref[...] = jnp.zeros(u1_ref.shape, F32)

    def step(ua_ref, ub_ref):
        starts_sequence = (i % n_t) == 0
        h_ref[...] = _rms(xc_ref[...], g_ref[...]).astype(BF16)
        for lb in range(n_lb):
            lanes = slice(lb * LANES, (lb + 1) * LANES)
            ag = jnp.dot(h_ref[...], w1_ref[lb], preferred_element_type=F32) + b1_ref[lb]
            u = ag[:, :LANES] * jax.nn.sigmoid(ag[:, LANES:])
            ua_ref[lb, 0:CONV_HIST, :] = jnp.where(starts_sequence, 0.0, ub_ref[lb, ts:ts + CONV_HIST, :])
            ua_ref[lb, CONV_HIST:CONV_HIST + ts, :] = u
            buf = lb % 2
            for s in range(SUBLANES):
                n_sh = ts + SUBLANES * ((CONV_WIDTH - 1 - s) // SUBLANES)
                ush_ref[buf, s, 0:n_sh, :] = ub_ref[lb, lead + s:lead + s + n_sh, :]
            taps = [wdw_ref[k, :, lanes] for k in range(CONV_WIDTH)]
            bias = jnp.broadcast_to(bdw_ref[:, lanes], (rc, LANES))
            for ci in range(ts // rc):
                r0 = ci * rc
                acc = bias
                for k in range(CONV_WIDTH):
                    lo = r0 + SUBLANES * (k // SUBLANES)
                    wk = jnp.concatenate([taps[k]] * reps, axis=0) if reps > 1 else taps[k]
                    acc = acc + ush_ref[buf, k % SUBLANES, lo:lo + rc, :] * wk
                conv_ref[lb, r0:r0 + rc, :] = acc

        acc = jnp.concatenate([conv_ref[lb] for lb in range(n_lb)], axis=1)
        mu = jnp.mean(acc, axis=-1, keepdims=True)
        cen = acc - mu
        var = jnp.mean(cen * cen, axis=-1, keepdims=True)
        c = _silu(cen * lax.rsqrt(var + EPS) * lng_ref[...] + lnb_ref[...]).astype(BF16)
        y = jnp.dot(c, w2_ref[...], preferred_element_type=F32) + b2_ref[...]

        @pl.when(i >= 1)
        def _():
            o_ref[...] = xp_ref[...] + y

        @pl.when((i % n_t) == n_t - 1)
        def _():
            for lb in range(n_lb):
                st_ref[0, :, lb * LANES:(lb + 1) * LANES] = ua_ref[lb, ts:ts + CONV_HIST, :]

    @pl.when(i % 2 == 0)
    def _():
        step(u0_ref, u1_ref)

    @pl.when(i % 2 == 1)
    def _():
        step(u1_ref, u0_ref)


def _conv_module_pipelined(x, g, w1, b1, wdw, bdw, lng, lnb, w2, b2):
    b, t, d = x.shape
    c_in = w2.shape[0]
    ts = TOKEN_TILE
    n_t = t // ts
    n_tiles = b * n_t
    n_lb = c_in // LANES
    row = lambda v: v.reshape(1, -1)
    cols = lambda v, lb: v[..., lb * LANES:(lb + 1) * LANES]
    w1_blocks = jnp.stack([jnp.concatenate([cols(w1, lb), cols(w1, n_lb + lb)], axis=-1) for lb in range(n_lb)])
    b1_blocks = jnp.stack([jnp.concatenate([cols(row(b1), lb), cols(row(b1), n_lb + lb)], axis=-1)
                           for lb in range(n_lb)])
    wdw_rep = jnp.broadcast_to(wdw[:, None, :], (CONV_WIDTH, SUBLANES, c_in))
    last = n_tiles - 1
    out, st = pl.pallas_call(
        functools.partial(_conv_pipe_kernel, ts=ts, n_t=n_t, rc=CONV_ROWS // 2),
        grid=(n_tiles + 1,),
        in_specs=[pl.BlockSpec((ts, d), lambda i: (jnp.minimum(i, last), 0)),
                  pl.BlockSpec((ts, d), lambda i: (jnp.maximum(i - 1, 0), 0)),
                  _const_spec((1, d)), _const_spec((n_lb, d, 2 * LANES)), _const_spec((n_lb, 1, 2 * LANES)),
                  _const_spec((CONV_WIDTH, SUBLANES, c_in)), _const_spec((1, c_in)), _const_spec((1, c_in)),
                  _const_spec((1, c_in)), _const_spec((c_in, d)), _const_spec((1, d))],
        out_specs=[pl.BlockSpec((ts, d), lambda i: (jnp.maximum(i - 1, 0), 0)),
                   pl.BlockSpec((1, CONV_HIST, c_in), lambda i: (jnp.minimum(i, last) // n_t, 0, 0))],
        out_shape=[jax.ShapeDtypeStruct((b * t, d), F32), jax.ShapeDtypeStruct((b, CONV_HIST, c_in), F32)],
        scratch_shapes=[pltpu.VMEM((ts, d), BF16),
                        pltpu.VMEM((n_lb, ts + CONV_HIST, LANES), F32),
                        pltpu.VMEM((n_lb, ts + CONV_HIST, LANES), F32),
                        pltpu.VMEM((2, SUBLANES, ts + 24, LANES), F32),
                        pltpu.VMEM((n_lb, ts, LANES), F32)],
        compiler_params=_params(1),
        name="conv_pipelined",
    )(x.reshape(b * t, d), x.reshape(b * t, d), row(g), w1_blocks, b1_blocks, wdw_rep, row(bdw), row(lng), row(lnb),
      w2, row(b2))
    return out.reshape(b, t, d), st[:, CONV_HIST - (CONV_WIDTH - 1):]


def _t5_bucket(dist):
    max_exact = N_BUCKETS // 2
    d_f = jnp.maximum(dist, 1).astype(F32)
    large = max_exact + (jnp.log(d_f / max_exact) / math.log(MAX_DISTANCE / max_exact)
                         * (N_BUCKETS - max_exact)).astype(jnp.int32)
    large = jnp.minimum(large, N_BUCKETS - 1)
    return jnp.where(dist < max_exact, dist, large)


def _group_bias(rel_bias, g):
    _, d = GROUPS[g]
    dist = d * jnp.arange(N_KEYS, dtype=jnp.int32)
    b = jnp.take(rel_bias, _t5_bucket(dist), axis=0)
    return b[:, g * HEADS_PER_GROUP:(g + 1) * HEADS_PER_GROUP].T.astype(F32)


def _toeplitz(p, n_rows, n_cols):
    length = p.shape[1]
    stride = length - 1
    flat = jnp.tile(p, (1, n_rows))[:, :n_rows * stride]
    return flat.reshape(p.shape[0], n_rows, stride)[:, :, :n_cols]


def _stack_rows(t):
    return t.reshape(2, HEADS_PER_HALF * t.shape[1], t.shape[2])


def _prompt_bias_tiles(bias_g):
    neg = jnp.full((bias_g.shape[0], 2 * QBLOCK + 1 - N_KEYS), NEG_INF, F32)
    p = jnp.concatenate([bias_g[:, ::-1], neg], axis=1)
    regular = _toeplitz(p, QBLOCK, 2 * QBLOCK)
    c = jnp.arange(2 * QBLOCK, dtype=jnp.int32)[None, None, :]
    first = jnp.where(c >= QBLOCK, regular, NEG_INF)
    return jnp.stack([_stack_rows(regular), _stack_rows(first)], axis=0)


def _sample_bias_tiles(bias_g, g, t_new):
    w, d = GROUPS[g]
    n_heads = bias_g.shape[0]
    vals = bias_g[:, N_KEYS - 1:0:-1]
    strided = jnp.concatenate([vals[:, :, None], jnp.full((n_heads, N_KEYS - 1, d - 1), NEG_INF, F32)], axis=2)
    p = jnp.concatenate([strided.reshape(n_heads, w), jnp.full((n_heads, t_new), NEG_INF, F32)], axis=1)
    cached = _toeplitz(p, t_new, w)
    row = jnp.arange(w, dtype=jnp.int32)[None, None, :]
    min_valid = 2 * w - PAST_LEN
    cached = jnp.where(w + row >= min_valid, cached, NEG_INF)
    t = jnp.arange(t_new, dtype=jnp.int32)[None, :, None]
    i = jnp.arange(LANES, dtype=jnp.int32)[None, None, :] - (LANES - t_new)
    dist = t - i
    new = jnp.full((n_heads, t_new, LANES), NEG_INF, F32)
    for j in range((t_new - 1) // d + 1):
        new = jnp.where((i >= 0) & (dist == d * j), bias_g[:, j][:, None, None], new)
    return _stack_rows(cached), _stack_rows(new)


def _stack_heads(x, lane_head):
    return jnp.concatenate([jnp.where(lane_head == h, x, jnp.zeros_like(x)) for h in range(HEADS_PER_HALF)], axis=0)


def _unstack_heads(x, lane_head, q):
    out = jnp.where(lane_head == 0, x[0:q], 0.0)
    for h in range(1, HEADS_PER_HALF):
        out = out + jnp.where(lane_head == h, x[h * q:(h + 1) * q], 0.0)
    return out


def _residue_rows(start, d):
    return pl.ds(start, QBLOCK, stride=d) if d > 1 else pl.ds(start, QBLOCK)


def _gather_rows(ref, start, d):
    return jnp.concatenate([ref[sl, _residue_rows(start, d), :] for sl in range(ref.shape[0])], axis=1)


def _attn_prompt_kernel(q_ref, kp_ref, kc_ref, vp_ref, vc_ref, bias_ref, o_ref, lse_ref, *, d, n_sub):
    qb = QBLOCK
    j = pl.program_id(1)
    half = pl.program_id(2)
    lane_head = lax.broadcasted_iota(jnp.int32, (qb, HALF), 1) // HEAD_DIM
    lane = lax.broadcasted_iota(jnp.int32, (qb, LANES), 1)
    first = jnp.where(j == 0, 1, 0)
    for r in range(d):
        k_prev = _gather_rows(kp_ref, r, d).astype(BF16)
        v_prev = _gather_rows(vp_ref, r, d).astype(BF16)
        for sub in range(n_sub):
            start = r + d * sub * qb
            q = _gather_rows(q_ref, start, d).astype(BF16)
            k_cur = _gather_rows(kc_ref, start, d).astype(BF16)
            v_cur = _gather_rows(vc_ref, start, d).astype(BF16)
            k = jnp.concatenate([k_prev, k_cur], axis=0)
            v = jnp.concatenate([v_prev, v_cur], axis=0)
            qs = _stack_heads(q, lane_head)
            s = lax.dot_general(qs, k, (((1,), (1,)), ((), ())), preferred_element_type=F32)
            s = s + (bias_ref[first, half] if sub == 0 else bias_ref[0, half])
            m = jnp.max(s, axis=-1, keepdims=True)
            p = jnp.exp(s - m)
            den = jnp.sum(p, axis=-1, keepdims=True)
            ov = jnp.dot((p / den).astype(BF16), v, preferred_element_type=F32)
            o = _unstack_heads(ov, lane_head, qb)
            rows = _residue_rows(start, d)
            for sl in range(o_ref.shape[0]):
                o_ref[sl, rows, :] = o[:, sl * LANES:(sl + 1) * LANES]
            lse = m + jnp.log(den)
            lse_tile = jnp.zeros((qb, LANES), F32)
            for h in range(HEADS_PER_HALF):
                lse_tile = jnp.where(lane == half * HEADS_PER_HALF + h, lse[h * qb:(h + 1) * qb], lse_tile)
            lse_ref[0, rows, :] = lse_tile
            k_prev, v_prev = k_cur, v_cur


def _attn_prompt(q_sl, kv_sl, bias_tiles, g, b, s):
    w, d = GROUPS[g]
    tb = d * ATTN_QUERIES[g]
    pb = d * QBLOCK
    n_tb = s // tb
    sph = HALF // LANES
    n_half = ATTN_WIDTH // HALF
    cur = lambda i, j: i * n_tb + j
    prv = lambda i, j: jnp.maximum(i * (s // pb) + j * (tb // pb) - 1, 0)
    return pl.pallas_call(
        functools.partial(_attn_prompt_kernel, d=d, n_sub=ATTN_QUERIES[g] // QBLOCK),
        grid=(b, n_tb, n_half),
        in_specs=[
            pl.BlockSpec((sph, tb, LANES), lambda i, j, hf: (g * n_half + hf, cur(i, j), 0)),
            pl.BlockSpec((sph, pb, LANES), lambda i, j, hf: (g * 2 * n_half + hf, prv(i, j), 0)),
            pl.BlockSpec((sph, tb, LANES), lambda i, j, hf: (g * 2 * n_half + hf, cur(i, j), 0)),
            pl.BlockSpec((sph, pb, LANES), lambda i, j, hf: (g * 2 * n_half + n_half + hf, prv(i, j), 0)),
            pl.BlockSpec((sph, tb, LANES), lambda i, j, hf: (g * 2 * n_half + n_half + hf, cur(i, j), 0)),
            _const_spec(bias_tiles.shape),
        ],
        out_specs=[pl.BlockSpec((sph, tb, LANES), lambda i, j, hf: (hf, cur(i, j), 0)),
                   pl.BlockSpec((1, tb, LANES), lambda i, j, hf: (hf, cur(i, j), 0))],
        out_shape=[jax.ShapeDtypeStruct((n_half * sph, b * s, LANES), F32),
                   jax.ShapeDtypeStruct((n_half, b * s, LANES), F32)],
        compiler_params=_params(3),
        name=f"attn_prompt_g{g}",
    )(q_sl, kv_sl, kv_sl, kv_sl, kv_sl, bias_tiles)


def _expand_heads(w, e_ref):
    hi = w.astype(BF16)
    lo = (w - hi.astype(F32)).astype(BF16)
    e = e_ref[...]
    return jnp.dot(hi, e, preferred_element_type=F32) + jnp.dot(lo, e, preferred_element_type=F32)


def _merged_attention(o_refs, l_refs, e_ref, wo_ref):
    ls = [l_ref[0] + l_ref[1] for l_ref in l_refs]
    mx = jnp.maximum(jnp.maximum(ls[0], ls[1]), ls[2])
    es = [jnp.exp(l - mx) for l in ls]
    tot = es[0] + es[1] + es[2]
    merged = None
    for e, o_ref in zip(es, o_refs):
        o = jnp.concatenate([o_ref[sl] for sl in range(o_ref.shape[0])], axis=1)
        term = _expand_heads(e / tot, e_ref) * o
        merged = term if merged is None else merged + term
    return jnp.dot(merged.astype(BF16), wo_ref[...], preferred_element_type=F32)


def _head_expander():
    head = jnp.arange(LANES, dtype=jnp.int32)[:, None]
    lane = jnp.arange(ATTN_WIDTH, dtype=jnp.int32)[None, :]
    return (lane // HEAD_DIM == head).astype(BF16)


def _attn_sample_kernel(*refs):
    (q0_ref, q1_ref, q2_ref, kn0_ref, kn1_ref, kn2_ref, vn0_ref, vn1_ref, vn2_ref, c0_ref, c1_ref, c2_ref,
     bc0_ref, bc1_ref, bc2_ref, bn0_ref, bn1_ref, bn2_ref, o_ref, n0_ref, n1_ref, n2_ref) = refs
    half = pl.program_id(1)
    t = q0_ref.shape[1]
    lane_head = lax.broadcasted_iota(jnp.int32, (t, HALF), 1) // HEAD_DIM
    is_new = lax.broadcasted_iota(jnp.int32, (HALF, LANES), 1) >= LANES - t
    zpad = jnp.zeros((LANES - t, HALF), F32)
    outs, lses = [], []
    for q_ref, kn_ref, vn_ref, c_ref, bc_ref, bn_ref, n_ref in (
            (q0_ref, kn0_ref, vn0_ref, c0_ref, bc0_ref, bn0_ref, n0_ref),
            (q1_ref, kn1_ref, vn1_ref, c1_ref, bc1_ref, bn1_ref, n1_ref),
            (q2_ref, kn2_ref, vn2_ref, c2_ref, bc2_ref, bn2_ref, n2_ref)):
        w = c_ref.shape[3]
        qs = _stack_heads(q_ref[0], lane_head)
        k_t = c_ref[0, 0]
        v_t = c_ref[0, 1]
        k_new = jnp.concatenate([zpad, kn_ref[0]], axis=0)
        v_new = jnp.concatenate([zpad, vn_ref[0]], axis=0)
        k_new_t = k_new.T
        v_new_t = v_new.T
        s_c = jnp.dot(qs, k_t.astype(BF16), preferred_element_type=F32) + bc_ref[half]
        s_n = jnp.dot(qs, k_new_t.astype(BF16), preferred_element_type=F32) + bn_ref[half]
        m = jnp.maximum(s_c.max(axis=-1, keepdims=True), s_n.max(axis=-1, keepdims=True))
        p_c = jnp.exp(s_c - m)
        p_n = jnp.exp(s_n - m)
        den = p_c.sum(axis=-1, keepdims=True) + p_n.sum(axis=-1, keepdims=True)
        ov = lax.dot_general((p_c / den).astype(BF16), v_t.astype(BF16), (((1,), (1,)), ((), ())),
                             preferred_element_type=F32)
        ov = ov + jnp.dot((p_n / den).astype(BF16), v_new.astype(BF16), preferred_element_type=F32)
        outs.append(ov)
        lses.append(m + jnp.log(den))
        for kv, (old_t, new_t) in enumerate(((k_t, k_new_t), (v_t, v_new_t))):
            rolled = pltpu.roll(old_t, w - t, axis=1)
            if w > LANES:
                n_ref[0, kv, :, 0:w - LANES] = rolled[:, 0:w - LANES]
            n_ref[0, kv, :, w - LANES:w] = jnp.where(is_new, new_t, rolled[:, w - LANES:w])
    mx = jnp.maximum(jnp.maximum(lses[0], lses[1]), lses[2])
    es = [jnp.exp(l - mx) for l in lses]
    tot = es[0] + es[1] + es[2]
    merged = (es[0] / tot) * outs[0] + (es[1] / tot) * outs[1] + (es[2] / tot) * outs[2]
    o_ref[0] = _unstack_heads(merged, lane_head, t).astype(o_ref.dtype)


def _attn_sample(q, kv_new, caches_t, bias_gs):
    b, t, _ = q.shape
    n_half = ATTN_WIDTH // HALF
    tok = lambda col: pl.BlockSpec((1, t, HALF), lambda i, hf, col=col: (i, 0, col + hf))
    in_specs = [tok(g * n_half) for g in range(N_GROUPS)]
    in_specs += [tok(g * 2 * n_half) for g in range(N_GROUPS)]
    in_specs += [tok(g * 2 * n_half + n_half) for g in range(N_GROUPS)]
    args = [q] * N_GROUPS + [kv_new] * (2 * N_GROUPS)
    cache_specs = [pl.BlockSpec((1, 2, HALF, GROUPS[g][0]), lambda i, hf: (i, 0, hf, 0)) for g in range(N_GROUPS)]
    in_specs += cache_specs
    args += list(caches_t)
    tiles = [_sample_bias_tiles(bias_gs[g], g, t) for g in range(N_GROUPS)]
    for part in range(2):
        for g in range(N_GROUPS):
            in_specs.append(_const_spec(tiles[g][part].shape))
            args.append(tiles[g][part])
    res = pl.pallas_call(
        _attn_sample_kernel,
        grid=(b, n_half),
        in_specs=in_specs,
        out_specs=[pl.BlockSpec((1, t, HALF), lambda i, hf: (i, 0, hf))] + cache_specs,
        out_shape=[jax.ShapeDtypeStruct((b, t, ATTN_WIDTH), BF16)]
        + [jax.ShapeDtypeStruct(c.shape, F32) for c in caches_t],
        compiler_params=_params(2),
        name="attn_sample",
    )(*args)
    return res[0].reshape(b * t, ATTN_WIDTH), res[1:]


def _out_proj_kernel(x_ref, a_ref, wo_ref, out_ref):
    out_ref[...] = x_ref[...] + jnp.dot(a_ref[...], wo_ref[...], preferred_element_type=F32)


def _out_proj(x, a, w_o):
    n, d = x.shape
    return pl.pallas_call(
        _out_proj_kernel,
        out_shape=jax.ShapeDtypeStruct((n, d), F32),
        compiler_params=pltpu.CompilerParams(vmem_limit_bytes=VMEM_LIMIT),
        name="out_proj",
    )(x, a, w_o)


def _cache_view(c_t, b):
    w = c_t.shape[-1]
    return jnp.transpose(c_t.reshape(b, 2, HEADS_PER_GROUP, HEAD_DIM, w), (0, 4, 1, 2, 3))


def _trunk(x, conv_prev, kv_bufs, wts, conv_tile):
    b, t, d = x.shape
    n = b * t
    xf = x.reshape(n, d)
    norms = wts["norms"]
    w_in, w_out = wts["ffn_w_in"], wts["ffn_w_out"]
    xf = _ffn(xf, norms[0, 0], w_in, w_out, (0, 0))
    conv_args = (norms[0, 1], wts["conv_w_pw1"][0], wts["conv_b_pw1"][0], wts["conv_w_dw"][0], wts["conv_b_dw"][0],
                 wts["conv_ln_g"][0], wts["conv_ln_b"][0], wts["conv_w_pw2"][0], wts["conv_b_pw2"][0])
    if conv_prev is None:
        x3, conv_state = _conv_module_pipelined(xf.reshape(b, t, d), *conv_args)
    else:
        x3, conv_state = _conv_module(xf.reshape(b, t, d), conv_prev, *conv_args, *conv_tile)
    xf = x3.reshape(n, d)
    xf = _ffn(xf, norms[0, 2], w_in, w_out, (0, 1))
    bias_gs = [_group_bias(wts["rel_bias"], g) for g in range(N_GROUPS)]
    scale = HEAD_DIM ** -0.5
    if kv_bufs is None:
        shared, *kv_t = _kv_prompt(xf.reshape(b, t, d), wts["norm_kv"], wts["w_kv"])
        new_kv = [_cache_view(c, b) for c in kv_t]
        xf, q_sl = _ffn(xf, norms[1, 0], w_in, w_out, (1, 0), q_proj=(norms[1, 1], wts["attn_w_q"][0], scale))
        outs, lses = [], []
        for g in range(N_GROUPS):
            o, lse = _attn_prompt(q_sl, shared, _prompt_bias_tiles(bias_gs[g]), g, b, t)
            outs.append(o)
            lses.append(lse)
        merge = (outs, lses, wts["attn_w_o"][0])
    else:
        shared = _proj(xf, wts["norm_kv"], wts["w_kv"], F32).reshape(b, t, -1)
        xf = _ffn(xf, norms[1, 0], w_in, w_out, (1, 0))
        q = _proj(xf, norms[1, 1], wts["attn_w_q"][0], BF16, scale=scale).reshape(b, t, -1)
        caches_t = [jnp.transpose(buf, (0, 2, 3, 4, 1)).reshape(b, 2, ATTN_WIDTH, buf.shape[1]) for buf in kv_bufs]
        attn, new_t = _attn_sample(q, shared, caches_t, bias_gs)
        xf = _out_proj(xf, attn, wts["attn_w_o"][0])
        new_kv = [_cache_view(c, b) for c in new_t]
        merge = None
    xf = _ffn(xf, norms[1, 2], w_in, w_out, (1, 1), merge=merge, g_final=wts["norm_final"])
    return xf.reshape(b, t, d), conv_state[None], new_kv


def kernel(x_prompt, x_sample, state_conv, cache_kv_w128, cache_kv_w512, cache_kv_w2048, norms, ffn_w_in, ffn_w_out,
           conv_w_pw1, conv_b_pw1, conv_w_dw, conv_b_dw, conv_ln_g, conv_ln_b, conv_w_pw2, conv_b_pw2, norm_kv, w_kv,
           attn_w_q, attn_w_o, rel_bias, norm_final):
    wts = dict(
        norms=norms, ffn_w_in=ffn_w_in.astype(BF16), ffn_w_out=ffn_w_out.astype(BF16),
        conv_w_pw1=conv_w_pw1.astype(BF16), conv_b_pw1=conv_b_pw1, conv_w_dw=conv_w_dw, conv_b_dw=conv_b_dw,
        conv_ln_g=conv_ln_g, conv_ln_b=conv_ln_b, conv_w_pw2=conv_w_pw2.astype(BF16), conv_b_pw2=conv_b_pw2,
        norm_kv=norm_kv, w_kv=w_kv.astype(BF16), attn_w_q=attn_w_q.astype(BF16), attn_w_o=attn_w_o.astype(BF16),
        rel_bias=rel_bias, norm_final=norm_final)
    y_prompt, conv_prompt, kv_prompt = _trunk(x_prompt, None, None, wts, conv_tile=(1, TOKEN_TILE))
    y_sample, conv_sample, kv_sample = _trunk(
        x_sample, state_conv[0], (cache_kv_w128, cache_kv_w512, cache_kv_w2048), wts,
        conv_tile=(SUBLANES, x_sample.shape[1]))
    return (y_prompt, y_sample, conv_prompt, conv_sample, kv_prompt[0], kv_sample[0],
            kv_prompt[1], kv_sample[1], kv_prompt[2], kv_sample[2])
```

```python
import functools
import math

import jax
import jax.numpy as jnp
from jax import lax
from jax.experimental import pallas as pl
from jax.experimental.pallas import tpu as pltpu

F32 = jnp.float32
BF16 = jnp.bfloat16

EPS = 1e-6
NEG_INF = -1e30
GROUPS = ((128, 1), (512, 4), (2048, 16))
N_GROUPS = len(GROUPS)
HEADS_PER_GROUP = 8
HEAD_DIM = 64
ATTN_WIDTH = HEADS_PER_GROUP * HEAD_DIM
ROW_WIDTH = 2 * ATTN_WIDTH
N_KEYS = 129
N_BUCKETS = 32
MAX_DISTANCE = 2048
PAST_LEN = 8192
CONV_WIDTH = 31
CONV_HIST = 32
LANES = 128
SUBLANES = 8
HALF = 256
HEADS_PER_HALF = HALF // HEAD_DIM
QBLOCK = 128
ATTN_QUERIES = (1024, 256, 128)
TOKEN_TILE = 512
CONV_ROWS = 64
VMEM_LIMIT = 56 * 1024 * 1024


def _params(n_axes):
    return pltpu.CompilerParams(dimension_semantics=("arbitrary",) * n_axes, vmem_limit_bytes=VMEM_LIMIT)


def _const_spec(shape):
    zeros = (0,) * len(shape)
    return pl.BlockSpec(shape, lambda *_: zeros, pipeline_mode=pl.Buffered(1))


def _rms(x, g):
    return x * lax.rsqrt(jnp.mean(x * x, axis=-1, keepdims=True) + EPS) * g


def _silu(x):
    return x * jax.nn.sigmoid(x)


def _ffn_kernel(*refs, d_ff, n_chunk, has_merge, has_final, q_scale, has_sample_attn):
    refs = list(refs)
    x_ref = refs.pop(0)
    if has_merge:
        attn_refs = [refs.pop(0) for _ in range(2 * N_GROUPS)]
        e_ref, wo_ref = refs.pop(0), refs.pop(0)
    g_ref, win_ref, wout_ref = refs.pop(0), refs.pop(0), refs.pop(0)
    if has_final:
        gf_ref = refs.pop(0)
    if q_scale is not None:
        gq_ref, wq_ref = refs.pop(0), refs.pop(0)
    if has_sample_attn:
        sample_in = [refs.pop(0) for _ in range(N_SAMPLE_ATTN_INPUTS)]
    o_ref = refs.pop(0)
    if q_scale is not None:
        q_ref = refs.pop(0)
    if has_sample_attn:
        sample_out = [refs.pop(0) for _ in range(N_SAMPLE_ATTN_OUTPUTS)]
    act_ref = refs.pop(0)

    if has_sample_attn:
        _attn_sample_unit(sample_in, sample_out, pl.program_id(0) % (ATTN_WIDTH // HALF))

    x = x_ref[...]
    if has_merge:
        x = x + _merged_attention(attn_refs[:N_GROUPS], attn_refs[N_GROUPS:], e_ref, wo_ref)
    h = _rms(x, g_ref[...]).astype(BF16)
    for c in range(d_ff // n_chunk):
        lo = c * n_chunk
        gate = jnp.dot(h, win_ref[:, lo:lo + n_chunk], preferred_element_type=F32)
        up = jnp.dot(h, win_ref[:, d_ff + lo:d_ff + lo + n_chunk], preferred_element_type=F32)
        act_ref[:, lo:lo + n_chunk] = (_silu(gate) * up).astype(BF16)
    y = jnp.dot(act_ref[...], wout_ref[...], preferred_element_type=F32)
    o = x + 0.5 * y
    if q_scale is not None:
        hq = _rms(o, gq_ref[...]).astype(BF16)
        _store_slabs(q_ref, jnp.dot(hq, wq_ref[...], preferred_element_type=F32) * q_scale)
    if has_final:
        o = _rms(o, gf_ref[...])
    o_ref[...] = o


def _ffn(x, g, w_in, w_out, idx, *, merge=None, g_final=None, q_proj=None, sample_attn=None):
    n, d = x.shape
    d_ff = w_out.shape[-2]
    tm = min(TOKEN_TILE, n)
    if sample_attn is not None:
        n_units = sample_attn[0].shape[0] * (ATTN_WIDTH // HALF)
        tm = n // n_units
        assert tm * n_units == n and tm % SUBLANES == 0, (n, n_units)
    tok = lambda width: pl.BlockSpec((tm, width), lambda i: (i, 0))
    slabs = lambda count: pl.BlockSpec((count, tm, LANES), lambda i: (0, i, 0))
    stacked = lambda rows, cols: pl.BlockSpec((None, None, rows, cols), lambda i: (*idx, 0, 0),
                                              pipeline_mode=pl.Buffered(1))
    in_specs = [tok(d)]
    args = [x]
    if merge is not None:
        outs, lses, w_o = merge
        in_specs += [slabs(ATTN_WIDTH // LANES)] * N_GROUPS + [slabs(ATTN_WIDTH // HALF)] * N_GROUPS
        in_specs += [_const_spec((LANES, ATTN_WIDTH)), _const_spec((ATTN_WIDTH, d))]
        args += [*outs, *lses, _head_expander(), w_o]
    in_specs += [_const_spec((1, d)), stacked(d, 2 * d_ff), stacked(d_ff, d)]
    args += [g.reshape(1, d), w_in, w_out]
    if g_final is not None:
        in_specs.append(_const_spec((1, d)))
        args.append(g_final.reshape(1, d))
    out_specs = [tok(d)]
    out_shape = [jax.ShapeDtypeStruct((n, d), F32)]
    q_scale = None
    if q_proj is not None:
        g_q, w_q, q_scale = q_proj
        n_q = w_q.shape[1]
        in_specs += [_const_spec((1, d)), _const_spec((d, n_q))]
        args += [g_q.reshape(1, d), w_q]
        out_specs.append(slabs(n_q // LANES))
        out_shape.append(jax.ShapeDtypeStruct((n_q // LANES, n, LANES), F32))
    if sample_attn is not None:
        s_in_specs, s_args, s_out_specs, s_out_shape = _attn_sample_operands(*sample_attn)
        in_specs += s_in_specs
        args += s_args
        out_specs += s_out_specs
        out_shape += s_out_shape
    res = pl.pallas_call(
        functools.partial(_ffn_kernel, d_ff=d_ff, n_chunk=HALF, has_merge=merge is not None,
                          has_final=g_final is not None, q_scale=q_scale, has_sample_attn=sample_attn is not None),
        grid=(n // tm,),
        in_specs=in_specs,
        out_specs=out_specs,
        out_shape=out_shape,
        scratch_shapes=[pltpu.VMEM((tm, d_ff), BF16)],
        compiler_params=_params(1),
        name="ffn",
    )(*args)
    return res[0] if len(res) == 1 else res


def _proj_kernel(x_ref, g_ref, w_ref, o_ref, *, scale):
    h = _rms(x_ref[...], g_ref[...]).astype(BF16)
    y = jnp.dot(h, w_ref[...], preferred_element_type=F32)
    if scale != 1.0:
        y = y * scale
    o_ref[...] = y.astype(o_ref.dtype)


def _proj(x, g, w, out_dtype, scale=1.0):
    n, d = x.shape
    n_out = w.shape[1]
    tm = min(TOKEN_TILE, n)
    return pl.pallas_call(
        functools.partial(_proj_kernel, scale=scale),
        grid=(n // tm,),
        in_specs=[pl.BlockSpec((tm, d), lambda i: (i, 0)), _const_spec((1, d)), _const_spec((d, n_out))],
        out_specs=pl.BlockSpec((tm, n_out), lambda i: (i, 0)),
        out_shape=jax.ShapeDtypeStruct((n, n_out), out_dtype),
        compiler_params=_params(1),
        name="proj",
    )(x, g.reshape(1, d), w)


def _store_slabs(slab_ref, y):
    for sl in range(slab_ref.shape[0]):
        slab_ref[sl] = y[:, sl * LANES:(sl + 1) * LANES]


def _kv_prompt_kernel(x_ref, g_ref, w_ref, sh_ref, t0_ref, t1_ref, t2_ref, *, n_t, tm):
    j = pl.program_id(1)
    h = _rms(x_ref[0], g_ref[...]).astype(BF16)
    y = jnp.dot(h, w_ref[...], preferred_element_type=F32)
    _store_slabs(sh_ref, y)
    for g, t_ref in enumerate((t0_ref, t1_ref, t2_ref)):
        keep = min(GROUPS[g][0], n_t * tm)
        rows = min(keep, tm)

        @pl.when(j >= n_t - max(keep // tm, 1))
        def _(g=g, t_ref=t_ref, rows=rows):
            t_ref[0] = y[tm - rows:, g * ROW_WIDTH:(g + 1) * ROW_WIDTH].T


def _kv_prompt(x, g, w):
    b, s, d = x.shape
    n_out = w.shape[1]
    tm = min(TOKEN_TILE, s)
    n_t = s // tm
    out_specs = [pl.BlockSpec((n_out // LANES, tm, LANES), lambda i, j: (0, i * n_t + j, 0))]
    out_shape = [jax.ShapeDtypeStruct((n_out // LANES, b * s, LANES), F32)]
    for grp in range(N_GROUPS):
        keep = min(GROUPS[grp][0], s)
        first = n_t - max(keep // tm, 1)
        out_specs.append(pl.BlockSpec((1, ROW_WIDTH, min(keep, tm)),
                                      lambda i, j, first=first: (i, 0, jnp.maximum(j - first, 0))))
        out_shape.append(jax.ShapeDtypeStruct((b, ROW_WIDTH, keep), F32))
    return pl.pallas_call(
        functools.partial(_kv_prompt_kernel, n_t=n_t, tm=tm),
        grid=(b, n_t),
        in_specs=[pl.BlockSpec((1, tm, d), lambda i, j: (i, j, 0)), _const_spec((1, d)), _const_spec((d, n_out))],
        out_specs=out_specs,
        out_shape=out_shape,
        compiler_params=_params(2),
        name="kv_prompt",
    )(x, g.reshape(1, d), w)


def _conv_kernel(*refs, bb, ts, n_t, rc, has_prev):
    if has_prev:
        (x_ref, prev_ref, g_ref, w1_ref, b1_ref, wdw_ref, bdw_ref, lng_ref, lnb_ref, w2_ref, b2_ref,
         o_ref, st_ref, uext_ref, ush_ref, conv_ref) = refs
    else:
        (x_ref, g_ref, w1_ref, b1_ref, wdw_ref, bdw_ref, lng_ref, lnb_ref, w2_ref, b2_ref,
         o_ref, st_ref, uext_ref, ush_ref, conv_ref) = refs
    it = pl.program_id(1)
    d = x_ref.shape[-1]
    c_in = w2_ref.shape[0]
    x = x_ref[...].reshape(bb * ts, d)
    h = _rms(x, g_ref[...]).astype(BF16)
    ag = jnp.dot(h, w1_ref[...], preferred_element_type=F32) + b1_ref[...]
    u = ag[:, :c_in] * jax.nn.sigmoid(ag[:, c_in:])

    @pl.when(it == 0)
    def _():
        if has_prev:
            uext_ref[:, 0:CONV_HIST, :] = prev_ref[...]
        else:
            uext_ref[:, 0:CONV_HIST, :] = jnp.zeros((bb, CONV_HIST, c_in), F32)

    uext_ref[:, CONV_HIST:CONV_HIST + ts, :] = u.reshape(bb, ts, c_in)

    lead = CONV_HIST - (CONV_WIDTH - 1)
    n_lb = c_in // LANES
    for s in range(SUBLANES):
        n_sh = ts + SUBLANES * ((CONV_WIDTH - 1 - s) // SUBLANES)
        for lb in range(n_lb):
            ush_ref[s, lb, :, 0:n_sh, :] = uext_ref[:, lead + s:lead + s + n_sh, lb * LANES:(lb + 1) * LANES]

    reps = rc // SUBLANES
    for lb in range(n_lb):
        lanes = slice(lb * LANES, (lb + 1) * LANES)
        taps = [wdw_ref[k, :, lanes] for k in range(CONV_WIDTH)]
        bias = bdw_ref[:, lanes]

        def chunk(ci, carry, lb=lb, lanes=lanes, taps=taps, bias=bias):
            r0 = pl.multiple_of(ci * rc, rc)
            acc = jnp.broadcast_to(bias.reshape(1, 1, LANES), (bb, rc, LANES))
            for k in range(CONV_WIDTH):
                win = ush_ref[k % SUBLANES, lb, :, pl.ds(r0 + SUBLANES * (k // SUBLANES), rc), :]
                wk = jnp.concatenate([taps[k]] * reps, axis=0) if reps > 1 else taps[k]
                acc = acc + win * wk[None]
            conv_ref[:, pl.ds(r0, rc), lanes] = acc
            return carry

        lax.fori_loop(0, ts // rc, chunk, 0)

    acc = conv_ref[...].reshape(bb * ts, c_in)
    mu = jnp.mean(acc, axis=-1, keepdims=True)
    cen = acc - mu
    var = jnp.mean(cen * cen, axis=-1, keepdims=True)
    c = _silu(cen * lax.rsqrt(var + EPS) * lng_ref[...] + lnb_ref[...]).astype(BF16)
    y = jnp.dot(c, w2_ref[...], preferred_element_type=F32) + b2_ref[...]
    o_ref[...] = (x + y).reshape(bb, ts, d)

    @pl.when(it == n_t - 1)
    def _():
        st_ref[...] = uext_ref[:, ts:ts + CONV_HIST, :]

    if n_t > 1:
        uext_ref[:, 0:CONV_HIST, :] = uext_ref[:, ts:ts + CONV_HIST, :]


def _conv_module(x, prev, g, w1, b1, wdw, bdw, lng, lnb, w2, b2, bb, ts):
    b, t, d = x.shape
    c_in = w2.shape[0]
    n_t = t // ts
    rc = min(ts, CONV_ROWS)
    has_prev = prev is not None
    row = lambda v: v.reshape(1, -1)
    in_specs = [pl.BlockSpec((bb, ts, d), lambda i, j: (i, j, 0))]
    args = [x]
    if has_prev:
        pad = CONV_HIST - prev.shape[1]
        in_specs.append(pl.BlockSpec((bb, CONV_HIST, c_in), lambda i, j: (i, 0, 0)))
        args.append(jnp.pad(prev, ((0, 0), (pad, 0), (0, 0))))
    in_specs += [_const_spec((1, d)), _const_spec((d, 2 * c_in)), _const_spec((1, 2 * c_in)),
                 _const_spec((CONV_WIDTH, SUBLANES, c_in)), _const_spec((1, c_in)), _const_spec((1, c_in)),
                 _const_spec((1, c_in)), _const_spec((c_in, d)), _const_spec((1, d))]
    wdw_rep = jnp.broadcast_to(wdw[:, None, :], (CONV_WIDTH, SUBLANES, c_in))
    args += [row(g), w1, row(b1), wdw_rep, row(bdw), row(lng), row(lnb), w2, row(b2)]
    out, st = pl.pallas_call(
        functools.partial(_conv_kernel, bb=bb, ts=ts, n_t=n_t, rc=rc, has_prev=has_prev),
        grid=(b // bb, n_t),
        in_specs=in_specs,
        out_specs=[pl.BlockSpec((bb, ts, d), lambda i, j: (i, j, 0)),
                   pl.BlockSpec((bb, CONV_HIST, c_in), lambda i, j: (i, 0, 0))],
        out_shape=[jax.ShapeDtypeStruct((b, t, d), F32), jax.ShapeDtypeStruct((b, CONV_HIST, c_in), F32)],
        scratch_shapes=[pltpu.VMEM((bb, ts + CONV_HIST, c_in), F32),
                        pltpu.VMEM((SUBLANES, c_in // LANES, bb, ts + 24, LANES), F32),
                        pltpu.VMEM((bb, ts, c_in), F32)],
        compiler_params=_params(2),
        name="conv_module",
    )(*args)
    return out, st[:, CONV_HIST - (CONV_WIDTH - 1):]


def _conv_pipe_kernel(xc_ref, xp_ref, g_ref, w1_ref, b1_ref, wdw_ref, bdw_ref, lng_ref, lnb_ref, w2_ref, b2_ref,
                      o_ref, st_ref, h_ref, u0_ref, u1_ref, ush_ref, conv_ref, *, ts, n_t, rc):
    i = pl.program_id(0)
    n_lb = conv_ref.shape[0]
    lead = CONV_HIST - (CONV_WIDTH - 1)
    reps = rc // SUBLANES

    @pl.when(i == 0)
    def _():
        u1_ref[...] = jnp.zeros(u1_ref.shape, F32)

    def step(ua_ref, ub_ref):
        starts_sequence = (i % n_t) == 0
        h_ref[...] = _rms(xc_ref[...], g_ref[...]).astype(BF16)
        for lb in range(n_lb):
            lanes = slice(lb * LANES, (lb + 1) * LANES)
            ag = jnp.dot(h_ref[...], w1_ref[lb], preferred_element_type=F32) + b1_ref[lb]
            u = ag[:, :LANES] * jax.nn.sigmoid(ag[:, LANES:])
            ua_ref[lb, 0:CONV_HIST, :] = jnp.where(starts_sequence, 0.0, ub_ref[lb, ts:ts + CONV_HIST, :])
            ua_ref[lb, CONV_HIST:CONV_HIST + ts, :] = u
            buf = lb % 2
            for s in range(SUBLANES):
                n_sh = ts + SUBLANES * ((CONV_WIDTH - 1 - s) // SUBLANES)
                ush_ref[buf, s, 0:n_sh, :] = ub_ref[lb, lead + s:lead + s + n_sh, :]
            taps = [wdw_ref[k, :, lanes] for k in range(CONV_WIDTH)]
            bias = jnp.broadcast_to(bdw_ref[:, lanes], (rc, LANES))
            for ci in range(ts // rc):
                r0 = ci * rc
                acc = bias
                for k in range(CONV_WIDTH):
                    lo = r0 + SUBLANES * (k // SUBLANES)
                    wk = jnp.concatenate([taps[k]] * reps, axis=0) if reps > 1 else taps[k]
                    acc = acc + ush_ref[buf, k % SUBLANES, lo:lo + rc, :] * wk
                conv_ref[lb, r0:r0 + rc, :] = acc

        acc = jnp.concatenate([conv_ref[lb] for lb in range(n_lb)], axis=1)
        mu = jnp.mean(acc, axis=-1, keepdims=True)
        cen = acc - mu
        var = jnp.mean(cen * cen, axis=-1, keepdims=True)
        c = _silu(cen * lax.rsqrt(var + EPS) * lng_ref[...] + lnb_ref[...]).astype(BF16)
        y = jnp.dot(c, w2_ref[...], preferred_element_type=F32) + b2_ref[...]

        @pl.when(i >= 1)
        def _():
            o_ref[...] = xp_ref[...] + y

        @pl.when((i % n_t) == n_t - 1)
        def _():
            for lb in range(n_lb):
                st_ref[0, :, lb * LANES:(lb + 1) * LANES] = ua_ref[lb, ts:ts + CONV_HIST, :]

    @pl.when(i % 2 == 0)
    def _():
        step(u0_ref, u1_ref)

    @pl.when(i % 2 == 1)
    def _():
        step(u1_ref, u0_ref)


def _conv_module_pipelined(x, g, w1, b1, wdw, bdw, lng, lnb, w2, b2):
    b, t, d = x.shape
    c_in = w2.shape[0]
    ts = TOKEN_TILE
    n_t = t // ts
    n_tiles = b * n_t
    n_lb = c_in // LANES
    row = lambda v: v.reshape(1, -1)
    cols = lambda v, lb: v[..., lb * LANES:(lb + 1) * LANES]
    w1_blocks = jnp.stack([jnp.concatenate([cols(w1, lb), cols(w1, n_lb + lb)], axis=-1) for lb in range(n_lb)])
    b1_blocks = jnp.stack([jnp.concatenate([cols(row(b1), lb), cols(row(b1), n_lb + lb)], axis=-1)
                           for lb in range(n_lb)])
    wdw_rep = jnp.broadcast_to(wdw[:, None, :], (CONV_WIDTH, SUBLANES, c_in))
    last = n_tiles - 1
    out, st = pl.pallas_call(
        functools.partial(_conv_pipe_kernel, ts=ts, n_t=n_t, rc=CONV_ROWS // 2),
        grid=(n_tiles + 1,),
        in_specs=[pl.BlockSpec((ts, d), lambda i: (jnp.minimum(i, last), 0)),
                  pl.BlockSpec((ts, d), lambda i: (jnp.maximum(i - 1, 0), 0)),
                  _const_spec((1, d)), _const_spec((n_lb, d, 2 * LANES)), _const_spec((n_lb, 1, 2 * LANES)),
                  _const_spec((CONV_WIDTH, SUBLANES, c_in)), _const_spec((1, c_in)), _const_spec((1, c_in)),
                  _const_spec((1, c_in)), _const_spec((c_in, d)), _const_spec((1, d))],
        out_specs=[pl.BlockSpec((ts, d), lambda i: (jnp.maximum(i - 1, 0), 0)),
                   pl.BlockSpec((1, CONV_HIST, c_in), lambda i: (jnp.minimum(i, last) // n_t, 0, 0))],
        out_shape=[jax.ShapeDtypeStruct((b * t, d), F32), jax.ShapeDtypeStruct((b, CONV_HIST, c_in), F32)],
        scratch_shapes=[pltpu.VMEM((ts, d), BF16),
                        pltpu.VMEM((n_lb, ts + CONV_HIST, LANES), F32),
                        pltpu.VMEM((n_lb, ts + CONV_HIST, LANES), F32),
                        pltpu.VMEM((2, SUBLANES, ts + 24, LANES), F32),
                        pltpu.VMEM((n_lb, ts, LANES), F32)],
        compiler_params=_params(1),
        name="conv_pipelined",
    )(x.reshape(b * t, d), x.reshape(b * t, d), row(g), w1_blocks, b1_blocks, wdw_rep, row(bdw), row(lng), row(lnb),
      w2, row(b2))
    return out.reshape(b, t, d), st[:, CONV_HIST - (CONV_WIDTH - 1):]


def _t5_bucket(dist):
    max_exact = N_BUCKETS // 2
    d_f = jnp.maximum(dist, 1).astype(F32)
    large = max_exact + (jnp.log(d_f / max_exact) / math.log(MAX_DISTANCE / max_exact)
                         * (N_BUCKETS - max_exact)).astype(jnp.int32)
    large = jnp.minimum(large, N_BUCKETS - 1)
    return jnp.where(dist < max_exact, dist, large)


def _group_bias(rel_bias, g):
    _, d = GROUPS[g]
    dist = d * jnp.arange(N_KEYS, dtype=jnp.int32)
    b = jnp.take(rel_bias, _t5_bucket(dist), axis=0)
    return b[:, g * HEADS_PER_GROUP:(g + 1) * HEADS_PER_GROUP].T.astype(F32)


def _toeplitz(p, n_rows, n_cols):
    length = p.shape[1]
    stride = length - 1
    flat = jnp.tile(p, (1, n_rows))[:, :n_rows * stride]
    return flat.reshape(p.shape[0], n_rows, stride)[:, :, :n_cols]


def _stack_rows(t):
    return t.reshape(2, HEADS_PER_HALF * t.shape[1], t.shape[2])


def _prompt_bias_tiles(bias_g):
    neg = jnp.full((bias_g.shape[0], 2 * QBLOCK + 1 - N_KEYS), NEG_INF, F32)
    p = jnp.concatenate([bias_g[:, ::-1], neg], axis=1)
    regular = _toeplitz(p, QBLOCK, 2 * QBLOCK)
    c = jnp.arange(2 * QBLOCK, dtype=jnp.int32)[None, None, :]
    first = jnp.where(c >= QBLOCK, regular, NEG_INF)
    return jnp.stack([_stack_rows(regular), _stack_rows(first)], axis=0)


def _sample_bias_tiles(bias_g, g, t_new):
    w, d = GROUPS[g]
    n_heads = bias_g.shape[0]
    vals = bias_g[:, N_KEYS - 1:0:-1]
    strided = jnp.concatenate([vals[:, :, None], jnp.full((n_heads, N_KEYS - 1, d - 1), NEG_INF, F32)], axis=2)
    p = jnp.concatenate([strided.reshape(n_heads, w), jnp.full((n_heads, t_new), NEG_INF, F32)], axis=1)
    cached = _toeplitz(p, t_new, w)
    row = jnp.arange(w, dtype=jnp.int32)[None, None, :]
    min_valid = 2 * w - PAST_LEN
    cached = jnp.where(w + row >= min_valid, cached, NEG_INF)
    t = jnp.arange(t_new, dtype=jnp.int32)[None, :, None]
    i = jnp.arange(LANES, dtype=jnp.int32)[None, None, :] - (LANES - t_new)
    dist = t - i
    new = jnp.full((n_heads, t_new, LANES), NEG_INF, F32)
    for j in range((t_new - 1) // d + 1):
        new = jnp.where((i >= 0) & (dist == d * j), bias_g[:, j][:, None, None], new)
    return _stack_rows(cached), _stack_rows(new)


def _stack_heads(x, lane_head):
    return jnp.concatenate([jnp.where(lane_head == h, x, jnp.zeros_like(x)) for h in range(HEADS_PER_HALF)], axis=0)


def _unstack_heads(x, lane_head, q):
    out = jnp.where(lane_head == 0, x[0:q], 0.0)
    for h in range(1, HEADS_PER_HALF):
        out = out + jnp.where(lane_head == h, x[h * q:(h + 1) * q], 0.0)
    return out


def _residue_rows(start, d):
    return pl.ds(start, QBLOCK, stride=d) if d > 1 else pl.ds(start, QBLOCK)


def _gather_rows(ref, start, d):
    return jnp.concatenate([ref[sl, _residue_rows(start, d), :] for sl in range(ref.shape[0])], axis=1)


def _attn_prompt_kernel(q_ref, kp_ref, kc_ref, vp_ref, vc_ref, bias_ref, o_ref, lse_ref, *, d, n_sub):
    qb = QBLOCK
    j = pl.program_id(1)
    half = pl.program_id(2)
    lane_head = lax.broadcasted_iota(jnp.int32, (qb, HALF), 1) // HEAD_DIM
    lane = lax.broadcasted_iota(jnp.int32, (qb, LANES), 1)
    first = jnp.where(j == 0, 1, 0)
    for r in range(d):
        k_prev = _gather_rows(kp_ref, r, d).astype(BF16)
        v_prev = _gather_rows(vp_ref, r, d).astype(BF16)
        for sub in range(n_sub):
            start = r + d * sub * qb
            q = _gather_rows(q_ref, start, d).astype(BF16)
            k_cur = _gather_rows(kc_ref, start, d).astype(BF16)
            v_cur = _gather_rows(vc_ref, start, d).astype(BF16)
            k = jnp.concatenate([k_prev, k_cur], axis=0)
            v = jnp.concatenate([v_prev, v_cur], axis=0)
            qs = _stack_heads(q, lane_head)
            s = lax.dot_general(qs, k, (((1,), (1,)), ((), ())), preferred_element_type=F32)
            s = s + (bias_ref[first, half] if sub == 0 else bias_ref[0, half])
            m = jnp.max(s, axis=-1, keepdims=True)
            p = jnp.exp(s - m)
            den = jnp.sum(p, axis=-1, keepdims=True)
            ov = jnp.dot((p / den).astype(BF16), v, preferred_element_type=F32)
            o = _unstack_heads(ov, lane_head, qb)
            rows = _residue_rows(start, d)
            for sl in range(o_ref.shape[0]):
                o_ref[sl, rows, :] = o[:, sl * LANES:(sl + 1) * LANES]
            lse = m + jnp.log(den)
            lse_tile = jnp.zeros((qb, LANES), F32)
            for h in range(HEADS_PER_HALF):
                lse_tile = jnp.where(lane == half * HEADS_PER_HALF + h, lse[h * qb:(h + 1) * qb], lse_tile)
            lse_ref[0, rows, :] = lse_tile
            k_prev, v_prev = k_cur, v_cur


def _attn_prompt(q_sl, kv_sl, bias_tiles, g, b, s):
    w, d = GROUPS[g]
    tb = d * ATTN_QUERIES[g]
    pb = d * QBLOCK
    n_tb = s // tb
    sph = HALF // LANES
    n_half = ATTN_WIDTH // HALF
    cur = lambda i, j: i * n_tb + j
    prv = lambda i, j: jnp.maximum(i * (s // pb) + j * (tb // pb) - 1, 0)
    return pl.pallas_call(
        functools.partial(_attn_prompt_kernel, d=d, n_sub=ATTN_QUERIES[g] // QBLOCK),
        grid=(b, n_tb, n_half),
        in_specs=[
            pl.BlockSpec((sph, tb, LANES), lambda i, j, hf: (g * n_half + hf, cur(i, j), 0)),
            pl.BlockSpec((sph, pb, LANES), lambda i, j, hf: (g * 2 * n_half + hf, prv(i, j), 0)),
            pl.BlockSpec((sph, tb, LANES), lambda i, j, hf: (g * 2 * n_half + hf, cur(i, j), 0)),
            pl.BlockSpec((sph, pb, LANES), lambda i, j, hf: (g * 2 * n_half + n_half + hf, prv(i, j), 0)),
            pl.BlockSpec((sph, tb, LANES), lambda i, j, hf: (g * 2 * n_half + n_half + hf, cur(i, j), 0)),
            _const_spec(bias_tiles.shape),
        ],
        out_specs=[pl.BlockSpec((sph, tb, LANES), lambda i, j, hf: (hf, cur(i, j), 0)),
                   pl.BlockSpec((1, tb, LANES), lambda i, j, hf: (hf, cur(i, j), 0))],
        out_shape=[jax.ShapeDtypeStruct((n_half * sph, b * s, LANES), F32),
                   jax.ShapeDtypeStruct((n_half, b * s, LANES), F32)],
        compiler_params=_params(3),
        name=f"attn_prompt_g{g}",
    )(q_sl, kv_sl, kv_sl, kv_sl, kv_sl, bias_tiles)


def _expand_heads(w, e_ref):
    hi = w.astype(BF16)
    lo = (w - hi.astype(F32)).astype(BF16)
    e = e_ref[...]
    return jnp.dot(hi, e, preferred_element_type=F32) + jnp.dot(lo, e, preferred_element_type=F32)


def _merged_attention(o_refs, l_refs, e_ref, wo_ref):
    ls = [l_ref[0] + l_ref[1] for l_ref in l_refs]
    mx = jnp.maximum(jnp.maximum(ls[0], ls[1]), ls[2])
    es = [jnp.exp(l - mx) for l in ls]
    tot = es[0] + es[1] + es[2]
    merged = None
    for e, o_ref in zip(es, o_refs):
        o = jnp.concatenate([o_ref[sl] for sl in range(o_ref.shape[0])], axis=1)
        term = _expand_heads(e / tot, e_ref) * o
        merged = term if merged is None else merged + term
    return jnp.dot(merged.astype(BF16), wo_ref[...], preferred_element_type=F32)


def _head_expander():
    head = jnp.arange(LANES, dtype=jnp.int32)[:, None]
    lane = jnp.arange(ATTN_WIDTH, dtype=jnp.int32)[None, :]
    return (lane // HEAD_DIM == head).astype(BF16)


N_SAMPLE_ATTN_INPUTS = 6 * N_GROUPS
N_SAMPLE_ATTN_OUTPUTS = 1 + N_GROUPS


def _attn_sample_unit(in_refs, out_refs, half):
    (q0_ref, q1_ref, q2_ref, kn0_ref, kn1_ref, kn2_ref, vn0_ref, vn1_ref, vn2_ref, c0_ref, c1_ref, c2_ref,
     bc0_ref, bc1_ref, bc2_ref, bn0_ref, bn1_ref, bn2_ref) = in_refs
    o_ref, n0_ref, n1_ref, n2_ref = out_refs
    t = q0_ref.shape[1]
    lane_head = lax.broadcasted_iota(jnp.int32, (t, HALF), 1) // HEAD_DIM
    is_new = lax.broadcasted_iota(jnp.int32, (HALF, LANES), 1) >= LANES - t
    zpad = jnp.zeros((LANES - t, HALF), F32)
    outs, lses = [], []
    for q_ref, kn_ref, vn_ref, c_ref, bc_ref, bn_ref, n_ref in (
            (q0_ref, kn0_ref, vn0_ref, c0_ref, bc0_ref, bn0_ref, n0_ref),
            (q1_ref, kn1_ref, vn1_ref, c1_ref, bc1_ref, bn1_ref, n1_ref),
            (q2_ref, kn2_ref, vn2_ref, c2_ref, bc2_ref, bn2_ref, n2_ref)):
        w = c_ref.shape[3]
        qs = _stack_heads(q_ref[0], lane_head)
        k_t = c_ref[0, 0]
        v_t = c_ref[0, 1]
        k_new = jnp.concatenate([zpad, kn_ref[0]], axis=0)
        v_new = jnp.concatenate([zpad, vn_ref[0]], axis=0)
        k_new_t = k_new.T
        v_new_t = v_new.T
        s_c = jnp.dot(qs, k_t.astype(BF16), preferred_element_type=F32) + bc_ref[half]
        s_n = jnp.dot(qs, k_new_t.astype(BF16), preferred_element_type=F32) + bn_ref[half]
        m = jnp.maximum(s_c.max(axis=-1, keepdims=True), s_n.max(axis=-1, keepdims=True))
        p_c = jnp.exp(s_c - m)
        p_n = jnp.exp(s_n - m)
        den = p_c.sum(axis=-1, keepdims=True) + p_n.sum(axis=-1, keepdims=True)
        ov = lax.dot_general((p_c / den).astype(BF16), v_t.astype(BF16), (((1,), (1,)), ((), ())),
                             preferred_element_type=F32)
        ov = ov + jnp.dot((p_n / den).astype(BF16), v_new.astype(BF16), preferred_element_type=F32)
        outs.append(ov)
        lses.append(m + jnp.log(den))
        for kv, (old_t, new_t) in enumerate(((k_t, k_new_t), (v_t, v_new_t))):
            rolled = pltpu.roll(old_t, w - t, axis=1)
            if w > LANES:
                n_ref[0, kv, :, 0:w - LANES] = rolled[:, 0:w - LANES]
            n_ref[0, kv, :, w - LANES:w] = jnp.where(is_new, new_t, rolled[:, w - LANES:w])
    mx = jnp.maximum(jnp.maximum(lses[0], lses[1]), lses[2])
    es = [jnp.exp(l - mx) for l in lses]
    tot = es[0] + es[1] + es[2]
    merged = (es[0] / tot) * outs[0] + (es[1] / tot) * outs[1] + (es[2] / tot) * outs[2]
    o_ref[0] = _unstack_heads(merged, lane_head, t).astype(o_ref.dtype)


def _attn_sample_operands(q, kv_new, caches_t, bias_gs):
    b, t, _ = q.shape
    n_half = ATTN_WIDTH // HALF
    tok = lambda col: pl.BlockSpec((1, t, HALF), lambda i, col=col: (i // n_half, 0, col + i % n_half))
    in_specs = [tok(g * n_half) for g in range(N_GROUPS)]
    in_specs += [tok(g * 2 * n_half) for g in range(N_GROUPS)]
    in_specs += [tok(g * 2 * n_half + n_half) for g in range(N_GROUPS)]
    args = [q] * N_GROUPS + [kv_new] * (2 * N_GROUPS)
    cache_spec = lambda g: pl.BlockSpec((1, 2, HALF, GROUPS[g][0]), lambda i: (i // n_half, 0, i % n_half, 0))
    in_specs += [cache_spec(g) for g in range(N_GROUPS)]
    args += list(caches_t)
    tiles = [_sample_bias_tiles(bias_gs[g], g, t) for g in range(N_GROUPS)]
    for part in range(2):
        for g in range(N_GROUPS):
            in_specs.append(_const_spec(tiles[g][part].shape))
            args.append(tiles[g][part])
    out_specs = [pl.BlockSpec((1, t, HALF), lambda i: (i // n_half, 0, i % n_half))]
    out_specs += [cache_spec(g) for g in range(N_GROUPS)]
    out_shape = [jax.ShapeDtypeStruct((b, t, ATTN_WIDTH), BF16)] + [jax.ShapeDtypeStruct(c.shape, F32) for c in caches_t]
    return in_specs, args, out_specs, out_shape


def _out_proj_kernel(x_ref, a_ref, wo_ref, out_ref):
    out_ref[...] = x_ref[...] + jnp.dot(a_ref[...], wo_ref[...], preferred_element_type=F32)


def _out_proj(x, a, w_o):
    n, d = x.shape
    return pl.pallas_call(
        _out_proj_kernel,
        out_shape=jax.ShapeDtypeStruct((n, d), F32),
        compiler_params=pltpu.CompilerParams(vmem_limit_bytes=VMEM_LIMIT),
        name="out_proj",
    )(x, a, w_o)


def _cache_view(c_t, b):
    w = c_t.shape[-1]
    return jnp.transpose(c_t.reshape(b, 2, HEADS_PER_GROUP, HEAD_DIM, w), (0, 4, 1, 2, 3))


def _conv_args(wts):
    return (wts["norms"][0, 1], wts["conv_w_pw1"][0], wts["conv_b_pw1"][0], wts["conv_w_dw"][0], wts["conv_b_dw"][0],
            wts["conv_ln_g"][0], wts["conv_ln_b"][0], wts["conv_w_pw2"][0], wts["conv_b_pw2"][0])


def _sample_until_attention(x, conv_prev, kv_bufs, wts):
    b, t, d = x.shape
    norms, w_in, w_out = wts["norms"], wts["ffn_w_in"], wts["ffn_w_out"]
    xf = _ffn(x.reshape(b * t, d), norms[0, 0], w_in, w_out, (0, 0))
    x3, conv_state = _conv_module(xf.reshape(b, t, d), conv_prev, *_conv_args(wts), SUBLANES, t)
    xf = _ffn(x3.reshape(b * t, d), norms[0, 2], w_in, w_out, (0, 1))
    kv_new = _proj(xf, wts["norm_kv"], wts["w_kv"], F32).reshape(b, t, -1)
    xf = _ffn(xf, norms[1, 0], w_in, w_out, (1, 0))
    q = _proj(xf, norms[1, 1], wts["attn_w_q"][0], BF16, scale=HEAD_DIM ** -0.5).reshape(b, t, -1)
    caches_t = [jnp.transpose(buf, (0, 2, 3, 4, 1)).reshape(b, 2, ATTN_WIDTH, buf.shape[1]) for buf in kv_bufs]
    return xf, conv_state, (q, kv_new, caches_t, wts["bias_gs"])


def _sample_after_attention(xf, attn, wts):
    xf = _out_proj(xf, attn.reshape(xf.shape[0], ATTN_WIDTH), wts["attn_w_o"][0])
    return _ffn(xf, wts["norms"][1, 2], wts["ffn_w_in"], wts["ffn_w_out"], (1, 1), g_final=wts["norm_final"])


def _prompt_trunk(x, wts, sample_attn):
    b, t, d = x.shape
    norms, w_in, w_out = wts["norms"], wts["ffn_w_in"], wts["ffn_w_out"]
    xf, *sample_res = _ffn(x.reshape(b * t, d), norms[0, 0], w_in, w_out, (0, 0), sample_attn=sample_attn)
    x3, conv_state = _conv_module_pipelined(xf.reshape(b, t, d), *_conv_args(wts))
    xf = _ffn(x3.reshape(b * t, d), norms[0, 2], w_in, w_out, (0, 1))
    shared, *kv_t = _kv_prompt(xf.reshape(b, t, d), wts["norm_kv"], wts["w_kv"])
    q_proj = (norms[1, 1], wts["attn_w_q"][0], HEAD_DIM ** -0.5)
    xf, q_sl = _ffn(xf, norms[1, 0], w_in, w_out, (1, 0), q_proj=q_proj)
    outs, lses = [], []
    for g in range(N_GROUPS):
        o, lse = _attn_prompt(q_sl, shared, _prompt_bias_tiles(wts["bias_gs"][g]), g, b, t)
        outs.append(o)
        lses.append(lse)
    xf = _ffn(xf, norms[1, 2], w_in, w_out, (1, 1), merge=(outs, lses, wts["attn_w_o"][0]),
              g_final=wts["norm_final"])
    return xf.reshape(b, t, d), conv_state, [_cache_view(c, b) for c in kv_t], sample_res


def kernel(x_prompt, x_sample, state_conv, cache_kv_w128, cache_kv_w512, cache_kv_w2048, norms, ffn_w_in, ffn_w_out,
           conv_w_pw1, conv_b_pw1, conv_w_dw, conv_b_dw, conv_ln_g, conv_ln_b, conv_w_pw2, conv_b_pw2, norm_kv, w_kv,
           attn_w_q, attn_w_o, rel_bias, norm_final):
    wts = dict(
        norms=norms, ffn_w_in=ffn_w_in.astype(BF16), ffn_w_out=ffn_w_out.astype(BF16),
        conv_w_pw1=conv_w_pw1.astype(BF16), conv_b_pw1=conv_b_pw1, conv_w_dw=conv_w_dw, conv_b_dw=conv_b_dw,
        conv_ln_g=conv_ln_g, conv_ln_b=conv_ln_b, conv_w_pw2=conv_w_pw2.astype(BF16), conv_b_pw2=conv_b_pw2,
        norm_kv=norm_kv, w_kv=w_kv.astype(BF16), attn_w_q=attn_w_q.astype(BF16), attn_w_o=attn_w_o.astype(BF16),
        rel_bias=rel_bias, norm_final=norm_final)
    wts["bias_gs"] = [_group_bias(rel_bias, g) for g in range(N_GROUPS)]
    bs, ts, d = x_sample.shape
    xs, conv_sample, sample_attn = _sample_until_attention(
        x_sample, state_conv[0], (cache_kv_w128, cache_kv_w512, cache_kv_w2048), wts)
    y_prompt, conv_prompt, kv_prompt, (attn_s, *caches_new) = _prompt_trunk(x_prompt, wts, sample_attn)
    y_sample = _sample_after_attention(xs, attn_s, wts).reshape(bs, ts, d)
    kv_sample = [_cache_view(c, bs) for c in caches_new]
    return (y_prompt, y_sample, conv_prompt[None], conv_sample[None], kv_prompt[0], kv_sample[0],
            kv_prompt[1], kv_sample[1], kv_prompt[2], kv_sample[2])
```

```python
import functools
import math

import jax
import jax.numpy as jnp
from jax import lax
from jax.experimental import pallas as pl
from jax.experimental.pallas import tpu as pltpu

F32 = jnp.float32
BF16 = jnp.bfloat16

EPS = 1e-6
NEG_INF = -1e30
GROUPS = ((128, 1), (512, 4), (2048, 16))
N_GROUPS = len(GROUPS)
HEADS_PER_GROUP = 8
HEAD_DIM = 64
ATTN_WIDTH = HEADS_PER_GROUP * HEAD_DIM
ROW_WIDTH = 2 * ATTN_WIDTH
N_KEYS = 129
N_BUCKETS = 32
MAX_DISTANCE = 2048
PAST_LEN = 8192
CONV_WIDTH = 31
CONV_HIST = 32
LANES = 128
SUBLANES = 8
HALF = 256
HEADS_PER_HALF = HALF // HEAD_DIM
QBLOCK = 128
ATTN_QUERIES = (1024, 256, 128)
TOKEN_TILE = 512
CONV_ROWS = 64
VMEM_LIMIT = 56 * 1024 * 1024


def _params(n_axes):
    return pltpu.CompilerParams(dimension_semantics=("arbitrary",) * n_axes, vmem_limit_bytes=VMEM_LIMIT)


def _const_spec(shape):
    zeros = (0,) * len(shape)
    return pl.BlockSpec(shape, lambda *_: zeros, pipeline_mode=pl.Buffered(1))


def _rms(x, g):
    return x * lax.rsqrt(jnp.mean(x * x, axis=-1, keepdims=True) + EPS) * g


def _silu(x):
    return x * jax.nn.sigmoid(x)


def _ffn_kernel(*refs, d_ff, n_chunk, has_merge, has_final, q_scale, has_sample_attn):
    refs = list(refs)
    x_ref = refs.pop(0)
    if has_merge:
        attn_refs = [refs.pop(0) for _ in range(2 * N_GROUPS)]
        e_ref, wo_ref = refs.pop(0), refs.pop(0)
    g_ref, win_ref, wout_ref = refs.pop(0), refs.pop(0), refs.pop(0)
    if has_final:
        gf_ref = refs.pop(0)
    if q_scale is not None:
        gq_ref, wq_ref = refs.pop(0), refs.pop(0)
    if has_sample_attn:
        sample_in = [refs.pop(0) for _ in range(N_SAMPLE_ATTN_INPUTS)]
    o_ref = refs.pop(0)
    if q_scale is not None:
        q_ref = refs.pop(0)
    if has_sample_attn:
        sample_out = [refs.pop(0) for _ in range(N_SAMPLE_ATTN_OUTPUTS)]
    act_ref = refs.pop(0)

    if has_sample_attn:
        _attn_sample_unit(sample_in, sample_out, pl.program_id(0) % (ATTN_WIDTH // HALF))

    x = x_ref[...]
    if has_merge:
        x = x + _merged_attention(attn_refs[:N_GROUPS], attn_refs[N_GROUPS:], e_ref, wo_ref)
    h = _rms(x, g_ref[...]).astype(BF16)
    for c in range(d_ff // n_chunk):
        lo = c * n_chunk
        gate = jnp.dot(h, win_ref[:, lo:lo + n_chunk], preferred_element_type=F32)
        up = jnp.dot(h, win_ref[:, d_ff + lo:d_ff + lo + n_chunk], preferred_element_type=F32)
        act_ref[:, lo:lo + n_chunk] = (_silu(gate) * up).astype(BF16)
    y = jnp.dot(act_ref[...], wout_ref[...], preferred_element_type=F32)
    o = x + 0.5 * y
    if q_scale is not None:
        hq = _rms(o, gq_ref[...]).astype(BF16)
        _store_slabs(q_ref, jnp.dot(hq, wq_ref[...], preferred_element_type=F32) * q_scale)
    if has_final:
        o = _rms(o, gf_ref[...])
    o_ref[...] = o


def _ffn(x, g, w_in, w_out, idx, *, merge=None, g_final=None, q_proj=None, sample_attn=None):
    n, d = x.shape
    d_ff = w_out.shape[-2]
    if n <= TOKEN_TILE and merge is None and q_proj is None and sample_attn is None:
        return _ffn_small(x, g, w_in, w_out, idx, g_final)
    tm = min(TOKEN_TILE, n)
    if sample_attn is not None:
        n_units = sample_attn[0].shape[0] * (ATTN_WIDTH // HALF)
        tm = n // n_units
        assert tm * n_units == n and tm % SUBLANES == 0, (n, n_units)
    tok = lambda width: pl.BlockSpec((tm, width), lambda i: (i, 0))
    slabs = lambda count: pl.BlockSpec((count, tm, LANES), lambda i: (0, i, 0))
    stacked = lambda rows, cols: pl.BlockSpec((None, None, rows, cols), lambda i: (*idx, 0, 0),
                                              pipeline_mode=pl.Buffered(1))
    in_specs = [tok(d)]
    args = [x]
    if merge is not None:
        outs, lses, w_o = merge
        in_specs += [slabs(ATTN_WIDTH // LANES)] * N_GROUPS + [slabs(ATTN_WIDTH // HALF)] * N_GROUPS
        in_specs += [_const_spec((2 * LANES, ATTN_WIDTH)), _const_spec((ATTN_WIDTH, d))]
        args += [*outs, *lses, _head_expander(), w_o]
    in_specs += [_const_spec((1, d)), stacked(d, 2 * d_ff), stacked(d_ff, d)]
    args += [g.reshape(1, d), w_in, w_out]
    if g_final is not None:
        in_specs.append(_const_spec((1, d)))
        args.append(g_final.reshape(1, d))
    out_specs = [tok(d)]
    out_shape = [jax.ShapeDtypeStruct((n, d), F32)]
    q_scale = None
    if q_proj is not None:
        g_q, w_q, q_scale = q_proj
        n_q = w_q.shape[1]
        in_specs += [_const_spec((1, d)), _const_spec((d, n_q))]
        args += [g_q.reshape(1, d), w_q]
        out_specs.append(slabs(n_q // LANES))
        out_shape.append(jax.ShapeDtypeStruct((n_q // LANES, n, LANES), F32))
    if sample_attn is not None:
        s_in_specs, s_args, s_out_specs, s_out_shape = _attn_sample_operands(*sample_attn)
        in_specs += s_in_specs
        args += s_args
        out_specs += s_out_specs
        out_shape += s_out_shape
    res = pl.pallas_call(
        functools.partial(_ffn_kernel, d_ff=d_ff, n_chunk=HALF, has_merge=merge is not None,
                          has_final=g_final is not None, q_scale=q_scale, has_sample_attn=sample_attn is not None),
        grid=(n // tm,),
        in_specs=in_specs,
        out_specs=out_specs,
        out_shape=out_shape,
        scratch_shapes=[pltpu.VMEM((tm, d_ff), BF16)],
        compiler_params=_params(1),
        name="ffn",
    )(*args)
    return res[0] if len(res) == 1 else res


def _ffn_small_kernel(*refs, has_final):
    if has_final:
        x_ref, g_ref, wg_ref, wu_ref, wo_ref, gf_ref, o_ref, h_ref, acc_ref = refs
    else:
        x_ref, g_ref, wg_ref, wu_ref, wo_ref, o_ref, h_ref, acc_ref = refs
    c = pl.program_id(0)

    @pl.when(c == 0)
    def _():
        h_ref[...] = _rms(x_ref[...], g_ref[...]).astype(BF16)
        acc_ref[...] = jnp.zeros(acc_ref.shape, F32)

    h = h_ref[...]
    gate = jnp.dot(h, wg_ref[...], preferred_element_type=F32)
    up = jnp.dot(h, wu_ref[...], preferred_element_type=F32)
    acc_ref[...] += jnp.dot((_silu(gate) * up).astype(BF16), wo_ref[...], preferred_element_type=F32)

    @pl.when(c == pl.num_programs(0) - 1)
    def _():
        o = x_ref[...] + 0.5 * acc_ref[...]
        if has_final:
            o = _rms(o, gf_ref[...])
        o_ref[...] = o


def _ffn_small(x, g, w_in, w_out, idx, g_final=None):
    n, d = x.shape
    d_ff = w_out.shape[-2]
    n_c = d_ff // HALF
    has_final = g_final is not None
    whole = lambda shape: pl.BlockSpec(shape, lambda c: (0,) * len(shape))
    in_specs = [whole((n, d)), whole((1, d)),
                pl.BlockSpec((None, None, d, HALF), lambda c: (*idx, 0, c)),
                pl.BlockSpec((None, None, d, HALF), lambda c: (*idx, 0, n_c + c)),
                pl.BlockSpec((None, None, HALF, d), lambda c: (*idx, c, 0))]
    args = [x, g.reshape(1, d), w_in, w_in, w_out]
    if has_final:
        in_specs.append(whole((1, d)))
        args.append(g_final.reshape(1, d))
    return pl.pallas_call(
        functools.partial(_ffn_small_kernel, has_final=has_final),
        grid=(n_c,),
        in_specs=in_specs,
        out_specs=whole((n, d)),
        out_shape=jax.ShapeDtypeStruct((n, d), F32),
        scratch_shapes=[pltpu.VMEM((n, d), BF16), pltpu.VMEM((n, d), F32)],
        compiler_params=_params(1),
        name="ffn_small",
    )(*args)


def _proj_kernel(x_ref, g_ref, w_ref, o_ref, *, scale):
    h = _rms(x_ref[...], g_ref[...]).astype(BF16)
    y = jnp.dot(h, w_ref[...], preferred_element_type=F32)
    if scale != 1.0:
        y = y * scale
    o_ref[...] = y.astype(o_ref.dtype)


def _proj(x, g, w, out_dtype, scale=1.0):
    n, d = x.shape
    n_out = w.shape[1]
    tm = min(TOKEN_TILE, n)
    return pl.pallas_call(
        functools.partial(_proj_kernel, scale=scale),
        grid=(n // tm,),
        in_specs=[pl.BlockSpec((tm, d), lambda i: (i, 0)), _const_spec((1, d)), _const_spec((d, n_out))],
        out_specs=pl.BlockSpec((tm, n_out), lambda i: (i, 0)),
        out_shape=jax.ShapeDtypeStruct((n, n_out), out_dtype),
        compiler_params=_params(1),
        name="proj",
    )(x, g.reshape(1, d), w)


def _store_slabs(slab_ref, y):
    for sl in range(slab_ref.shape[0]):
        slab_ref[sl] = y[:, sl * LANES:(sl + 1) * LANES]


def _kv_prompt_kernel(x_ref, g_ref, w_ref, sh_ref, t0_ref, t1_ref, t2_ref, *, n_t, tm):
    h = _rms(x_ref[0], g_ref[...]).astype(BF16)
    y = jnp.dot(h, w_ref[...], preferred_element_type=F32)
    _store_slabs(sh_ref, y)
    for g, t_ref in enumerate((t0_ref, t1_ref, t2_ref)):
        rows = min(GROUPS[g][0], n_t * tm, tm)
        t_ref[0] = y[tm - rows:, g * ROW_WIDTH:(g + 1) * ROW_WIDTH].T


def _kv_prompt(x, g, w):
    b, s, d = x.shape
    n_out = w.shape[1]
    tm = min(TOKEN_TILE, s)
    n_t = s // tm
    out_specs = [pl.BlockSpec((n_out // LANES, tm, LANES), lambda i, j: (0, i * n_t + j, 0))]
    out_shape = [jax.ShapeDtypeStruct((n_out // LANES, b * s, LANES), F32)]
    for grp in range(N_GROUPS):
        keep = min(GROUPS[grp][0], s)
        first = n_t - max(keep // tm, 1)
        out_specs.append(pl.BlockSpec((1, ROW_WIDTH, min(keep, tm)),
                                      lambda i, j, first=first: (i, 0, jnp.maximum(j - first, 0))))
        out_shape.append(jax.ShapeDtypeStruct((b, ROW_WIDTH, keep), F32))
    return pl.pallas_call(
        functools.partial(_kv_prompt_kernel, n_t=n_t, tm=tm),
        grid=(b, n_t),
        in_specs=[pl.BlockSpec((1, tm, d), lambda i, j: (i, j, 0)), _const_spec((1, d)), _const_spec((d, n_out))],
        out_specs=out_specs,
        out_shape=out_shape,
        compiler_params=_params(2),
        name="kv_prompt",
    )(x, g.reshape(1, d), w)


def _conv_kernel(*refs, bb, ts, n_t, rc, has_prev):
    if has_prev:
        (x_ref, prev_ref, g_ref, w1_ref, b1_ref, wdw_ref, bdw_ref, lng_ref, lnb_ref, w2_ref, b2_ref,
         o_ref, st_ref, uext_ref, ush_ref, conv_ref) = refs
    else:
        (x_ref, g_ref, w1_ref, b1_ref, wdw_ref, bdw_ref, lng_ref, lnb_ref, w2_ref, b2_ref,
         o_ref, st_ref, uext_ref, ush_ref, conv_ref) = refs
    it = pl.program_id(1)
    d = x_ref.shape[-1]
    c_in = w2_ref.shape[0]
    x = x_ref[...].reshape(bb * ts, d)
    h = _rms(x, g_ref[...]).astype(BF16)
    ag = jnp.dot(h, w1_ref[...], preferred_element_type=F32) + b1_ref[...]
    u = ag[:, :c_in] * jax.nn.sigmoid(ag[:, c_in:])

    @pl.when(it == 0)
    def _():
        if has_prev:
            uext_ref[:, 0:CONV_HIST, :] = prev_ref[...]
        else:
            uext_ref[:, 0:CONV_HIST, :] = jnp.zeros((bb, CONV_HIST, c_in), F32)

    uext_ref[:, CONV_HIST:CONV_HIST + ts, :] = u.reshape(bb, ts, c_in)

    lead = CONV_HIST - (CONV_WIDTH - 1)
    n_lb = c_in // LANES
    for s in range(SUBLANES):
        n_sh = ts + SUBLANES * ((CONV_WIDTH - 1 - s) // SUBLANES)
        for lb in range(n_lb):
            ush_ref[s, lb, :, 0:n_sh, :] = uext_ref[:, lead + s:lead + s + n_sh, lb * LANES:(lb + 1) * LANES]

    reps = rc // SUBLANES
    for lb in range(n_lb):
        lanes = slice(lb * LANES, (lb + 1) * LANES)
        taps = [wdw_ref[k, :, lanes] for k in range(CONV_WIDTH)]
        bias = bdw_ref[:, lanes]

        def chunk(ci, carry, lb=lb, lanes=lanes, taps=taps, bias=bias):
            r0 = pl.multiple_of(ci * rc, rc)
            acc = jnp.broadcast_to(bias.reshape(1, 1, LANES), (bb, rc, LANES))
            for k in range(CONV_WIDTH):
                win = ush_ref[k % SUBLANES, lb, :, pl.ds(r0 + SUBLANES * (k // SUBLANES), rc), :]
                wk = jnp.concatenate([taps[k]] * reps, axis=0) if reps > 1 else taps[k]
                acc = acc + win * wk[None]
            conv_ref[:, pl.ds(r0, rc), lanes] = acc
            return carry

        lax.fori_loop(0, ts // rc, chunk, 0)

    acc = conv_ref[...].reshape(bb * ts, c_in)
    mu = jnp.mean(acc, axis=-1, keepdims=True)
    cen = acc - mu
    var = jnp.mean(cen * cen, axis=-1, keepdims=True)
    c = _silu(cen * lax.rsqrt(var + EPS) * lng_ref[...] + lnb_ref[...]).astype(BF16)
    y = jnp.dot(c, w2_ref[...], preferred_element_type=F32) + b2_ref[...]
    o_ref[...] = (x + y).reshape(bb, ts, d)

    @pl.when(it == n_t - 1)
    def _():
        st_ref[...] = uext_ref[:, ts:ts + CONV_HIST, :]

    if n_t > 1:
        uext_ref[:, 0:CONV_HIST, :] = uext_ref[:, ts:ts + CONV_HIST, :]


def _conv_module(x, prev, g, w1, b1, wdw, bdw, lng, lnb, w2, b2, bb, ts):
    b, t, d = x.shape
    c_in = w2.shape[0]
    n_t = t // ts
    rc = min(ts, CONV_ROWS)
    has_prev = prev is not None
    row = lambda v: v.reshape(1, -1)
    in_specs = [pl.BlockSpec((bb, ts, d), lambda i, j: (i, j, 0))]
    args = [x]
    if has_prev:
        pad = CONV_HIST - prev.shape[1]
        in_specs.append(pl.BlockSpec((bb, CONV_HIST, c_in), lambda i, j: (i, 0, 0)))
        args.append(jnp.pad(prev, ((0, 0), (pad, 0), (0, 0))))
    in_specs += [_const_spec((1, d)), _const_spec((d, 2 * c_in)), _const_spec((1, 2 * c_in)),
                 _const_spec((CONV_WIDTH, SUBLANES, c_in)), _const_spec((1, c_in)), _const_spec((1, c_in)),
                 _const_spec((1, c_in)), _const_spec((c_in, d)), _const_spec((1, d))]
    wdw_rep = jnp.broadcast_to(wdw[:, None, :], (CONV_WIDTH, SUBLANES, c_in))
    args += [row(g), w1, row(b1), wdw_rep, row(bdw), row(lng), row(lnb), w2, row(b2)]
    out, st = pl.pallas_call(
        functools.partial(_conv_kernel, bb=bb, ts=ts, n_t=n_t, rc=rc, has_prev=has_prev),
        grid=(b // bb, n_t),
        in_specs=in_specs,
        out_specs=[pl.BlockSpec((bb, ts, d), lambda i, j: (i, j, 0)),
                   pl.BlockSpec((bb, CONV_HIST, c_in), lambda i, j: (i, 0, 0))],
        out_shape=[jax.ShapeDtypeStruct((b, t, d), F32), jax.ShapeDtypeStruct((b, CONV_HIST, c_in), F32)],
        scratch_shapes=[pltpu.VMEM((bb, ts + CONV_HIST, c_in), F32),
                        pltpu.VMEM((SUBLANES, c_in // LANES, bb, ts + 24, LANES), F32),
                        pltpu.VMEM((bb, ts, c_in), F32)],
        compiler_params=_params(2),
        name="conv_module",
    )(*args)
    return out, st[:, CONV_HIST - (CONV_WIDTH - 1):]


def _conv_pipe_kernel(xc_ref, xp_ref, g_ref, w1_ref, b1_ref, wdw_ref, bdw_ref, lng_ref, lnb_ref, w2_ref, b2_ref,
                      o_ref, st_ref, h_ref, u0_ref, u1_ref, ush_ref, conv_ref, *, ts, n_t, rc):
    i = pl.program_id(0)
    n_lb = conv_ref.shape[0]
    lead = CONV_HIST - (CONV_WIDTH - 1)
    reps = rc // SUBLANES

    @pl.when(i == 0)
    def _():
        u1_ref[...] = jnp.zeros(u1_ref.shape, F32)

    def step(ua_ref, ub_ref):
        starts_sequence = (i % n_t) == 0
        h_ref[...] = _rms(xc_ref[...], g_ref[...]).astype(BF16)
        for lb in range(n_lb):
            lanes = slice(lb * LANES, (lb + 1) * LANES)
            ag = jnp.dot(h_ref[...], w1_ref[lb], preferred_element_type=F32) + b1_ref[lb]
            u = ag[:, :LANES] * jax.nn.sigmoid(ag[:, LANES:])
            ua_ref[lb, 0:CONV_HIST, :] = jnp.where(starts_sequence, 0.0, ub_ref[lb, ts:ts + CONV_HIST, :])
            ua_ref[lb, CONV_HIST:CONV_HIST + ts, :] = u
            buf = lb % 2
            for s in range(SUBLANES):
                n_sh = ts + SUBLANES * ((CONV_WIDTH - 1 - s) // SUBLANES)
                ush_ref[buf, s, 0:n_sh, :] = ub_ref[lb, lead + s:lead + s + n_sh, :]
            taps = [wdw_ref[k, :, lanes] for k in range(CONV_WIDTH)]
            bias = jnp.broadcast_to(bdw_ref[:, lanes], (rc, LANES))
            for ci in range(ts // rc):
                r0 = ci * rc
                acc = bias
                for k in range(CONV_WIDTH):
                    lo = r0 + SUBLANES * (k // SUBLANES)
                    wk = jnp.concatenate([taps[k]] * reps, axis=0) if reps > 1 else taps[k]
                    acc = acc + ush_ref[buf, k % SUBLANES, lo:lo + rc, :] * wk
                conv_ref[lb, r0:r0 + rc, :] = acc

        acc = jnp.concatenate([conv_ref[lb] for lb in range(n_lb)], axis=1)
        mu = jnp.mean(acc, axis=-1, keepdims=True)
        cen = acc - mu
        var = jnp.mean(cen * cen, axis=-1, keepdims=True)
        c = _silu(cen * lax.rsqrt(var + EPS) * lng_ref[...] + lnb_ref[...]).astype(BF16)
        y = jnp.dot(c, w2_ref[...], preferred_element_type=F32) + b2_ref[...]

        @pl.when(i >= 1)
        def _():
            o_ref[...] = xp_ref[...] + y

        @pl.when((i % n_t) == n_t - 1)
        def _():
            for lb in range(n_lb):
                st_ref[0, :, lb * LANES:(lb + 1) * LANES] = ua_ref[lb, ts:ts + CONV_HIST, :]

    @pl.when(i % 2 == 0)
    def _():
        step(u0_ref, u1_ref)

    @pl.when(i % 2 == 1)
    def _():
        step(u1_ref, u0_ref)


def _conv_module_pipelined(x, g, w1, b1, wdw, bdw, lng, lnb, w2, b2):
    b, t, d = x.shape
    c_in = w2.shape[0]
    ts = TOKEN_TILE
    n_t = t // ts
    n_tiles = b * n_t
    n_lb = c_in // LANES
    row = lambda v: v.reshape(1, -1)
    cols = lambda v, lb: v[..., lb * LANES:(lb + 1) * LANES]
    w1_blocks = jnp.stack([jnp.concatenate([cols(w1, lb), cols(w1, n_lb + lb)], axis=-1) for lb in range(n_lb)])
    b1_blocks = jnp.stack([jnp.concatenate([cols(row(b1), lb), cols(row(b1), n_lb + lb)], axis=-1)
                           for lb in range(n_lb)])
    wdw_rep = jnp.broadcast_to(wdw[:, None, :], (CONV_WIDTH, SUBLANES, c_in))
    last = n_tiles - 1
    out, st = pl.pallas_call(
        functools.partial(_conv_pipe_kernel, ts=ts, n_t=n_t, rc=CONV_ROWS // 2),
        grid=(n_tiles + 1,),
        in_specs=[pl.BlockSpec((ts, d), lambda i: (jnp.minimum(i, last), 0)),
                  pl.BlockSpec((ts, d), lambda i: (jnp.maximum(i - 1, 0), 0)),
                  _const_spec((1, d)), _const_spec((n_lb, d, 2 * LANES)), _const_spec((n_lb, 1, 2 * LANES)),
                  _const_spec((CONV_WIDTH, SUBLANES, c_in)), _const_spec((1, c_in)), _const_spec((1, c_in)),
                  _const_spec((1, c_in)), _const_spec((c_in, d)), _const_spec((1, d))],
        out_specs=[pl.BlockSpec((ts, d), lambda i: (jnp.maximum(i - 1, 0), 0)),
                   pl.BlockSpec((1, CONV_HIST, c_in), lambda i: (jnp.minimum(i, last) // n_t, 0, 0))],
        out_shape=[jax.ShapeDtypeStruct((b * t, d), F32), jax.ShapeDtypeStruct((b, CONV_HIST, c_in), F32)],
        scratch_shapes=[pltpu.VMEM((ts, d), BF16),
                        pltpu.VMEM((n_lb, ts + CONV_HIST, LANES), F32),
                        pltpu.VMEM((n_lb, ts + CONV_HIST, LANES), F32),
                        pltpu.VMEM((2, SUBLANES, ts + 24, LANES), F32),
                        pltpu.VMEM((n_lb, ts, LANES), F32)],
        compiler_params=_params(1),
        name="conv_pipelined",
    )(x.reshape(b * t, d), x.reshape(b * t, d), row(g), w1_blocks, b1_blocks, wdw_rep, row(bdw), row(lng), row(lnb),
      w2, row(b2))
    return out.reshape(b, t, d), st[:, CONV_HIST - (CONV_WIDTH - 1):]


def _conv_glu_kernel(x_ref, g_ref, w1_ref, b1_ref, uext_ref, st_ref, tail_ref, *, ts, n_t):
    i = pl.program_id(0)
    c_in = w1_ref.shape[1] // 2

    @pl.when(i == 0)
    def _():
        tail_ref[...] = jnp.zeros(tail_ref.shape, F32)

    h = _rms(x_ref[...], g_ref[...]).astype(BF16)
    ag = jnp.dot(h, w1_ref[...], preferred_element_type=F32) + b1_ref[...]
    u = ag[:, :c_in] * jax.nn.sigmoid(ag[:, c_in:])
    starts_sequence = (i % n_t) == 0
    for lb in range(uext_ref.shape[0]):
        lanes = slice(lb * LANES, (lb + 1) * LANES)
        uext_ref[lb, 0:CONV_HIST, :] = jnp.where(starts_sequence, 0.0, tail_ref[lb])
        uext_ref[lb, CONV_HIST:CONV_HIST + ts, :] = u[:, lanes]
        tail_ref[lb] = u[ts - CONV_HIST:, lanes]
    st_ref[0] = u[ts - CONV_HIST:, :]


def _conv_glu(x, g, w1, b1, n_t):
    n, d = x.shape
    c_in = w1.shape[1] // 2
    ts = TOKEN_TILE
    n_tiles = n // ts
    n_lb = c_in // LANES
    return pl.pallas_call(
        functools.partial(_conv_glu_kernel, ts=ts, n_t=n_t),
        grid=(n_tiles,),
        in_specs=[pl.BlockSpec((ts, d), lambda i: (i, 0)), _const_spec((1, d)), _const_spec((d, 2 * c_in)),
                  _const_spec((1, 2 * c_in))],
        out_specs=[pl.BlockSpec((n_lb, ts + CONV_HIST, LANES), lambda i: (0, i, 0)),
                   pl.BlockSpec((1, CONV_HIST, c_in), lambda i: (i // n_t, 0, 0))],
        out_shape=[jax.ShapeDtypeStruct((n_lb, n_tiles * (ts + CONV_HIST), LANES), F32),
                   jax.ShapeDtypeStruct((n_tiles // n_t, CONV_HIST, c_in), F32)],
        scratch_shapes=[pltpu.VMEM((n_lb, CONV_HIST, LANES), F32)],
        compiler_params=_params(1),
        name="conv_glu",
    )(x, g.reshape(1, d), w1, b1.reshape(1, -1))


def _conv_ffn_kernel(uext_ref, x1_ref, wdw_ref, bdw_ref, lng_ref, lnb_ref, w2_ref, b2_ref, g_ref, win_ref, wout_ref,
                     o_ref, x2_ref, ush_ref, conv_ref, act_ref, *, d_ff, n_chunk, ts, rc):
    i = pl.program_id(0)
    n_lb = conv_ref.shape[0]
    lead = CONV_HIST - (CONV_WIDTH - 1)
    reps = rc // SUBLANES

    @pl.when(i == 0)
    def _():
        x2_ref[...] = jnp.zeros(x2_ref.shape, F32)

    x = x2_ref[(i + 1) % 2]
    h = _rms(x, g_ref[...]).astype(BF16)
    for c in range(d_ff // n_chunk):
        lo = c * n_chunk
        gate = jnp.dot(h, win_ref[:, lo:lo + n_chunk], preferred_element_type=F32)
        up = jnp.dot(h, win_ref[:, d_ff + lo:d_ff + lo + n_chunk], preferred_element_type=F32)
        act_ref[:, lo:lo + n_chunk] = (_silu(gate) * up).astype(BF16)
    y = jnp.dot(act_ref[...], wout_ref[...], preferred_element_type=F32)
    o_ref[...] = x + 0.5 * y

    for lb in range(n_lb):
        lanes = slice(lb * LANES, (lb + 1) * LANES)
        buf = lb % 2
        for s in range(SUBLANES):
            n_sh = ts + SUBLANES * ((CONV_WIDTH - 1 - s) // SUBLANES)
            ush_ref[buf, s, 0:n_sh, :] = uext_ref[lb, lead + s:lead + s + n_sh, :]
        taps = [wdw_ref[k, :, lanes] for k in range(CONV_WIDTH)]
        bias = jnp.broadcast_to(bdw_ref[:, lanes], (rc, LANES))
        for ci in range(ts // rc):
            r0 = ci * rc
            acc = bias
            for k in range(CONV_WIDTH):
                lo = r0 + SUBLANES * (k // SUBLANES)
                wk = jnp.concatenate([taps[k]] * reps, axis=0) if reps > 1 else taps[k]
                acc = acc + ush_ref[buf, k % SUBLANES, lo:lo + rc, :] * wk
            conv_ref[lb, r0:r0 + rc, :] = acc
    acc = jnp.concatenate([conv_ref[lb] for lb in range(n_lb)], axis=1)
    mu = jnp.mean(acc, axis=-1, keepdims=True)
    cen = acc - mu
    var = jnp.mean(cen * cen, axis=-1, keepdims=True)
    c = _silu(cen * lax.rsqrt(var + EPS) * lng_ref[...] + lnb_ref[...]).astype(BF16)
    x2_ref[i % 2] = x1_ref[...] + jnp.dot(c, w2_ref[...], preferred_element_type=F32) + b2_ref[...]


def _conv_ffn(uext, x1, wdw, bdw, lng, lnb, w2, b2, g, w_in, w_out, idx):
    n, d = x1.shape
    c_in = w2.shape[0]
    d_ff = w_out.shape[-2]
    ts = TOKEN_TILE
    n_tiles = n // ts
    n_lb = c_in // LANES
    last = n_tiles - 1
    row = lambda v: v.reshape(1, -1)
    stacked = lambda rows, cols: pl.BlockSpec((None, None, rows, cols), lambda i: (*idx, 0, 0),
                                              pipeline_mode=pl.Buffered(1))
    wdw_rep = jnp.broadcast_to(wdw[:, None, :], (CONV_WIDTH, SUBLANES, c_in))
    return pl.pallas_call(
        functools.partial(_conv_ffn_kernel, d_ff=d_ff, n_chunk=HALF, ts=ts, rc=CONV_ROWS // 2),
        grid=(n_tiles + 1,),
        in_specs=[pl.BlockSpec((n_lb, ts + CONV_HIST, LANES), lambda i: (0, jnp.minimum(i, last), 0)),
                  pl.BlockSpec((ts, d), lambda i: (jnp.minimum(i, last), 0)),
                  _const_spec((CONV_WIDTH, SUBLANES, c_in)), _const_spec((1, c_in)), _const_spec((1, c_in)),
                  _const_spec((1, c_in)), _const_spec((c_in, d)), _const_spec((1, d)),
                  _const_spec((1, d)), stacked(d, 2 * d_ff), stacked(d_ff, d)],
        out_specs=pl.BlockSpec((ts, d), lambda i: (jnp.maximum(i - 1, 0), 0)),
        out_shape=jax.ShapeDtypeStruct((n, d), F32),
        scratch_shapes=[pltpu.VMEM((2, ts, d), F32),
                        pltpu.VMEM((2, SUBLANES, ts + 24, LANES), F32),
                        pltpu.VMEM((n_lb, ts, LANES), F32),
                        pltpu.VMEM((ts, d_ff), BF16)],
        compiler_params=_params(1),
        name="conv_ffn",
    )(uext, x1, wdw_rep, row(bdw), row(lng), row(lnb), w2, row(b2), row(g), w_in, w_out)


def _t5_bucket(dist):
    max_exact = N_BUCKETS // 2
    d_f = jnp.maximum(dist, 1).astype(F32)
    large = max_exact + (jnp.log(d_f / max_exact) / math.log(MAX_DISTANCE / max_exact)
                         * (N_BUCKETS - max_exact)).astype(jnp.int32)
    large = jnp.minimum(large, N_BUCKETS - 1)
    return jnp.where(dist < max_exact, dist, large)


def _group_bias(rel_bias, g):
    _, d = GROUPS[g]
    dist = d * jnp.arange(N_KEYS, dtype=jnp.int32)
    b = jnp.take(rel_bias, _t5_bucket(dist), axis=0)
    return b[:, g * HEADS_PER_GROUP:(g + 1) * HEADS_PER_GROUP].T.astype(F32)


def _toeplitz(p, n_rows, n_cols):
    length = p.shape[1]
    stride = length - 1
    flat = jnp.tile(p, (1, n_rows))[:, :n_rows * stride]
    return flat.reshape(p.shape[0], n_rows, stride)[:, :, :n_cols]


def _stack_rows(t):
    return t.reshape(2, HEADS_PER_HALF * t.shape[1], t.shape[2])


def _prompt_bias_tiles(bias_g):
    neg = jnp.full((bias_g.shape[0], 2 * QBLOCK + 1 - N_KEYS), NEG_INF, F32)
    p = jnp.concatenate([bias_g[:, ::-1], neg], axis=1)
    regular = _toeplitz(p, QBLOCK, 2 * QBLOCK)
    c = jnp.arange(2 * QBLOCK, dtype=jnp.int32)[None, None, :]
    first = jnp.where(c >= QBLOCK, regular, NEG_INF)
    return jnp.stack([_stack_rows(regular), _stack_rows(first)], axis=0)


def _sample_bias_tiles(bias_g, g, t_new):
    w, d = GROUPS[g]
    n_heads = bias_g.shape[0]
    vals = bias_g[:, N_KEYS - 1:0:-1]
    strided = jnp.concatenate([vals[:, :, None], jnp.full((n_heads, N_KEYS - 1, d - 1), NEG_INF, F32)], axis=2)
    p = jnp.concatenate([strided.reshape(n_heads, w), jnp.full((n_heads, t_new), NEG_INF, F32)], axis=1)
    cached = _toeplitz(p, t_new, w)
    row = jnp.arange(w, dtype=jnp.int32)[None, None, :]
    min_valid = 2 * w - PAST_LEN
    cached = jnp.where(w + row >= min_valid, cached, NEG_INF)
    t = jnp.arange(t_new, dtype=jnp.int32)[None, :, None]
    i = jnp.arange(LANES, dtype=jnp.int32)[None, None, :] - (LANES - t_new)
    dist = t - i
    new = jnp.full((n_heads, t_new, LANES), NEG_INF, F32)
    for j in range((t_new - 1) // d + 1):
        new = jnp.where((i >= 0) & (dist == d * j), bias_g[:, j][:, None, None], new)
    return _stack_rows(cached), _stack_rows(new)


def _stack_heads(x, lane_head):
    return jnp.concatenate([jnp.where(lane_head == h, x, jnp.zeros_like(x)) for h in range(HEADS_PER_HALF)], axis=0)


def _unstack_heads(x, lane_head, q):
    out = jnp.where(lane_head == 0, x[0:q], 0.0)
    for h in range(1, HEADS_PER_HALF):
        out = out + jnp.where(lane_head == h, x[h * q:(h + 1) * q], 0.0)
    return out


def _residue_rows(start, d):
    return pl.ds(start, QBLOCK, stride=d) if d > 1 else pl.ds(start, QBLOCK)


def _gather_rows(ref, start, d):
    return jnp.concatenate([ref[sl, _residue_rows(start, d), :] for sl in range(ref.shape[0])], axis=1)


def _attn_prompt_kernel(q_ref, kp_ref, kc_ref, vp_ref, vc_ref, bias_ref, o_ref, lse_ref, *, d, n_sub):
    qb = QBLOCK
    j = pl.program_id(1)
    half = pl.program_id(2)
    lane_head = lax.broadcasted_iota(jnp.int32, (qb, HALF), 1) // HEAD_DIM
    lane = lax.broadcasted_iota(jnp.int32, (qb, LANES), 1)
    first = jnp.where(j == 0, 1, 0)
    for r in range(d):
        k_prev = _gather_rows(kp_ref, r, d).astype(BF16)
        v_prev = _gather_rows(vp_ref, r, d).astype(BF16)
        for sub in range(n_sub):
            start = r + d * sub * qb
            q = _gather_rows(q_ref, start, d).astype(BF16)
            k_cur = _gather_rows(kc_ref, start, d).astype(BF16)
            v_cur = _gather_rows(vc_ref, start, d).astype(BF16)
            k = jnp.concatenate([k_prev, k_cur], axis=0)
            v = jnp.concatenate([v_prev, v_cur], axis=0)
            qs = _stack_heads(q, lane_head)
            s = lax.dot_general(qs, k, (((1,), (1,)), ((), ())), preferred_element_type=F32)
            s = s + (bias_ref[first, half] if sub == 0 else bias_ref[0, half])
            m = jnp.max(s, axis=-1, keepdims=True)
            p = jnp.exp(s - m)
            den = jnp.sum(p, axis=-1, keepdims=True)
            ov = jnp.dot((p / den).astype(BF16), v, preferred_element_type=F32)
            o = _unstack_heads(ov, lane_head, qb)
            rows = _residue_rows(start, d)
            for sl in range(o_ref.shape[0]):
                o_ref[sl, rows, :] = o[:, sl * LANES:(sl + 1) * LANES]
            lse = m + jnp.log(den)
            lse_tile = jnp.zeros((qb, LANES), F32)
            for h in range(HEADS_PER_HALF):
                lse_tile = jnp.where(lane == half * HEADS_PER_HALF + h, lse[h * qb:(h + 1) * qb], lse_tile)
            lse_ref[0, rows, :] = lse_tile
            k_prev, v_prev = k_cur, v_cur


def _attn_prompt(q_sl, kv_sl, bias_tiles, g, b, s):
    w, d = GROUPS[g]
    tb = d * ATTN_QUERIES[g]
    pb = d * QBLOCK
    n_tb = s // tb
    sph = HALF // LANES
    n_half = ATTN_WIDTH // HALF
    cur = lambda i, j: i * n_tb + j
    prv = lambda i, j: jnp.maximum(i * (s // pb) + j * (tb // pb) - 1, 0)
    return pl.pallas_call(
        functools.partial(_attn_prompt_kernel, d=d, n_sub=ATTN_QUERIES[g] // QBLOCK),
        grid=(b, n_tb, n_half),
        in_specs=[
            pl.BlockSpec((sph, tb, LANES), lambda i, j, hf: (g * n_half + hf, cur(i, j), 0)),
            pl.BlockSpec((sph, pb, LANES), lambda i, j, hf: (g * 2 * n_half + hf, prv(i, j), 0)),
            pl.BlockSpec((sph, tb, LANES), lambda i, j, hf: (g * 2 * n_half + hf, cur(i, j), 0)),
            pl.BlockSpec((sph, pb, LANES), lambda i, j, hf: (g * 2 * n_half + n_half + hf, prv(i, j), 0)),
            pl.BlockSpec((sph, tb, LANES), lambda i, j, hf: (g * 2 * n_half + n_half + hf, cur(i, j), 0)),
            _const_spec(bias_tiles.shape),
        ],
        out_specs=[pl.BlockSpec((sph, tb, LANES), lambda i, j, hf: (hf, cur(i, j), 0)),
                   pl.BlockSpec((1, tb, LANES), lambda i, j, hf: (hf, cur(i, j), 0))],
        out_shape=[jax.ShapeDtypeStruct((n_half * sph, b * s, LANES), F32),
                   jax.ShapeDtypeStruct((n_half, b * s, LANES), F32)],
        compiler_params=_params(3),
        name=f"attn_prompt_g{g}",
    )(q_sl, kv_sl, kv_sl, kv_sl, kv_sl, bias_tiles)


def _expand_heads(w, e_ref):
    hi = w.astype(BF16)
    lo = (w - hi.astype(F32)).astype(BF16)
    return jnp.dot(jnp.concatenate([hi, lo], axis=1), e_ref[...], preferred_element_type=F32)


def _merged_attention(o_refs, l_refs, e_ref, wo_ref):
    ls = [l_ref[0] + l_ref[1] for l_ref in l_refs]
    mx = jnp.maximum(jnp.maximum(ls[0], ls[1]), ls[2])
    es = [jnp.exp(l - mx) for l in ls]
    tot = es[0] + es[1] + es[2]
    merged = None
    for e, o_ref in zip(es, o_refs):
        o = jnp.concatenate([o_ref[sl] for sl in range(o_ref.shape[0])], axis=1)
        term = _expand_heads(e / tot, e_ref) * o
        merged = term if merged is None else merged + term
    return jnp.dot(merged.astype(BF16), wo_ref[...], preferred_element_type=F32)


def _head_expander():
    head = jnp.arange(2 * LANES, dtype=jnp.int32)[:, None] % LANES
    lane = jnp.arange(ATTN_WIDTH, dtype=jnp.int32)[None, :]
    return (lane // HEAD_DIM == head).astype(BF16)


N_SAMPLE_ATTN_INPUTS = 6 * N_GROUPS
N_SAMPLE_ATTN_OUTPUTS = 1 + N_GROUPS


def _attn_sample_unit(in_refs, out_refs, half):
    (q0_ref, q1_ref, q2_ref, kn0_ref, kn1_ref, kn2_ref, vn0_ref, vn1_ref, vn2_ref, c0_ref, c1_ref, c2_ref,
     bc0_ref, bc1_ref, bc2_ref, bn0_ref, bn1_ref, bn2_ref) = in_refs
    o_ref, n0_ref, n1_ref, n2_ref = out_refs
    t = q0_ref.shape[1]
    lane_head = lax.broadcasted_iota(jnp.int32, (t, HALF), 1) // HEAD_DIM
    is_new = lax.broadcasted_iota(jnp.int32, (HALF, LANES), 1) >= LANES - t
    zpad = jnp.zeros((LANES - t, HALF), F32)
    outs, lses = [], []
    for q_ref, kn_ref, vn_ref, c_ref, bc_ref, bn_ref, n_ref in (
            (q0_ref, kn0_ref, vn0_ref, c0_ref, bc0_ref, bn0_ref, n0_ref),
            (q1_ref, kn1_ref, vn1_ref, c1_ref, bc1_ref, bn1_ref, n1_ref),
            (q2_ref, kn2_ref, vn2_ref, c2_ref, bc2_ref, bn2_ref, n2_ref)):
        w = c_ref.shape[3]
        qs = _stack_heads(q_ref[0], lane_head)
        k_t = c_ref[0, 0]
        v_t = c_ref[0, 1]
        k_new = jnp.concatenate([zpad, kn_ref[0]], axis=0)
        v_new = jnp.concatenate([zpad, vn_ref[0]], axis=0)
        k_new_t = k_new.T
        v_new_t = v_new.T
        s_c = jnp.dot(qs, k_t.astype(BF16), preferred_element_type=F32) + bc_ref[half]
        s_n = jnp.dot(qs, k_new_t.astype(BF16), preferred_element_type=F32) + bn_ref[half]
        m = jnp.maximum(s_c.max(axis=-1, keepdims=True), s_n.max(axis=-1, keepdims=True))
        p_c = jnp.exp(s_c - m)
        p_n = jnp.exp(s_n - m)
        den = p_c.sum(axis=-1, keepdims=True) + p_n.sum(axis=-1, keepdims=True)
        ov = lax.dot_general((p_c / den).astype(BF16), v_t.astype(BF16), (((1,), (1,)), ((), ())),
                             preferred_element_type=F32)
        ov = ov + jnp.dot((p_n / den).astype(BF16), v_new.astype(BF16), preferred_element_type=F32)
        outs.append(ov)
        lses.append(m + jnp.log(den))
        for kv, (old_t, new_t) in enumerate(((k_t, k_new_t), (v_t, v_new_t))):
            rolled = pltpu.roll(old_t, w - t, axis=1)
            if w > LANES:
                n_ref[0, kv, :, 0:w - LANES] = rolled[:, 0:w - LANES]
            n_ref[0, kv, :, w - LANES:w] = jnp.where(is_new, new_t, rolled[:, w - LANES:w])
    mx = jnp.maximum(jnp.maximum(lses[0], lses[1]), lses[2])
    es = [jnp.exp(l - mx) for l in lses]
    tot = es[0] + es[1] + es[2]
    merged = (es[0] / tot) * outs[0] + (es[1] / tot) * outs[1] + (es[2] / tot) * outs[2]
    o_ref[0] = _unstack_heads(merged, lane_head, t).astype(o_ref.dtype)


def _attn_sample_operands(q, kv_new, caches_t, bias_gs):
    b, t, _ = q.shape
    n_half = ATTN_WIDTH // HALF
    tok = lambda col: pl.BlockSpec((1, t, HALF), lambda i, col=col: (i // n_half, 0, col + i % n_half))
    in_specs = [tok(g * n_half) for g in range(N_GROUPS)]
    in_specs += [tok(g * 2 * n_half) for g in range(N_GROUPS)]
    in_specs += [tok(g * 2 * n_half + n_half) for g in range(N_GROUPS)]
    args = [q] * N_GROUPS + [kv_new] * (2 * N_GROUPS)
    cache_spec = lambda g: pl.BlockSpec((1, 2, HALF, GROUPS[g][0]), lambda i: (i // n_half, 0, i % n_half, 0))
    in_specs += [cache_spec(g) for g in range(N_GROUPS)]
    args += list(caches_t)
    tiles = [_sample_bias_tiles(bias_gs[g], g, t) for g in range(N_GROUPS)]
    for part in range(2):
        for g in range(N_GROUPS):
            in_specs.append(_const_spec(tiles[g][part].shape))
            args.append(tiles[g][part])
    out_specs = [pl.BlockSpec((1, t, HALF), lambda i: (i // n_half, 0, i % n_half))]
    out_specs += [cache_spec(g) for g in range(N_GROUPS)]
    out_shape = [jax.ShapeDtypeStruct((b, t, ATTN_WIDTH), BF16)] + [jax.ShapeDtypeStruct(c.shape, F32) for c in caches_t]
    return in_specs, args, out_specs, out_shape


def _out_proj_kernel(x_ref, a_ref, wo_ref, out_ref):
    out_ref[...] = x_ref[...] + jnp.dot(a_ref[...], wo_ref[...], preferred_element_type=F32)


def _out_proj(x, a, w_o):
    n, d = x.shape
    return pl.pallas_call(
        _out_proj_kernel,
        out_shape=jax.ShapeDtypeStruct((n, d), F32),
        compiler_params=pltpu.CompilerParams(vmem_limit_bytes=VMEM_LIMIT),
        name="out_proj",
    )(x, a, w_o)


def _cache_view(c_t, b):
    w = c_t.shape[-1]
    return jnp.transpose(c_t.reshape(b, 2, HEADS_PER_GROUP, HEAD_DIM, w), (0, 4, 1, 2, 3))


def _conv_args(wts):
    return (wts["norms"][0, 1], wts["conv_w_pw1"][0], wts["conv_b_pw1"][0], wts["conv_w_dw"][0], wts["conv_b_dw"][0],
            wts["conv_ln_g"][0], wts["conv_ln_b"][0], wts["conv_w_pw2"][0], wts["conv_b_pw2"][0])


def _sample_until_attention(x, conv_prev, kv_bufs, wts):
    b, t, d = x.shape
    norms, w_in, w_out = wts["norms"], wts["ffn_w_in"], wts["ffn_w_out"]
    xf = _ffn(x.reshape(b * t, d), norms[0, 0], w_in, w_out, (0, 0))
    x3, conv_state = _conv_module(xf.reshape(b, t, d), conv_prev, *_conv_args(wts), SUBLANES, t)
    xf = _ffn(x3.reshape(b * t, d), norms[0, 2], w_in, w_out, (0, 1))
    kv_new = _proj(xf, wts["norm_kv"], wts["w_kv"], F32).reshape(b, t, -1)
    xf = _ffn(xf, norms[1, 0], w_in, w_out, (1, 0))
    q = _proj(xf, norms[1, 1], wts["attn_w_q"][0], BF16, scale=HEAD_DIM ** -0.5).reshape(b, t, -1)
    caches_t = [jnp.transpose(buf, (0, 2, 3, 4, 1)).reshape(b, 2, ATTN_WIDTH, buf.shape[1]) for buf in kv_bufs]
    return xf, conv_state, (q, kv_new, caches_t, wts["bias_gs"])


def _sample_after_attention(xf, attn, wts):
    xf = _out_proj(xf, attn.reshape(xf.shape[0], ATTN_WIDTH), wts["attn_w_o"][0])
    return _ffn(xf, wts["norms"][1, 2], wts["ffn_w_in"], wts["ffn_w_out"], (1, 1), g_final=wts["norm_final"])


def _prompt_trunk(x, wts, sample_attn):
    b, t, d = x.shape
    norms, w_in, w_out = wts["norms"], wts["ffn_w_in"], wts["ffn_w_out"]
    xf, *sample_res = _ffn(x.reshape(b * t, d), norms[0, 0], w_in, w_out, (0, 0), sample_attn=sample_attn)
    x3, conv_state = _conv_module_pipelined(xf.reshape(b, t, d), *_conv_args(wts))
    xf = _ffn(x3.reshape(b * t, d), norms[0, 2], w_in, w_out, (0, 1))
    shared, *kv_t = _kv_prompt(xf.reshape(b, t, d), wts["norm_kv"], wts["w_kv"])
    q_proj = (norms[1, 1], wts["attn_w_q"][0], HEAD_DIM ** -0.5)
    xf, q_sl = _ffn(xf, norms[1, 0], w_in, w_out, (1, 0), q_proj=q_proj)
    outs, lses = [], []
    for g in range(N_GROUPS):
        o, lse = _attn_prompt(q_sl, shared, _prompt_bias_tiles(wts["bias_gs"][g]), g, b, t)
        outs.append(o)
        lses.append(lse)
    xf = _ffn(xf, norms[1, 2], w_in, w_out, (1, 1), merge=(outs, lses, wts["attn_w_o"][0]),
              g_final=wts["norm_final"])
    return xf.reshape(b, t, d), conv_state, [_cache_view(c, b) for c in kv_t], sample_res


def kernel(x_prompt, x_sample, state_conv, cache_kv_w128, cache_kv_w512, cache_kv_w2048, norms, ffn_w_in, ffn_w_out,
           conv_w_pw1, conv_b_pw1, conv_w_dw, conv_b_dw, conv_ln_g, conv_ln_b, conv_w_pw2, conv_b_pw2, norm_kv, w_kv,
           attn_w_q, attn_w_o, rel_bias, norm_final):
    wts = dict(
        norms=norms, ffn_w_in=ffn_w_in.astype(BF16), ffn_w_out=ffn_w_out.astype(BF16),
        conv_w_pw1=conv_w_pw1.astype(BF16), conv_b_pw1=conv_b_pw1, conv_w_dw=conv_w_dw, conv_b_dw=conv_b_dw,
        conv_ln_g=conv_ln_g, conv_ln_b=conv_ln_b, conv_w_pw2=conv_w_pw2.astype(BF16), conv_b_pw2=conv_b_pw2,
        norm_kv=norm_kv, w_kv=w_kv.astype(BF16), attn_w_q=attn_w_q.astype(BF16), attn_w_o=attn_w_o.astype(BF16),
        rel_bias=rel_bias, norm_final=norm_final)
    wts["bias_gs"] = [_group_bias(rel_bias, g) for g in range(N_GROUPS)]
    bs, ts, d = x_sample.shape
    xs, conv_sample, sample_attn = _sample_until_attention(
        x_sample, state_conv[0], (cache_kv_w128, cache_kv_w512, cache_kv_w2048), wts)
    y_prompt, conv_prompt, kv_prompt, (attn_s, *caches_new) = _prompt_trunk(x_prompt, wts, sample_attn)
    y_sample = _sample_after_attention(xs, attn_s, wts).reshape(bs, ts, d)
    kv_sample = [_cache_view(c, bs) for c in caches_new]
    return (y_prompt, y_sample, conv_prompt[None], conv_sample[None], kv_prompt[0], kv_sample[0],
            kv_prompt[1], kv_sample[1], kv_prompt[2], kv_sample[2])
```

```python
import functools
import math

import jax
import jax.numpy as jnp
from jax import lax
from jax.experimental import pallas as pl
from jax.experimental.pallas import tpu as pltpu

F32 = jnp.float32
BF16 = jnp.bfloat16

EPS = 1e-6
NEG_INF = -1e30
GROUPS = ((128, 1), (512, 4), (2048, 16))
N_GROUPS = len(GROUPS)
HEADS_PER_GROUP = 8
HEAD_DIM = 64
ATTN_WIDTH = HEADS_PER_GROUP * HEAD_DIM
ROW_WIDTH = 2 * ATTN_WIDTH
N_KEYS = 129
N_BUCKETS = 32
MAX_DISTANCE = 2048
PAST_LEN = 8192
CONV_WIDTH = 31
CONV_HIST = 32
LANES = 128
SUBLANES = 8
HALF = 256
HEADS_PER_HALF = HALF // HEAD_DIM
QBLOCK = 128
ATTN_QUERIES = (4096, 1024, 256)
TOKEN_TILE = 512
CONV_ROWS = 64
VMEM_LIMIT = 56 * 1024 * 1024


def _params(n_axes):
    return pltpu.CompilerParams(dimension_semantics=("arbitrary",) * n_axes, vmem_limit_bytes=VMEM_LIMIT)


def _const_spec(shape):
    zeros = (0,) * len(shape)
    return pl.BlockSpec(shape, lambda *_: zeros, pipeline_mode=pl.Buffered(1))


def _rms(x, g):
    return x * lax.rsqrt(jnp.mean(x * x, axis=-1, keepdims=True) + EPS) * g


def _silu(x):
    return x * jax.nn.sigmoid(x)


def _ffn_kernel(*refs, d_ff, n_chunk, has_merge, has_final, q_scale, has_sample_attn):
    refs = list(refs)
    x_ref = refs.pop(0)
    if has_merge:
        attn_refs = [refs.pop(0) for _ in range(2 * N_GROUPS)]
        e_ref, wo_ref = refs.pop(0), refs.pop(0)
    g_ref, win_ref, wout_ref = refs.pop(0), refs.pop(0), refs.pop(0)
    if has_final:
        gf_ref = refs.pop(0)
    if q_scale is not None:
        gq_ref, wq_ref = refs.pop(0), refs.pop(0)
    if has_sample_attn:
        sample_in = [refs.pop(0) for _ in range(N_SAMPLE_ATTN_INPUTS)]
    o_ref = refs.pop(0)
    if q_scale is not None:
        q_ref = refs.pop(0)
    if has_sample_attn:
        sample_out = [refs.pop(0) for _ in range(N_SAMPLE_ATTN_OUTPUTS)]
    act_ref = refs.pop(0)

    if has_sample_attn:
        _attn_sample_unit(sample_in, sample_out, pl.program_id(0) % (ATTN_WIDTH // HALF))

    x = x_ref[...]
    if has_merge:
        x = x + _merged_attention(attn_refs[:N_GROUPS], attn_refs[N_GROUPS:], e_ref, wo_ref)
    h = _rms(x, g_ref[...]).astype(BF16)
    for c in range(d_ff // n_chunk):
        lo = c * n_chunk
        gate = jnp.dot(h, win_ref[:, lo:lo + n_chunk], preferred_element_type=F32)
        up = jnp.dot(h, win_ref[:, d_ff + lo:d_ff + lo + n_chunk], preferred_element_type=F32)
        act_ref[:, lo:lo + n_chunk] = (_silu(gate) * up).astype(BF16)
    y = jnp.dot(act_ref[...], wout_ref[...], preferred_element_type=F32)
    o = x + 0.5 * y
    if q_scale is not None:
        hq = _rms(o, gq_ref[...]).astype(BF16)
        _store_slabs(q_ref, jnp.dot(hq, wq_ref[...], preferred_element_type=F32) * q_scale)
    if has_final:
        o = _rms(o, gf_ref[...])
    o_ref[...] = o


def _ffn(x, g, w_in, w_out, idx, *, merge=None, g_final=None, q_proj=None, sample_attn=None):
    n, d = x.shape
    d_ff = w_out.shape[-2]
    tm = min(TOKEN_TILE, n)
    if sample_attn is not None:
        n_units = sample_attn[0].shape[0] * (ATTN_WIDTH // HALF)
        tm = n // n_units
        assert tm * n_units == n and tm % SUBLANES == 0, (n, n_units)
    tok = lambda width: pl.BlockSpec((tm, width), lambda i: (i, 0))
    slabs = lambda count: pl.BlockSpec((count, tm, LANES), lambda i: (0, i, 0))
    stacked = lambda rows, cols: pl.BlockSpec((None, None, rows, cols), lambda i: (*idx, 0, 0),
                                              pipeline_mode=pl.Buffered(1))
    in_specs = [tok(d)]
    args = [x]
    if merge is not None:
        outs, lses, w_o = merge
        in_specs += [slabs(ATTN_WIDTH // LANES)] * N_GROUPS + [slabs(ATTN_WIDTH // HALF)] * N_GROUPS
        in_specs += [_const_spec((2 * LANES, ATTN_WIDTH)), _const_spec((ATTN_WIDTH, d))]
        args += [*outs, *lses, _head_expander(), w_o]
    in_specs += [_const_spec((1, d)), stacked(d, 2 * d_ff), stacked(d_ff, d)]
    args += [g.reshape(1, d), w_in, w_out]
    if g_final is not None:
        in_specs.append(_const_spec((1, d)))
        args.append(g_final.reshape(1, d))
    out_specs = [tok(d)]
    out_shape = [jax.ShapeDtypeStruct((n, d), F32)]
    q_scale = None
    if q_proj is not None:
        g_q, w_q, q_scale = q_proj
        n_q = w_q.shape[1]
        in_specs += [_const_spec((1, d)), _const_spec((d, n_q))]
        args += [g_q.reshape(1, d), w_q]
        out_specs.append(slabs(n_q // LANES))
        out_shape.append(jax.ShapeDtypeStruct((n_q // LANES, n, LANES), F32))
    if sample_attn is not None:
        s_in_specs, s_args, s_out_specs, s_out_shape = _attn_sample_operands(*sample_attn)
        in_specs += s_in_specs
        args += s_args
        out_specs += s_out_specs
        out_shape += s_out_shape
    res = pl.pallas_call(
        functools.partial(_ffn_kernel, d_ff=d_ff, n_chunk=HALF, has_merge=merge is not None,
                          has_final=g_final is not None, q_scale=q_scale, has_sample_attn=sample_attn is not None),
        grid=(n // tm,),
        in_specs=in_specs,
        out_specs=out_specs,
        out_shape=out_shape,
        scratch_shapes=[pltpu.VMEM((tm, d_ff), BF16)],
        compiler_params=_params(1),
        name="ffn",
    )(*args)
    return res[0] if len(res) == 1 else res


def _proj_kernel(x_ref, g_ref, w_ref, o_ref, *, scale):
    h = _rms(x_ref[...], g_ref[...]).astype(BF16)
    y = jnp.dot(h, w_ref[...], preferred_element_type=F32)
    if scale != 1.0:
        y = y * scale
    o_ref[...] = y.astype(o_ref.dtype)


def _proj(x, g, w, out_dtype, scale=1.0):
    n, d = x.shape
    n_out = w.shape[1]
    tm = min(TOKEN_TILE, n)
    return pl.pallas_call(
        functools.partial(_proj_kernel, scale=scale),
        grid=(n // tm,),
        in_specs=[pl.BlockSpec((tm, d), lambda i: (i, 0)), _const_spec((1, d)), _const_spec((d, n_out))],
        out_specs=pl.BlockSpec((tm, n_out), lambda i: (i, 0)),
        out_shape=jax.ShapeDtypeStruct((n, n_out), out_dtype),
        compiler_params=_params(1),
        name="proj",
    )(x, g.reshape(1, d), w)


def _store_slabs(slab_ref, y):
    for sl in range(slab_ref.shape[0]):
        slab_ref[sl] = y[:, sl * LANES:(sl + 1) * LANES]


def _kv_prompt_kernel(x_ref, g_ref, w_ref, sh_ref, t0_ref, t1_ref, t2_ref, *, n_t, tm):
    h = _rms(x_ref[0], g_ref[...]).astype(BF16)
    y = jnp.dot(h, w_ref[...], preferred_element_type=F32)
    _store_slabs(sh_ref, y)
    for g, t_ref in enumerate((t0_ref, t1_ref, t2_ref)):
        rows = min(GROUPS[g][0], n_t * tm, tm)
        t_ref[0] = y[tm - rows:, g * ROW_WIDTH:(g + 1) * ROW_WIDTH].T


def _kv_prompt(x, g, w):
    b, s, d = x.shape
    n_out = w.shape[1]
    tm = min(TOKEN_TILE, s)
    n_t = s // tm
    out_specs = [pl.BlockSpec((n_out // LANES, tm, LANES), lambda i, j: (0, i * n_t + j, 0))]
    out_shape = [jax.ShapeDtypeStruct((n_out // LANES, b * s, LANES), F32)]
    for grp in range(N_GROUPS):
        keep = min(GROUPS[grp][0], s)
        first = n_t - max(keep // tm, 1)
        out_specs.append(pl.BlockSpec((1, ROW_WIDTH, min(keep, tm)),
                                      lambda i, j, first=first: (i, 0, jnp.maximum(j - first, 0))))
        out_shape.append(jax.ShapeDtypeStruct((b, ROW_WIDTH, keep), F32))
    return pl.pallas_call(
        functools.partial(_kv_prompt_kernel, n_t=n_t, tm=tm),
        grid=(b, n_t),
        in_specs=[pl.BlockSpec((1, tm, d), lambda i, j: (i, j, 0)), _const_spec((1, d)), _const_spec((d, n_out))],
        out_specs=out_specs,
        out_shape=out_shape,
        compiler_params=_params(2),
        name="kv_prompt",
    )(x, g.reshape(1, d), w)


def _conv_kernel(x_ref, prev_ref, g_ref, w1_ref, b1_ref, wdw_ref, bdw_ref, lng_ref, lnb_ref, w2_ref, b2_ref,
                 o_ref, st_ref, uext_ref, ush_ref, conv_ref, *, bb, ts, n_t, rc):
    it = pl.program_id(1)
    d = x_ref.shape[-1]
    c_in = w2_ref.shape[0]
    x = x_ref[...].reshape(bb * ts, d)
    h = _rms(x, g_ref[...]).astype(BF16)
    ag = jnp.dot(h, w1_ref[...], preferred_element_type=F32) + b1_ref[...]
    u = ag[:, :c_in] * jax.nn.sigmoid(ag[:, c_in:])

    @pl.when(it == 0)
    def _():
        uext_ref[:, 0:CONV_HIST, :] = prev_ref[...]

    uext_ref[:, CONV_HIST:CONV_HIST + ts, :] = u.reshape(bb, ts, c_in)

    lead = CONV_HIST - (CONV_WIDTH - 1)
    n_lb = c_in // LANES
    for s in range(SUBLANES):
        n_sh = ts + SUBLANES * ((CONV_WIDTH - 1 - s) // SUBLANES)
        for lb in range(n_lb):
            ush_ref[s, lb, :, 0:n_sh, :] = uext_ref[:, lead + s:lead + s + n_sh, lb * LANES:(lb + 1) * LANES]

    reps = rc // SUBLANES
    for lb in range(n_lb):
        lanes = slice(lb * LANES, (lb + 1) * LANES)
        taps = [wdw_ref[k, :, lanes] for k in range(CONV_WIDTH)]
        bias = bdw_ref[:, lanes]

        def chunk(ci, carry, lb=lb, lanes=lanes, taps=taps, bias=bias):
            r0 = pl.multiple_of(ci * rc, rc)
            acc = jnp.broadcast_to(bias.reshape(1, 1, LANES), (bb, rc, LANES))
            for k in range(CONV_WIDTH):
                win = ush_ref[k % SUBLANES, lb, :, pl.ds(r0 + SUBLANES * (k // SUBLANES), rc), :]
                wk = jnp.concatenate([taps[k]] * reps, axis=0) if reps > 1 else taps[k]
                acc = acc + win * wk[None]
            conv_ref[:, pl.ds(r0, rc), lanes] = acc
            return carry

        lax.fori_loop(0, ts // rc, chunk, 0)

    acc = conv_ref[...].reshape(bb * ts, c_in)
    mu = jnp.mean(acc, axis=-1, keepdims=True)
    cen = acc - mu
    var = jnp.mean(cen * cen, axis=-1, keepdims=True)
    c = _silu(cen * lax.rsqrt(var + EPS) * lng_ref[...] + lnb_ref[...]).astype(BF16)
    y = jnp.dot(c, w2_ref[...], preferred_element_type=F32) + b2_ref[...]
    o_ref[...] = (x + y).reshape(bb, ts, d)

    @pl.when(it == n_t - 1)
    def _():
        st_ref[...] = uext_ref[:, ts:ts + CONV_HIST, :]

    if n_t > 1:
        uext_ref[:, 0:CONV_HIST, :] = uext_ref[:, ts:ts + CONV_HIST, :]


def _conv_module(x, prev, g, w1, b1, wdw, bdw, lng, lnb, w2, b2, bb, ts):
    b, t, d = x.shape
    c_in = w2.shape[0]
    n_t = t // ts
    rc = min(ts, CONV_ROWS)
    row = lambda v: v.reshape(1, -1)
    in_specs = [pl.BlockSpec((bb, ts, d), lambda i, j: (i, j, 0)),
                pl.BlockSpec((bb, CONV_HIST, c_in), lambda i, j: (i, 0, 0)),
                _const_spec((1, d)), _const_spec((d, 2 * c_in)), _const_spec((1, 2 * c_in)),
                _const_spec((CONV_WIDTH, SUBLANES, c_in)), _const_spec((1, c_in)), _const_spec((1, c_in)),
                _const_spec((1, c_in)), _const_spec((c_in, d)), _const_spec((1, d))]
    wdw_rep = jnp.broadcast_to(wdw[:, None, :], (CONV_WIDTH, SUBLANES, c_in))
    prev_rows = jnp.pad(prev, ((0, 0), (CONV_HIST - prev.shape[1], 0), (0, 0)))
    args = [x, prev_rows, row(g), w1, row(b1), wdw_rep, row(bdw), row(lng), row(lnb), w2, row(b2)]
    out, st = pl.pallas_call(
        functools.partial(_conv_kernel, bb=bb, ts=ts, n_t=n_t, rc=rc),
        grid=(b // bb, n_t),
        in_specs=in_specs,
        out_specs=[pl.BlockSpec((bb, ts, d), lambda i, j: (i, j, 0)),
                   pl.BlockSpec((bb, CONV_HIST, c_in), lambda i, j: (i, 0, 0))],
        out_shape=[jax.ShapeDtypeStruct((b, t, d), F32), jax.ShapeDtypeStruct((b, CONV_HIST, c_in), F32)],
        scratch_shapes=[pltpu.VMEM((bb, ts + CONV_HIST, c_in), F32),
                        pltpu.VMEM((SUBLANES, c_in // LANES, bb, ts + 24, LANES), F32),
                        pltpu.VMEM((bb, ts, c_in), F32)],
        compiler_params=_params(2),
        name="conv_module",
    )(*args)
    return out, st[:, CONV_HIST - (CONV_WIDTH - 1):]


def _conv_pipe_kernel(xc_ref, xp_ref, g_ref, w1_ref, b1_ref, wdw_ref, bdw_ref, lng_ref, lnb_ref, w2_ref, b2_ref,
                      o_ref, st_ref, h_ref, u0_ref, u1_ref, ush_ref, conv_ref, *, ts, n_t, rc):
    i = pl.program_id(0)
    n_lb = conv_ref.shape[0]
    lead = CONV_HIST - (CONV_WIDTH - 1)
    reps = rc // SUBLANES

    @pl.when(i == 0)
    def _():
        u1_ref[...] = jnp.zeros(u1_ref.shape, F32)

    def step(ua_ref, ub_ref):
        starts_sequence = (i % n_t) == 0
        h_ref[...] = _rms(xc_ref[...], g_ref[...]).astype(BF16)
        for lb in range(n_lb):
            lanes = slice(lb * LANES, (lb + 1) * LANES)
            ag = jnp.dot(h_ref[...], w1_ref[lb], preferred_element_type=F32) + b1_ref[lb]
            u = ag[:, :LANES] * jax.nn.sigmoid(ag[:, LANES:])
            ua_ref[lb, 0:CONV_HIST, :] = jnp.where(starts_sequence, 0.0, ub_ref[lb, ts:ts + CONV_HIST, :])
            ua_ref[lb, CONV_HIST:CONV_HIST + ts, :] = u
            buf = lb % 2
            for s in range(SUBLANES):
                n_sh = ts + SUBLANES * ((CONV_WIDTH - 1 - s) // SUBLANES)
                ush_ref[buf, s, 0:n_sh, :] = ub_ref[lb, lead + s:lead + s + n_sh, :]
            taps = [wdw_ref[k, :, lanes] for k in range(CONV_WIDTH)]
            bias = jnp.broadcast_to(bdw_ref[:, lanes], (rc, LANES))
            for ci in range(ts // rc):
                r0 = ci * rc
                acc = bias
                for k in range(CONV_WIDTH):
                    lo = r0 + SUBLANES * (k // SUBLANES)
                    wk = jnp.concatenate([taps[k]] * reps, axis=0) if reps > 1 else taps[k]
                    acc = acc + ush_ref[buf, k % SUBLANES, lo:lo + rc, :] * wk
                conv_ref[lb, r0:r0 + rc, :] = acc

        acc = jnp.concatenate([conv_ref[lb] for lb in range(n_lb)], axis=1)
        mu = jnp.mean(acc, axis=-1, keepdims=True)
        cen = acc - mu
        var = jnp.mean(cen * cen, axis=-1, keepdims=True)
        c = _silu(cen * lax.rsqrt(var + EPS) * lng_ref[...] + lnb_ref[...]).astype(BF16)
        y = jnp.dot(c, w2_ref[...], preferred_element_type=F32) + b2_ref[...]
        o_ref[...] = xp_ref[...] + y

        @pl.when((i % n_t) == n_t - 1)
        def _():
            for lb in range(n_lb):
                st_ref[0, :, lb * LANES:(lb + 1) * LANES] = ua_ref[lb, ts:ts + CONV_HIST, :]

    @pl.when(i % 2 == 0)
    def _():
        step(u0_ref, u1_ref)

    @pl.when(i % 2 == 1)
    def _():
        step(u1_ref, u0_ref)


def _conv_module_pipelined(x, g, w1, b1, wdw, bdw, lng, lnb, w2, b2):
    b, t, d = x.shape
    c_in = w2.shape[0]
    ts = TOKEN_TILE
    n_t = t // ts
    n_tiles = b * n_t
    n_lb = c_in // LANES
    row = lambda v: v.reshape(1, -1)
    cols = lambda v, lb: v[..., lb * LANES:(lb + 1) * LANES]
    w1_blocks = jnp.stack([jnp.concatenate([cols(w1, lb), cols(w1, n_lb + lb)], axis=-1) for lb in range(n_lb)])
    b1_blocks = jnp.stack([jnp.concatenate([cols(row(b1), lb), cols(row(b1), n_lb + lb)], axis=-1)
                           for lb in range(n_lb)])
    wdw_rep = jnp.broadcast_to(wdw[:, None, :], (CONV_WIDTH, SUBLANES, c_in))
    last = n_tiles - 1
    out, st = pl.pallas_call(
        functools.partial(_conv_pipe_kernel, ts=ts, n_t=n_t, rc=CONV_ROWS // 2),
        grid=(n_tiles + 1,),
        in_specs=[pl.BlockSpec((ts, d), lambda i: (jnp.minimum(i, last), 0)),
                  pl.BlockSpec((ts, d), lambda i: (jnp.maximum(i - 1, 0), 0)),
                  _const_spec((1, d)), _const_spec((n_lb, d, 2 * LANES)), _const_spec((n_lb, 1, 2 * LANES)),
                  _const_spec((CONV_WIDTH, SUBLANES, c_in)), _const_spec((1, c_in)), _const_spec((1, c_in)),
                  _const_spec((1, c_in)), _const_spec((c_in, d)), _const_spec((1, d))],
        out_specs=[pl.BlockSpec((ts, d), lambda i: (jnp.maximum(i - 1, 0), 0)),
                   pl.BlockSpec((1, CONV_HIST, c_in), lambda i: (jnp.minimum(i, last) // n_t, 0, 0))],
        out_shape=[jax.ShapeDtypeStruct((b * t, d), F32), jax.ShapeDtypeStruct((b, CONV_HIST, c_in), F32)],
        scratch_shapes=[pltpu.VMEM((ts, d), BF16),
                        pltpu.VMEM((n_lb, ts + CONV_HIST, LANES), F32),
                        pltpu.VMEM((n_lb, ts + CONV_HIST, LANES), F32),
                        pltpu.VMEM((2, SUBLANES, ts + 24, LANES), F32),
                        pltpu.VMEM((n_lb, ts, LANES), F32)],
        compiler_params=_params(1),
        name="conv_pipelined",
    )(x.reshape(b * t, d), x.reshape(b * t, d), row(g), w1_blocks, b1_blocks, wdw_rep, row(bdw), row(lng), row(lnb),
      w2, row(b2))
    return out.reshape(b, t, d), st[:, CONV_HIST - (CONV_WIDTH - 1):]


def _t5_bucket(dist):
    max_exact = N_BUCKETS // 2
    d_f = jnp.maximum(dist, 1).astype(F32)
    large = max_exact + (jnp.log(d_f / max_exact) / math.log(MAX_DISTANCE / max_exact)
                         * (N_BUCKETS - max_exact)).astype(jnp.int32)
    large = jnp.minimum(large, N_BUCKETS - 1)
    return jnp.where(dist < max_exact, dist, large)


def _group_bias(rel_bias, g):
    _, d = GROUPS[g]
    dist = d * jnp.arange(N_KEYS, dtype=jnp.int32)
    b = jnp.take(rel_bias, _t5_bucket(dist), axis=0)
    return b[:, g * HEADS_PER_GROUP:(g + 1) * HEADS_PER_GROUP].T.astype(F32)


def _toeplitz(p, n_rows, n_cols):
    length = p.shape[1]
    stride = length - 1
    flat = jnp.tile(p, (1, n_rows))[:, :n_rows * stride]
    return flat.reshape(p.shape[0], n_rows, stride)[:, :, :n_cols]


def _stack_rows(t):
    return t.reshape(2, HEADS_PER_HALF * t.shape[1], t.shape[2])


def _prompt_bias_tiles(bias_g):
    neg = jnp.full((bias_g.shape[0], 2 * QBLOCK + 1 - N_KEYS), NEG_INF, F32)
    p = jnp.concatenate([bias_g[:, ::-1], neg], axis=1)
    regular = _toeplitz(p, QBLOCK, 2 * QBLOCK)
    c = jnp.arange(2 * QBLOCK, dtype=jnp.int32)[None, None, :]
    first = jnp.where(c >= QBLOCK, regular, NEG_INF)
    return jnp.stack([_stack_rows(regular), _stack_rows(first)], axis=0)


def _sample_bias_tiles(bias_g, g, t_new):
    w, d = GROUPS[g]
    n_heads = bias_g.shape[0]
    vals = bias_g[:, N_KEYS - 1:0:-1]
    strided = jnp.concatenate([vals[:, :, None], jnp.full((n_heads, N_KEYS - 1, d - 1), NEG_INF, F32)], axis=2)
    p = jnp.concatenate([strided.reshape(n_heads, w), jnp.full((n_heads, t_new), NEG_INF, F32)], axis=1)
    cached = _toeplitz(p, t_new, w)
    row = jnp.arange(w, dtype=jnp.int32)[None, None, :]
    min_valid = 2 * w - PAST_LEN
    cached = jnp.where(w + row >= min_valid, cached, NEG_INF)
    t = jnp.arange(t_new, dtype=jnp.int32)[None, :, None]
    i = jnp.arange(LANES, dtype=jnp.int32)[None, None, :] - (LANES - t_new)
    dist = t - i
    new = jnp.full((n_heads, t_new, LANES), NEG_INF, F32)
    for j in range((t_new - 1) // d + 1):
        new = jnp.where((i >= 0) & (dist == d * j), bias_g[:, j][:, None, None], new)
    return _stack_rows(cached), _stack_rows(new)


def _stack_heads(x, lane_head):
    return jnp.concatenate([jnp.where(lane_head == h, x, jnp.zeros_like(x)) for h in range(HEADS_PER_HALF)], axis=0)


def _unstack_heads(x, lane_head, q):
    out = jnp.where(lane_head == 0, x[0:q], 0.0)
    for h in range(1, HEADS_PER_HALF):
        out = out + jnp.where(lane_head == h, x[h * q:(h + 1) * q], 0.0)
    return out


def _residue_rows(start, d):
    return pl.ds(start, QBLOCK, stride=d) if d > 1 else pl.ds(start, QBLOCK)


def _gather_rows(ref, start, d):
    return jnp.concatenate([ref[sl, _residue_rows(start, d), :] for sl in range(ref.shape[0])], axis=1)


def _attn_prompt_kernel(q_ref, kp_ref, kc_ref, vp_ref, vc_ref, bias_ref, o_ref, lse_ref, *, d, n_sub):
    qb = QBLOCK
    j = pl.program_id(1)
    half = pl.program_id(2)
    lane_head = lax.broadcasted_iota(jnp.int32, (qb, HALF), 1) // HEAD_DIM
    lane = lax.broadcasted_iota(jnp.int32, (qb, LANES), 1)
    first = jnp.where(j == 0, 1, 0)
    for r in range(d):
        k_prev = _gather_rows(kp_ref, r, d).astype(BF16)
        v_prev = _gather_rows(vp_ref, r, d).astype(BF16)
        for sub in range(n_sub):
            start = r + d * sub * qb
            q = _gather_rows(q_ref, start, d).astype(BF16)
            k_cur = _gather_rows(kc_ref, start, d).astype(BF16)
            v_cur = _gather_rows(vc_ref, start, d).astype(BF16)
            k = jnp.concatenate([k_prev, k_cur], axis=0)
            v = jnp.concatenate([v_prev, v_cur], axis=0)
            qs = _stack_heads(q, lane_head)
            s = lax.dot_general(qs, k, (((1,), (1,)), ((), ())), preferred_element_type=F32)
            s = s + (bias_ref[first, half] if sub == 0 else bias_ref[0, half])
            m = jnp.max(s, axis=-1, keepdims=True)
            p = jnp.exp(s - m)
            den = jnp.sum(p, axis=-1, keepdims=True)
            ov = jnp.dot((p / den).astype(BF16), v, preferred_element_type=F32)
            o = _unstack_heads(ov, lane_head, qb)
            rows = _residue_rows(start, d)
            for sl in range(o_ref.shape[0]):
                o_ref[sl, rows, :] = o[:, sl * LANES:(sl + 1) * LANES]
            lse = m + jnp.log(den)
            lse_tile = jnp.zeros((qb, LANES), F32)
            for h in range(HEADS_PER_HALF):
                lse_tile = jnp.where(lane == half * HEADS_PER_HALF + h, lse[h * qb:(h + 1) * qb], lse_tile)
            lse_ref[0, rows, :] = lse_tile
            k_prev, v_prev = k_cur, v_cur


def _attn_prompt(q_sl, kv_sl, bias_tiles, g, b, s):
    w, d = GROUPS[g]
    tb = d * ATTN_QUERIES[g]
    pb = d * QBLOCK
    n_tb = s // tb
    sph = HALF // LANES
    n_half = ATTN_WIDTH // HALF
    cur = lambda i, j: i * n_tb + j
    prv = lambda i, j: jnp.maximum(i * (s // pb) + j * (tb // pb) - 1, 0)
    return pl.pallas_call(
        functools.partial(_attn_prompt_kernel, d=d, n_sub=ATTN_QUERIES[g] // QBLOCK),
        grid=(b, n_tb, n_half),
        in_specs=[
            pl.BlockSpec((sph, tb, LANES), lambda i, j, hf: (g * n_half + hf, cur(i, j), 0)),
            pl.BlockSpec((sph, pb, LANES), lambda i, j, hf: (g * 2 * n_half + hf, prv(i, j), 0)),
            pl.BlockSpec((sph, tb, LANES), lambda i, j, hf: (g * 2 * n_half + hf, cur(i, j), 0)),
            pl.BlockSpec((sph, pb, LANES), lambda i, j, hf: (g * 2 * n_half + n_half + hf, prv(i, j), 0)),
            pl.BlockSpec((sph, tb, LANES), lambda i, j, hf: (g * 2 * n_half + n_half + hf, cur(i, j), 0)),
            _const_spec(bias_tiles.shape),
        ],
        out_specs=[pl.BlockSpec((sph, tb, LANES), lambda i, j, hf: (hf, cur(i, j), 0)),
                   pl.BlockSpec((1, tb, LANES), lambda i, j, hf: (hf, cur(i, j), 0))],
        out_shape=[jax.ShapeDtypeStruct((n_half * sph, b * s, LANES), F32),
                   jax.ShapeDtypeStruct((n_half, b * s, LANES), F32)],
        compiler_params=_params(3),
        name=f"attn_prompt_g{g}",
    )(q_sl, kv_sl, kv_sl, kv_sl, kv_sl, bias_tiles)


def _expand_heads(w, e_ref):
    hi = w.astype(BF16)
    lo = (w - hi.astype(F32)).astype(BF16)
    return jnp.dot(jnp.concatenate([hi, lo], axis=1), e_ref[...], preferred_element_type=F32)


def _merged_attention(o_refs, l_refs, e_ref, wo_ref):
    ls = [l_ref[0] + l_ref[1] for l_ref in l_refs]
    mx = jnp.maximum(jnp.maximum(ls[0], ls[1]), ls[2])
    es = [jnp.exp(l - mx) for l in ls]
    tot = es[0] + es[1] + es[2]
    merged = None
    for e, o_ref in zip(es, o_refs):
        o = jnp.concatenate([o_ref[sl] for sl in range(o_ref.shape[0])], axis=1)
        term = _expand_heads(e / tot, e_ref) * o
        merged = term if merged is None else merged + term
    return jnp.dot(merged.astype(BF16), wo_ref[...], preferred_element_type=F32)


def _head_expander():
    head = jnp.arange(2 * LANES, dtype=jnp.int32)[:, None] % LANES
    lane = jnp.arange(ATTN_WIDTH, dtype=jnp.int32)[None, :]
    return (lane // HEAD_DIM == head).astype(BF16)


N_SAMPLE_ATTN_INPUTS = 6 * N_GROUPS
N_SAMPLE_ATTN_OUTPUTS = 1 + N_GROUPS


def _attn_sample_unit(in_refs, out_refs, half):
    (q0_ref, q1_ref, q2_ref, kn0_ref, kn1_ref, kn2_ref, vn0_ref, vn1_ref, vn2_ref, c0_ref, c1_ref, c2_ref,
     bc0_ref, bc1_ref, bc2_ref, bn0_ref, bn1_ref, bn2_ref) = in_refs
    o_ref, n0_ref, n1_ref, n2_ref = out_refs
    t = q0_ref.shape[1]
    lane_head = lax.broadcasted_iota(jnp.int32, (t, HALF), 1) // HEAD_DIM
    is_new = lax.broadcasted_iota(jnp.int32, (HALF, LANES), 1) >= LANES - t
    zpad = jnp.zeros((LANES - t, HALF), F32)
    outs, lses = [], []
    for q_ref, kn_ref, vn_ref, c_ref, bc_ref, bn_ref, n_ref in (
            (q0_ref, kn0_ref, vn0_ref, c0_ref, bc0_ref, bn0_ref, n0_ref),
            (q1_ref, kn1_ref, vn1_ref, c1_ref, bc1_ref, bn1_ref, n1_ref),
            (q2_ref, kn2_ref, vn2_ref, c2_ref, bc2_ref, bn2_ref, n2_ref)):
        w = c_ref.shape[3]
        qs = _stack_heads(q_ref[0], lane_head)
        k_t = c_ref[0, 0]
        v_t = c_ref[0, 1]
        k_new = jnp.concatenate([zpad, kn_ref[0]], axis=0)
        v_new = jnp.concatenate([zpad, vn_ref[0]], axis=0)
        k_new_t = k_new.T
        v_new_t = v_new.T
        s_c = jnp.dot(qs, k_t.astype(BF16), preferred_element_type=F32) + bc_ref[half]
        s_n = jnp.dot(qs, k_new_t.astype(BF16), preferred_element_type=F32) + bn_ref[half]
        m = jnp.maximum(s_c.max(axis=-1, keepdims=True), s_n.max(axis=-1, keepdims=True))
        p_c = jnp.exp(s_c - m)
        p_n = jnp.exp(s_n - m)
        den = p_c.sum(axis=-1, keepdims=True) + p_n.sum(axis=-1, keepdims=True)
        ov = lax.dot_general((p_c / den).astype(BF16), v_t.astype(BF16), (((1,), (1,)), ((), ())),
                             preferred_element_type=F32)
        ov = ov + jnp.dot((p_n / den).astype(BF16), v_new.astype(BF16), preferred_element_type=F32)
        outs.append(ov)
        lses.append(m + jnp.log(den))
        for kv, (old_t, new_t) in enumerate(((k_t, k_new_t), (v_t, v_new_t))):
            rolled = pltpu.roll(old_t, w - t, axis=1)
            if w > LANES:
                n_ref[0, kv, :, 0:w - LANES] = rolled[:, 0:w - LANES]
            n_ref[0, kv, :, w - LANES:w] = jnp.where(is_new, new_t, rolled[:, w - LANES:w])
    mx = jnp.maximum(jnp.maximum(lses[0], lses[1]), lses[2])
    es = [jnp.exp(l - mx) for l in lses]
    tot = es[0] + es[1] + es[2]
    merged = (es[0] / tot) * outs[0] + (es[1] / tot) * outs[1] + (es[2] / tot) * outs[2]
    o_ref[0] = _unstack_heads(merged, lane_head, t).astype(o_ref.dtype)


def _attn_sample_operands(q, kv_new, caches_t, bias_gs):
    b, t, _ = q.shape
    n_half = ATTN_WIDTH // HALF
    tok = lambda col: pl.BlockSpec((1, t, HALF), lambda i, col=col: (i // n_half, 0, col + i % n_half))
    in_specs = [tok(g * n_half) for g in range(N_GROUPS)]
    in_specs += [tok(g * 2 * n_half) for g in range(N_GROUPS)]
    in_specs += [tok(g * 2 * n_half + n_half) for g in range(N_GROUPS)]
    args = [q] * N_GROUPS + [kv_new] * (2 * N_GROUPS)
    cache_spec = lambda g: pl.BlockSpec((1, 2, HALF, GROUPS[g][0]), lambda i: (i // n_half, 0, i % n_half, 0))
    in_specs += [cache_spec(g) for g in range(N_GROUPS)]
    args += list(caches_t)
    tiles = [_sample_bias_tiles(bias_gs[g], g, t) for g in range(N_GROUPS)]
    for part in range(2):
        for g in range(N_GROUPS):
            in_specs.append(_const_spec(tiles[g][part].shape))
            args.append(tiles[g][part])
    out_specs = [pl.BlockSpec((1, t, HALF), lambda i: (i // n_half, 0, i % n_half))]
    out_specs += [cache_spec(g) for g in range(N_GROUPS)]
    out_shape = [jax.ShapeDtypeStruct((b, t, ATTN_WIDTH), BF16)] + [jax.ShapeDtypeStruct(c.shape, F32) for c in caches_t]
    return in_specs, args, out_specs, out_shape


def _out_proj_kernel(x_ref, a_ref, wo_ref, out_ref):
    out_ref[...] = x_ref[...] + jnp.dot(a_ref[...], wo_ref[...], preferred_element_type=F32)


def _out_proj(x, a, w_o):
    n, d = x.shape
    return pl.pallas_call(
        _out_proj_kernel,
        out_shape=jax.ShapeDtypeStruct((n, d), F32),
        compiler_params=pltpu.CompilerParams(vmem_limit_bytes=VMEM_LIMIT),
        name="out_proj",
    )(x, a, w_o)


def _cache_view(c_t, b):
    w = c_t.shape[-1]
    return jnp.transpose(c_t.reshape(b, 2, HEADS_PER_GROUP, HEAD_DIM, w), (0, 4, 1, 2, 3))


def _conv_args(wts):
    return (wts["norms"][0, 1], wts["conv_w_pw1"][0], wts["conv_b_pw1"][0], wts["conv_w_dw"][0], wts["conv_b_dw"][0],
            wts["conv_ln_g"][0], wts["conv_ln_b"][0], wts["conv_w_pw2"][0], wts["conv_b_pw2"][0])


def _sample_until_attention(x, conv_prev, kv_bufs, wts):
    b, t, d = x.shape
    norms, w_in, w_out = wts["norms"], wts["ffn_w_in"], wts["ffn_w_out"]
    xf = _ffn(x.reshape(b * t, d), norms[0, 0], w_in, w_out, (0, 0))
    x3, conv_state = _conv_module(xf.reshape(b, t, d), conv_prev, *_conv_args(wts), SUBLANES, t)
    xf = _ffn(x3.reshape(b * t, d), norms[0, 2], w_in, w_out, (0, 1))
    kv_new = _proj(xf, wts["norm_kv"], wts["w_kv"], F32).reshape(b, t, -1)
    xf = _ffn(xf, norms[1, 0], w_in, w_out, (1, 0))
    q = _proj(xf, norms[1, 1], wts["attn_w_q"][0], BF16, scale=HEAD_DIM ** -0.5).reshape(b, t, -1)
    caches_t = [jnp.transpose(buf, (0, 2, 3, 4, 1)).reshape(b, 2, ATTN_WIDTH, buf.shape[1]) for buf in kv_bufs]
    return xf, conv_state, (q, kv_new, caches_t, wts["bias_gs"])


def _sample_after_attention(xf, attn, wts):
    xf = _out_proj(xf, attn.reshape(xf.shape[0], ATTN_WIDTH), wts["attn_w_o"][0])
    return _ffn(xf, wts["norms"][1, 2], wts["ffn_w_in"], wts["ffn_w_out"], (1, 1), g_final=wts["norm_final"])


def _prompt_trunk(x, wts, sample_attn):
    b, t, d = x.shape
    norms, w_in, w_out = wts["norms"], wts["ffn_w_in"], wts["ffn_w_out"]
    xf, *sample_res = _ffn(x.reshape(b * t, d), norms[0, 0], w_in, w_out, (0, 0), sample_attn=sample_attn)
    x3, conv_state = _conv_module_pipelined(xf.reshape(b, t, d), *_conv_args(wts))
    xf = _ffn(x3.reshape(b * t, d), norms[0, 2], w_in, w_out, (0, 1))
    shared, *kv_t = _kv_prompt(xf.reshape(b, t, d), wts["norm_kv"], wts["w_kv"])
    q_proj = (norms[1, 1], wts["attn_w_q"][0], HEAD_DIM ** -0.5)
    xf, q_sl = _ffn(xf, norms[1, 0], w_in, w_out, (1, 0), q_proj=q_proj)
    outs, lses = [], []
    for g in range(N_GROUPS):
        o, lse = _attn_prompt(q_sl, shared, _prompt_bias_tiles(wts["bias_gs"][g]), g, b, t)
        outs.append(o)
        lses.append(lse)
    xf = _ffn(xf, norms[1, 2], w_in, w_out, (1, 1), merge=(outs, lses, wts["attn_w_o"][0]),
              g_final=wts["norm_final"])
    return xf.reshape(b, t, d), conv_state, [_cache_view(c, b) for c in kv_t], sample_res


def kernel(x_prompt, x_sample, state_conv, cache_kv_w128, cache_kv_w512, cache_kv_w2048, norms, ffn_w_in, ffn_w_out,
           conv_w_pw1, conv_b_pw1, conv_w_dw, conv_b_dw, conv_ln_g, conv_ln_b, conv_w_pw2, conv_b_pw2, norm_kv, w_kv,
           attn_w_q, attn_w_o, rel_bias, norm_final):
    wts = dict(
        norms=norms, ffn_w_in=ffn_w_in.astype(BF16), ffn_w_out=ffn_w_out.astype(BF16),
        conv_w_pw1=conv_w_pw1.astype(BF16), conv_b_pw1=conv_b_pw1, conv_w_dw=conv_w_dw, conv_b_dw=conv_b_dw,
        conv_ln_g=conv_ln_g, conv_ln_b=conv_ln_b, conv_w_pw2=conv_w_pw2.astype(BF16), conv_b_pw2=conv_b_pw2,
        norm_kv=norm_kv, w_kv=w_kv.astype(BF16), attn_w_q=attn_w_q.astype(BF16), attn_w_o=attn_w_o.astype(BF16),
        rel_bias=rel_bias, norm_final=norm_final)
    wts["bias_gs"] = [_group_bias(rel_bias, g) for g in range(N_GROUPS)]
    bs, ts, d = x_sample.shape
    xs, conv_sample, sample_attn = _sample_until_attention(
        x_sample, state_conv[0], (cache_kv_w128, cache_kv_w512, cache_kv_w2048), wts)
    y_prompt, conv_prompt, kv_prompt, (attn_s, *caches_new) = _prompt_trunk(x_prompt, wts, sample_attn)
    y_sample = _sample_after_attention(xs, attn_s, wts).reshape(bs, ts, d)
    kv_sample = [_cache_view(c, bs) for c in caches_new]
    return (y_prompt, y_sample, conv_prompt[None], conv_sample[None], kv_prompt[0], kv_sample[0],
            kv_prompt[1], kv_sample[1], kv_prompt[2], kv_sample[2])
```

```python
import functools
import math

import jax
import jax.numpy as jnp
from jax import lax
from jax.experimental import pallas as pl
from jax.experimental.pallas import tpu as pltpu

F32 = jnp.float32
BF16 = jnp.bfloat16

EPS = 1e-6
NEG_INF = -1e30
GROUPS = ((128, 1), (512, 4), (2048, 16))
N_GROUPS = len(GROUPS)
HEADS_PER_GROUP = 8
HEAD_DIM = 64
ATTN_WIDTH = HEADS_PER_GROUP * HEAD_DIM
ROW_WIDTH = 2 * ATTN_WIDTH
N_KEYS = 129
N_BUCKETS = 32
MAX_DISTANCE = 2048
PAST_LEN = 8192
CONV_WIDTH = 31
CONV_HIST = 32
LANES = 128
SUBLANES = 8
HALF = 256
HEADS_PER_HALF = HALF // HEAD_DIM
QBLOCK = 128
ATTN_QUERIES = (4096, 1024, 256)
TOKEN_TILE = 512
PLAIN_FFN_TILE = 1024
SAMPLE_CONV_BATCH = 16
CONV_ROWS = 64
VMEM_LIMIT = 56 * 1024 * 1024


def _params(n_axes):
    return pltpu.CompilerParams(dimension_semantics=("arbitrary",) * n_axes, vmem_limit_bytes=VMEM_LIMIT)


def _const_spec(shape):
    zeros = (0,) * len(shape)
    return pl.BlockSpec(shape, lambda *_: zeros, pipeline_mode=pl.Buffered(1))


def _rms(x, g):
    return x * lax.rsqrt(jnp.mean(x * x, axis=-1, keepdims=True) + EPS) * g


def _silu(x):
    return x * jax.nn.sigmoid(x)


def _ffn_kernel(*refs, d_ff, n_chunk, has_merge, has_final, q_scale, has_sample_attn):
    refs = list(refs)
    x_ref = refs.pop(0)
    if has_merge:
        attn_refs = [refs.pop(0) for _ in range(2 * N_GROUPS)]
        e_ref, wo_ref = refs.pop(0), refs.pop(0)
    g_ref, win_ref, wout_ref = refs.pop(0), refs.pop(0), refs.pop(0)
    if has_final:
        gf_ref = refs.pop(0)
    if q_scale is not None:
        gq_ref, wq_ref = refs.pop(0), refs.pop(0)
    if has_sample_attn:
        sample_in = [refs.pop(0) for _ in range(N_SAMPLE_ATTN_INPUTS)]
    o_ref = refs.pop(0)
    if q_scale is not None:
        q_ref = refs.pop(0)
    if has_sample_attn:
        sample_out = [refs.pop(0) for _ in range(N_SAMPLE_ATTN_OUTPUTS)]
    act_ref = refs.pop(0)

    if has_sample_attn:
        _attn_sample_unit(sample_in, sample_out, pl.program_id(0) % (ATTN_WIDTH // HALF))

    x = x_ref[...]
    if has_merge:
        x = x + _merged_attention(attn_refs[:N_GROUPS], attn_refs[N_GROUPS:], e_ref, wo_ref)
    h = _rms(x, g_ref[...]).astype(BF16)
    for c in range(d_ff // n_chunk):
        lo = c * n_chunk
        gate = jnp.dot(h, win_ref[:, lo:lo + n_chunk], preferred_element_type=F32)
        up = jnp.dot(h, win_ref[:, d_ff + lo:d_ff + lo + n_chunk], preferred_element_type=F32)
        act_ref[:, lo:lo + n_chunk] = (_silu(gate) * up).astype(BF16)
    y = jnp.dot(act_ref[...], wout_ref[...], preferred_element_type=F32)
    o = x + 0.5 * y
    if q_scale is not None:
        hq = _rms(o, gq_ref[...]).astype(BF16)
        _store_slabs(q_ref, jnp.dot(hq, wq_ref[...], preferred_element_type=F32) * q_scale)
    if has_final:
        o = _rms(o, gf_ref[...])
    o_ref[...] = o


def _ffn(x, g, w_in, w_out, idx, *, merge=None, g_final=None, q_proj=None, sample_attn=None):
    n, d = x.shape
    d_ff = w_out.shape[-2]
    plain = merge is None and q_proj is None and sample_attn is None
    tm = min(PLAIN_FFN_TILE if plain else TOKEN_TILE, n)
    if sample_attn is not None:
        n_units = sample_attn[0].shape[0] * (ATTN_WIDTH // HALF)
        tm = n // n_units
        assert tm * n_units == n and tm % SUBLANES == 0, (n, n_units)
    tok = lambda width: pl.BlockSpec((tm, width), lambda i: (i, 0))
    slabs = lambda count: pl.BlockSpec((count, tm, LANES), lambda i: (0, i, 0))
    stacked = lambda rows, cols: pl.BlockSpec((None, None, rows, cols), lambda i: (*idx, 0, 0),
                                              pipeline_mode=pl.Buffered(1))
    in_specs = [tok(d)]
    args = [x]
    if merge is not None:
        outs, lses, w_o = merge
        in_specs += [slabs(ATTN_WIDTH // LANES)] * N_GROUPS + [slabs(ATTN_WIDTH // HALF)] * N_GROUPS
        in_specs += [_const_spec((2 * LANES, ATTN_WIDTH)), _const_spec((ATTN_WIDTH, d))]
        args += [*outs, *lses, _head_expander(), w_o]
    in_specs += [_const_spec((1, d)), stacked(d, 2 * d_ff), stacked(d_ff, d)]
    args += [g.reshape(1, d), w_in, w_out]
    if g_final is not None:
        in_specs.append(_const_spec((1, d)))
        args.append(g_final.reshape(1, d))
    out_specs = [tok(d)]
    out_shape = [jax.ShapeDtypeStruct((n, d), F32)]
    q_scale = None
    if q_proj is not None:
        g_q, w_q, q_scale = q_proj
        n_q = w_q.shape[1]
        in_specs += [_const_spec((1, d)), _const_spec((d, n_q))]
        args += [g_q.reshape(1, d), w_q]
        out_specs.append(slabs(n_q // LANES))
        out_shape.append(jax.ShapeDtypeStruct((n_q // LANES, n, LANES), F32))
    if sample_attn is not None:
        s_in_specs, s_args, s_out_specs, s_out_shape = _attn_sample_operands(*sample_attn)
        in_specs += s_in_specs
        args += s_args
        out_specs += s_out_specs
        out_shape += s_out_shape
    res = pl.pallas_call(
        functools.partial(_ffn_kernel, d_ff=d_ff, n_chunk=HALF, has_merge=merge is not None,
                          has_final=g_final is not None, q_scale=q_scale, has_sample_attn=sample_attn is not None),
        grid=(n // tm,),
        in_specs=in_specs,
        out_specs=out_specs,
        out_shape=out_shape,
        scratch_shapes=[pltpu.VMEM((tm, d_ff), BF16)],
        compiler_params=_params(1),
        name="ffn",
    )(*args)
    return res[0] if len(res) == 1 else res


def _proj_kernel(x_ref, g_ref, w_ref, o_ref, *, scale):
    h = _rms(x_ref[...], g_ref[...]).astype(BF16)
    y = jnp.dot(h, w_ref[...], preferred_element_type=F32)
    if scale != 1.0:
        y = y * scale
    o_ref[...] = y.astype(o_ref.dtype)


def _proj(x, g, w, out_dtype, scale=1.0):
    n, d = x.shape
    n_out = w.shape[1]
    tm = min(TOKEN_TILE, n)
    return pl.pallas_call(
        functools.partial(_proj_kernel, scale=scale),
        grid=(n // tm,),
        in_specs=[pl.BlockSpec((tm, d), lambda i: (i, 0)), _const_spec((1, d)), _const_spec((d, n_out))],
        out_specs=pl.BlockSpec((tm, n_out), lambda i: (i, 0)),
        out_shape=jax.ShapeDtypeStruct((n, n_out), out_dtype),
        compiler_params=_params(1),
        name="proj",
    )(x, g.reshape(1, d), w)


def _store_slabs(slab_ref, y):
    for sl in range(slab_ref.shape[0]):
        slab_ref[sl] = y[:, sl * LANES:(sl + 1) * LANES]


def _kv_prompt_kernel(x_ref, g_ref, w_ref, sh_ref, t0_ref, t1_ref, t2_ref, *, n_t, tm):
    h = _rms(x_ref[0], g_ref[...]).astype(BF16)
    y = jnp.dot(h, w_ref[...], preferred_element_type=F32)
    _store_slabs(sh_ref, y)
    for g, t_ref in enumerate((t0_ref, t1_ref, t2_ref)):
        rows = min(GROUPS[g][0], n_t * tm, tm)
        t_ref[0] = y[tm - rows:, g * ROW_WIDTH:(g + 1) * ROW_WIDTH].T


def _kv_prompt(x, g, w):
    b, s, d = x.shape
    n_out = w.shape[1]
    tm = min(TOKEN_TILE, s)
    n_t = s // tm
    out_specs = [pl.BlockSpec((n_out // LANES, tm, LANES), lambda i, j: (0, i * n_t + j, 0))]
    out_shape = [jax.ShapeDtypeStruct((n_out // LANES, b * s, LANES), F32)]
    for grp in range(N_GROUPS):
        keep = min(GROUPS[grp][0], s)
        first = n_t - max(keep // tm, 1)
        out_specs.append(pl.BlockSpec((1, ROW_WIDTH, min(keep, tm)),
                                      lambda i, j, first=first: (i, 0, jnp.maximum(j - first, 0))))
        out_shape.append(jax.ShapeDtypeStruct((b, ROW_WIDTH, keep), F32))
    return pl.pallas_call(
        functools.partial(_kv_prompt_kernel, n_t=n_t, tm=tm),
        grid=(b, n_t),
        in_specs=[pl.BlockSpec((1, tm, d), lambda i, j: (i, j, 0)), _const_spec((1, d)), _const_spec((d, n_out))],
        out_specs=out_specs,
        out_shape=out_shape,
        compiler_params=_params(2),
        name="kv_prompt",
    )(x, g.reshape(1, d), w)


def _conv_kernel(x_ref, prev_ref, g_ref, w1_ref, b1_ref, wdw_ref, bdw_ref, lng_ref, lnb_ref, w2_ref, b2_ref,
                 o_ref, st_ref, uext_ref, ush_ref, conv_ref, *, bb, ts, n_t, rc):
    it = pl.program_id(1)
    d = x_ref.shape[-1]
    c_in = w2_ref.shape[0]
    x = x_ref[...].reshape(bb * ts, d)
    h = _rms(x, g_ref[...]).astype(BF16)
    ag = jnp.dot(h, w1_ref[...], preferred_element_type=F32) + b1_ref[...]
    u = ag[:, :c_in] * jax.nn.sigmoid(ag[:, c_in:])

    @pl.when(it == 0)
    def _():
        uext_ref[:, 0:CONV_HIST, :] = prev_ref[...]

    uext_ref[:, CONV_HIST:CONV_HIST + ts, :] = u.reshape(bb, ts, c_in)

    lead = CONV_HIST - (CONV_WIDTH - 1)
    n_lb = c_in // LANES
    for s in range(SUBLANES):
        n_sh = ts + SUBLANES * ((CONV_WIDTH - 1 - s) // SUBLANES)
        for lb in range(n_lb):
            ush_ref[s, lb, :, 0:n_sh, :] = uext_ref[:, lead + s:lead + s + n_sh, lb * LANES:(lb + 1) * LANES]

    reps = rc // SUBLANES
    for lb in range(n_lb):
        lanes = slice(lb * LANES, (lb + 1) * LANES)
        taps = [wdw_ref[k, :, lanes] for k in range(CONV_WIDTH)]
        bias = bdw_ref[:, lanes]

        def chunk(ci, carry, lb=lb, lanes=lanes, taps=taps, bias=bias):
            r0 = pl.multiple_of(ci * rc, rc)
            acc = jnp.broadcast_to(bias.reshape(1, 1, LANES), (bb, rc, LANES))
            for k in range(CONV_WIDTH):
                win = ush_ref[k % SUBLANES, lb, :, pl.ds(r0 + SUBLANES * (k // SUBLANES), rc), :]
                wk = jnp.concatenate([taps[k]] * reps, axis=0) if reps > 1 else taps[k]
                acc = acc + win * wk[None]
            conv_ref[:, pl.ds(r0, rc), lanes] = acc
            return carry

        lax.fori_loop(0, ts // rc, chunk, 0)

    acc = conv_ref[...].reshape(bb * ts, c_in)
    mu = jnp.mean(acc, axis=-1, keepdims=True)
    cen = acc - mu
    var = jnp.mean(cen * cen, axis=-1, keepdims=True)
    c = _silu(cen * lax.rsqrt(var + EPS) * lng_ref[...] + lnb_ref[...]).astype(BF16)
    y = jnp.dot(c, w2_ref[...], preferred_element_type=F32) + b2_ref[...]
    o_ref[...] = (x + y).reshape(bb, ts, d)

    @pl.when(it == n_t - 1)
    def _():
        st_ref[...] = uext_ref[:, ts:ts + CONV_HIST, :]

    if n_t > 1:
        uext_ref[:, 0:CONV_HIST, :] = uext_ref[:, ts:ts + CONV_HIST, :]


def _conv_module(x, prev, g, w1, b1, wdw, bdw, lng, lnb, w2, b2, bb, ts):
    b, t, d = x.shape
    c_in = w2.shape[0]
    n_t = t // ts
    rc = min(ts, CONV_ROWS)
    row = lambda v: v.reshape(1, -1)
    in_specs = [pl.BlockSpec((bb, ts, d), lambda i, j: (i, j, 0)),
                pl.BlockSpec((bb, CONV_HIST, c_in), lambda i, j: (i, 0, 0)),
                _const_spec((1, d)), _const_spec((d, 2 * c_in)), _const_spec((1, 2 * c_in)),
                _const_spec((CONV_WIDTH, SUBLANES, c_in)), _const_spec((1, c_in)), _const_spec((1, c_in)),
                _const_spec((1, c_in)), _const_spec((c_in, d)), _const_spec((1, d))]
    wdw_rep = jnp.broadcast_to(wdw[:, None, :], (CONV_WIDTH, SUBLANES, c_in))
    prev_rows = jnp.pad(prev, ((0, 0), (CONV_HIST - prev.shape[1], 0), (0, 0)))
    args = [x, prev_rows, row(g), w1, row(b1), wdw_rep, row(bdw), row(lng), row(lnb), w2, row(b2)]
    out, st = pl.pallas_call(
        functools.partial(_conv_kernel, bb=bb, ts=ts, n_t=n_t, rc=rc),
        grid=(b // bb, n_t),
        in_specs=in_specs,
        out_specs=[pl.BlockSpec((bb, ts, d), lambda i, j: (i, j, 0)),
                   pl.BlockSpec((bb, CONV_HIST, c_in), lambda i, j: (i, 0, 0))],
        out_shape=[jax.ShapeDtypeStruct((b, t, d), F32), jax.ShapeDtypeStruct((b, CONV_HIST, c_in), F32)],
        scratch_shapes=[pltpu.VMEM((bb, ts + CONV_HIST, c_in), F32),
                        pltpu.VMEM((SUBLANES, c_in // LANES, bb, ts + 24, LANES), F32),
                        pltpu.VMEM((bb, ts, c_in), F32)],
        compiler_params=_params(2),
        name="conv_module",
    )(*args)
    return out, st[:, CONV_HIST - (CONV_WIDTH - 1):]


def _conv_pipe_kernel(xc_ref, xp_ref, g_ref, w1_ref, b1_ref, wdw_ref, bdw_ref, lng_ref, lnb_ref, w2_ref, b2_ref,
                      o_ref, st_ref, h_ref, u0_ref, u1_ref, ush_ref, conv_ref, *, ts, n_t, rc):
    i = pl.program_id(0)
    n_lb = conv_ref.shape[0]
    lead = CONV_HIST - (CONV_WIDTH - 1)
    reps = rc // SUBLANES

    @pl.when(i == 0)
    def _():
        u1_ref[...] = jnp.zeros(u1_ref.shape, F32)

    def step(ua_ref, ub_ref):
        starts_sequence = (i % n_t) == 0
        h_ref[...] = _rms(xc_ref[...], g_ref[...]).astype(BF16)
        for lb in range(n_lb):
            lanes = slice(lb * LANES, (lb + 1) * LANES)
            ag = jnp.dot(h_ref[...], w1_ref[lb], preferred_element_type=F32) + b1_ref[lb]
            u = ag[:, :LANES] * jax.nn.sigmoid(ag[:, LANES:])
            ua_ref[lb, 0:CONV_HIST, :] = jnp.where(starts_sequence, 0.0, ub_ref[lb, ts:ts + CONV_HIST, :])
            ua_ref[lb, CONV_HIST:CONV_HIST + ts, :] = u
            buf = lb % 2
            for s in range(SUBLANES):
                n_sh = ts + SUBLANES * ((CONV_WIDTH - 1 - s) // SUBLANES)
                ush_ref[buf, s, 0:n_sh, :] = ub_ref[lb, lead + s:lead + s + n_sh, :]
            taps = [wdw_ref[k, :, lanes] for k in range(CONV_WIDTH)]
            bias = jnp.broadcast_to(bdw_ref[:, lanes], (rc, LANES))
            for ci in range(ts // rc):
                r0 = ci * rc
                acc = bias
                for k in range(CONV_WIDTH):
                    lo = r0 + SUBLANES * (k // SUBLANES)
                    wk = jnp.concatenate([taps[k]] * reps, axis=0) if reps > 1 else taps[k]
                    acc = acc + ush_ref[buf, k % SUBLANES, lo:lo + rc, :] * wk
                conv_ref[lb, r0:r0 + rc, :] = acc

        acc = jnp.concatenate([conv_ref[lb] for lb in range(n_lb)], axis=1)
        mu = jnp.mean(acc, axis=-1, keepdims=True)
        cen = acc - mu
        var = jnp.mean(cen * cen, axis=-1, keepdims=True)
        c = _silu(cen * lax.rsqrt(var + EPS) * lng_ref[...] + lnb_ref[...]).astype(BF16)
        y = jnp.dot(c, w2_ref[...], preferred_element_type=F32) + b2_ref[...]
        o_ref[...] = xp_ref[...] + y

        @pl.when((i % n_t) == n_t - 1)
        def _():
            for lb in range(n_lb):
                st_ref[0, :, lb * LANES:(lb + 1) * LANES] = ua_ref[lb, ts:ts + CONV_HIST, :]

    @pl.when(i % 2 == 0)
    def _():
        step(u0_ref, u1_ref)

    @pl.when(i % 2 == 1)
    def _():
        step(u1_ref, u0_ref)


def _conv_module_pipelined(x, g, w1, b1, wdw, bdw, lng, lnb, w2, b2):
    b, t, d = x.shape
    c_in = w2.shape[0]
    ts = TOKEN_TILE
    n_t = t // ts
    n_tiles = b * n_t
    n_lb = c_in // LANES
    row = lambda v: v.reshape(1, -1)
    cols = lambda v, lb: v[..., lb * LANES:(lb + 1) * LANES]
    w1_blocks = jnp.stack([jnp.concatenate([cols(w1, lb), cols(w1, n_lb + lb)], axis=-1) for lb in range(n_lb)])
    b1_blocks = jnp.stack([jnp.concatenate([cols(row(b1), lb), cols(row(b1), n_lb + lb)], axis=-1)
                           for lb in range(n_lb)])
    wdw_rep = jnp.broadcast_to(wdw[:, None, :], (CONV_WIDTH, SUBLANES, c_in))
    last = n_tiles - 1
    out, st = pl.pallas_call(
        functools.partial(_conv_pipe_kernel, ts=ts, n_t=n_t, rc=CONV_ROWS // 2),
        grid=(n_tiles + 1,),
        in_specs=[pl.BlockSpec((ts, d), lambda i: (jnp.minimum(i, last), 0)),
                  pl.BlockSpec((ts, d), lambda i: (jnp.maximum(i - 1, 0), 0)),
                  _const_spec((1, d)), _const_spec((n_lb, d, 2 * LANES)), _const_spec((n_lb, 1, 2 * LANES)),
                  _const_spec((CONV_WIDTH, SUBLANES, c_in)), _const_spec((1, c_in)), _const_spec((1, c_in)),
                  _const_spec((1, c_in)), _const_spec((c_in, d)), _const_spec((1, d))],
        out_specs=[pl.BlockSpec((ts, d), lambda i: (jnp.maximum(i - 1, 0), 0)),
                   pl.BlockSpec((1, CONV_HIST, c_in), lambda i: (jnp.minimum(i, last) // n_t, 0, 0))],
        out_shape=[jax.ShapeDtypeStruct((b * t, d), F32), jax.ShapeDtypeStruct((b, CONV_HIST, c_in), F32)],
        scratch_shapes=[pltpu.VMEM((ts, d), BF16),
                        pltpu.VMEM((n_lb, ts + CONV_HIST, LANES), F32),
                        pltpu.VMEM((n_lb, ts + CONV_HIST, LANES), F32),
                        pltpu.VMEM((2, SUBLANES, ts + 24, LANES), F32),
                        pltpu.VMEM((n_lb, ts, LANES), F32)],
        compiler_params=_params(1),
        name="conv_pipelined",
    )(x.reshape(b * t, d), x.reshape(b * t, d), row(g), w1_blocks, b1_blocks, wdw_rep, row(bdw), row(lng), row(lnb),
      w2, row(b2))
    return out.reshape(b, t, d), st[:, CONV_HIST - (CONV_WIDTH - 1):]


def _t5_bucket(dist):
    max_exact = N_BUCKETS // 2
    d_f = jnp.maximum(dist, 1).astype(F32)
    large = max_exact + (jnp.log(d_f / max_exact) / math.log(MAX_DISTANCE / max_exact)
                         * (N_BUCKETS - max_exact)).astype(jnp.int32)
    large = jnp.minimum(large, N_BUCKETS - 1)
    return jnp.where(dist < max_exact, dist, large)


def _group_bias(rel_bias, g):
    _, d = GROUPS[g]
    dist = d * jnp.arange(N_KEYS, dtype=jnp.int32)
    b = jnp.take(rel_bias, _t5_bucket(dist), axis=0)
    return b[:, g * HEADS_PER_GROUP:(g + 1) * HEADS_PER_GROUP].T.astype(F32)


def _toeplitz(p, n_rows, n_cols):
    length = p.shape[1]
    stride = length - 1
    flat = jnp.tile(p, (1, n_rows))[:, :n_rows * stride]
    return flat.reshape(p.shape[0], n_rows, stride)[:, :, :n_cols]


def _stack_rows(t):
    return t.reshape(2, HEADS_PER_HALF * t.shape[1], t.shape[2])


def _prompt_bias_tiles(bias_g):
    neg = jnp.full((bias_g.shape[0], 2 * QBLOCK + 1 - N_KEYS), NEG_INF, F32)
    p = jnp.concatenate([bias_g[:, ::-1], neg], axis=1)
    regular = _toeplitz(p, QBLOCK, 2 * QBLOCK)
    c = jnp.arange(2 * QBLOCK, dtype=jnp.int32)[None, None, :]
    first = jnp.where(c >= QBLOCK, regular, NEG_INF)
    return jnp.stack([_stack_rows(regular), _stack_rows(first)], axis=0)


def _sample_bias_tiles(bias_g, g, t_new):
    w, d = GROUPS[g]
    n_heads = bias_g.shape[0]
    vals = bias_g[:, N_KEYS - 1:0:-1]
    strided = jnp.concatenate([vals[:, :, None], jnp.full((n_heads, N_KEYS - 1, d - 1), NEG_INF, F32)], axis=2)
    p = jnp.concatenate([strided.reshape(n_heads, w), jnp.full((n_heads, t_new), NEG_INF, F32)], axis=1)
    cached = _toeplitz(p, t_new, w)
    row = jnp.arange(w, dtype=jnp.int32)[None, None, :]
    min_valid = 2 * w - PAST_LEN
    cached = jnp.where(w + row >= min_valid, cached, NEG_INF)
    t = jnp.arange(t_new, dtype=jnp.int32)[None, :, None]
    i = jnp.arange(LANES, dtype=jnp.int32)[None, None, :] - (LANES - t_new)
    dist = t - i
    new = jnp.full((n_heads, t_new, LANES), NEG_INF, F32)
    for j in range((t_new - 1) // d + 1):
        new = jnp.where((i >= 0) & (dist == d * j), bias_g[:, j][:, None, None], new)
    return _stack_rows(cached), _stack_rows(new)


def _stack_heads(x, lane_head):
    return jnp.concatenate([jnp.where(lane_head == h, x, jnp.zeros_like(x)) for h in range(HEADS_PER_HALF)], axis=0)


def _unstack_heads(x, lane_head, q):
    out = jnp.where(lane_head == 0, x[0:q], 0.0)
    for h in range(1, HEADS_PER_HALF):
        out = out + jnp.where(lane_head == h, x[h * q:(h + 1) * q], 0.0)
    return out


def _residue_rows(start, d):
    return pl.ds(start, QBLOCK, stride=d) if d > 1 else pl.ds(start, QBLOCK)


def _gather_rows(ref, start, d):
    return jnp.concatenate([ref[sl, _residue_rows(start, d), :] for sl in range(ref.shape[0])], axis=1)


def _attn_prompt_kernel(q_ref, kp_ref, kc_ref, vp_ref, vc_ref, bias_ref, o_ref, lse_ref, *, d, n_sub):
    qb = QBLOCK
    j = pl.program_id(1)
    half = pl.program_id(2)
    lane_head = lax.broadcasted_iota(jnp.int32, (qb, HALF), 1) // HEAD_DIM
    lane = lax.broadcasted_iota(jnp.int32, (qb, LANES), 1)
    first = jnp.where(j == 0, 1, 0)
    for r in range(d):
        k_prev = _gather_rows(kp_ref, r, d).astype(BF16)
        v_prev = _gather_rows(vp_ref, r, d).astype(BF16)
        for sub in range(n_sub):
            start = r + d * sub * qb
            q = _gather_rows(q_ref, start, d).astype(BF16)
            k_cur = _gather_rows(kc_ref, start, d).astype(BF16)
            v_cur = _gather_rows(vc_ref, start, d).astype(BF16)
            k = jnp.concatenate([k_prev, k_cur], axis=0)
            v = jnp.concatenate([v_prev, v_cur], axis=0)
            qs = _stack_heads(q, lane_head)
            s = lax.dot_general(qs, k, (((1,), (1,)), ((), ())), preferred_element_type=F32)
            s = s + (bias_ref[first, half] if sub == 0 else bias_ref[0, half])
            m = jnp.max(s, axis=-1, keepdims=True)
            p = jnp.exp(s - m)
            den = jnp.sum(p, axis=-1, keepdims=True)
            ov = jnp.dot((p / den).astype(BF16), v, preferred_element_type=F32)
            o = _unstack_heads(ov, lane_head, qb)
            rows = _residue_rows(start, d)
            for sl in range(o_ref.shape[0]):
                o_ref[sl, rows, :] = o[:, sl * LANES:(sl + 1) * LANES]
            lse = m + jnp.log(den)
            lse_tile = jnp.zeros((qb, LANES), F32)
            for h in range(HEADS_PER_HALF):
                lse_tile = jnp.where(lane == half * HEADS_PER_HALF + h, lse[h * qb:(h + 1) * qb], lse_tile)
            lse_ref[0, rows, :] = lse_tile
            k_prev, v_prev = k_cur, v_cur


def _attn_prompt(q_sl, kv_sl, bias_tiles, g, b, s):
    w, d = GROUPS[g]
    tb = d * ATTN_QUERIES[g]
    pb = d * QBLOCK
    n_tb = s // tb
    sph = HALF // LANES
    n_half = ATTN_WIDTH // HALF
    cur = lambda i, j: i * n_tb + j
    prv = lambda i, j: jnp.maximum(i * (s // pb) + j * (tb // pb) - 1, 0)
    return pl.pallas_call(
        functools.partial(_attn_prompt_kernel, d=d, n_sub=ATTN_QUERIES[g] // QBLOCK),
        grid=(b, n_tb, n_half),
        in_specs=[
            pl.BlockSpec((sph, tb, LANES), lambda i, j, hf: (g * n_half + hf, cur(i, j), 0)),
            pl.BlockSpec((sph, pb, LANES), lambda i, j, hf: (g * 2 * n_half + hf, prv(i, j), 0)),
            pl.BlockSpec((sph, tb, LANES), lambda i, j, hf: (g * 2 * n_half + hf, cur(i, j), 0)),
            pl.BlockSpec((sph, pb, LANES), lambda i, j, hf: (g * 2 * n_half + n_half + hf, prv(i, j), 0)),
            pl.BlockSpec((sph, tb, LANES), lambda i, j, hf: (g * 2 * n_half + n_half + hf, cur(i, j), 0)),
            _const_spec(bias_tiles.shape),
        ],
        out_specs=[pl.BlockSpec((sph, tb, LANES), lambda i, j, hf: (hf, cur(i, j), 0)),
                   pl.BlockSpec((1, tb, LANES), lambda i, j, hf: (hf, cur(i, j), 0))],
        out_shape=[jax.ShapeDtypeStruct((n_half * sph, b * s, LANES), F32),
                   jax.ShapeDtypeStruct((n_half, b * s, LANES), F32)],
        compiler_params=_params(3),
        name=f"attn_prompt_g{g}",
    )(q_sl, kv_sl, kv_sl, kv_sl, kv_sl, bias_tiles)


def _expand_heads(w, e_ref):
    hi = w.astype(BF16)
    lo = (w - hi.astype(F32)).astype(BF16)
    return jnp.dot(jnp.concatenate([hi, lo], axis=1), e_ref[...], preferred_element_type=F32)


def _merged_attention(o_refs, l_refs, e_ref, wo_ref):
    ls = [l_ref[0] + l_ref[1] for l_ref in l_refs]
    mx = jnp.maximum(jnp.maximum(ls[0], ls[1]), ls[2])
    es = [jnp.exp(l - mx) for l in ls]
    tot = es[0] + es[1] + es[2]
    merged = None
    for e, o_ref in zip(es, o_refs):
        o = jnp.concatenate([o_ref[sl] for sl in range(o_ref.shape[0])], axis=1)
        term = _expand_heads(e / tot, e_ref) * o
        merged = term if merged is None else merged + term
    return jnp.dot(merged.astype(BF16), wo_ref[...], preferred_element_type=F32)


def _head_expander():
    head = jnp.arange(2 * LANES, dtype=jnp.int32)[:, None] % LANES
    lane = jnp.arange(ATTN_WIDTH, dtype=jnp.int32)[None, :]
    return (lane // HEAD_DIM == head).astype(BF16)


N_SAMPLE_ATTN_INPUTS = 6 * N_GROUPS
N_SAMPLE_ATTN_OUTPUTS = 1 + N_GROUPS


def _attn_sample_unit(in_refs, out_refs, half):
    (q0_ref, q1_ref, q2_ref, kn0_ref, kn1_ref, kn2_ref, vn0_ref, vn1_ref, vn2_ref, c0_ref, c1_ref, c2_ref,
     bc0_ref, bc1_ref, bc2_ref, bn0_ref, bn1_ref, bn2_ref) = in_refs
    o_ref, n0_ref, n1_ref, n2_ref = out_refs
    t = q0_ref.shape[1]
    lane_head = lax.broadcasted_iota(jnp.int32, (t, HALF), 1) // HEAD_DIM
    is_new = lax.broadcasted_iota(jnp.int32, (HALF, LANES), 1) >= LANES - t
    zpad = jnp.zeros((LANES - t, HALF), F32)
    outs, lses = [], []
    for q_ref, kn_ref, vn_ref, c_ref, bc_ref, bn_ref, n_ref in (
            (q0_ref, kn0_ref, vn0_ref, c0_ref, bc0_ref, bn0_ref, n0_ref),
            (q1_ref, kn1_ref, vn1_ref, c1_ref, bc1_ref, bn1_ref, n1_ref),
            (q2_ref, kn2_ref, vn2_ref, c2_ref, bc2_ref, bn2_ref, n2_ref)):
        w = c_ref.shape[3]
        qs = _stack_heads(q_ref[0], lane_head)
        k_t = c_ref[0, 0]
        v_t = c_ref[0, 1]
        k_new = jnp.concatenate([zpad, kn_ref[0]], axis=0)
        v_new = jnp.concatenate([zpad, vn_ref[0]], axis=0)
        k_new_t = k_new.T
        v_new_t = v_new.T
        s_c = jnp.dot(qs, k_t.astype(BF16), preferred_element_type=F32) + bc_ref[half]
        s_n = jnp.dot(qs, k_new_t.astype(BF16), preferred_element_type=F32) + bn_ref[half]
        m = jnp.maximum(s_c.max(axis=-1, keepdims=True), s_n.max(axis=-1, keepdims=True))
        p_c = jnp.exp(s_c - m)
        p_n = jnp.exp(s_n - m)
        den = p_c.sum(axis=-1, keepdims=True) + p_n.sum(axis=-1, keepdims=True)
        ov = lax.dot_general((p_c / den).astype(BF16), v_t.astype(BF16), (((1,), (1,)), ((), ())),
                             preferred_element_type=F32)
        ov = ov + jnp.dot((p_n / den).astype(BF16), v_new.astype(BF16), preferred_element_type=F32)
        outs.append(ov)
        lses.append(m + jnp.log(den))
        for kv, (old_t, new_t) in enumerate(((k_t, k_new_t), (v_t, v_new_t))):
            rolled = pltpu.roll(old_t, w - t, axis=1)
            if w > LANES:
                n_ref[0, kv, :, 0:w - LANES] = rolled[:, 0:w - LANES]
            n_ref[0, kv, :, w - LANES:w] = jnp.where(is_new, new_t, rolled[:, w - LANES:w])
    mx = jnp.maximum(jnp.maximum(lses[0], lses[1]), lses[2])
    es = [jnp.exp(l - mx) for l in lses]
    tot = es[0] + es[1] + es[2]
    merged = (es[0] / tot) * outs[0] + (es[1] / tot) * outs[1] + (es[2] / tot) * outs[2]
    o_ref[0] = _unstack_heads(merged, lane_head, t).astype(o_ref.dtype)


def _attn_sample_operands(q, kv_new, caches_t, bias_gs):
    b, t, _ = q.shape
    n_half = ATTN_WIDTH // HALF
    tok = lambda col: pl.BlockSpec((1, t, HALF), lambda i, col=col: (i // n_half, 0, col + i % n_half))
    in_specs = [tok(g * n_half) for g in range(N_GROUPS)]
    in_specs += [tok(g * 2 * n_half) for g in range(N_GROUPS)]
    in_specs += [tok(g * 2 * n_half + n_half) for g in range(N_GROUPS)]
    args = [q] * N_GROUPS + [kv_new] * (2 * N_GROUPS)
    cache_spec = lambda g: pl.BlockSpec((1, 2, HALF, GROUPS[g][0]), lambda i: (i // n_half, 0, i % n_half, 0))
    in_specs += [cache_spec(g) for g in range(N_GROUPS)]
    args += list(caches_t)
    tiles = [_sample_bias_tiles(bias_gs[g], g, t) for g in range(N_GROUPS)]
    for part in range(2):
        for g in range(N_GROUPS):
            in_specs.append(_const_spec(tiles[g][part].shape))
            args.append(tiles[g][part])
    out_specs = [pl.BlockSpec((1, t, HALF), lambda i: (i // n_half, 0, i % n_half))]
    out_specs += [cache_spec(g) for g in range(N_GROUPS)]
    out_shape = [jax.ShapeDtypeStruct((b, t, ATTN_WIDTH), BF16)] + [jax.ShapeDtypeStruct(c.shape, F32) for c in caches_t]
    return in_specs, args, out_specs, out_shape


def _out_proj_kernel(x_ref, a_ref, wo_ref, out_ref):
    out_ref[...] = x_ref[...] + jnp.dot(a_ref[...], wo_ref[...], preferred_element_type=F32)


def _out_proj(x, a, w_o):
    n, d = x.shape
    return pl.pallas_call(
        _out_proj_kernel,
        out_shape=jax.ShapeDtypeStruct((n, d), F32),
        compiler_params=pltpu.CompilerParams(vmem_limit_bytes=VMEM_LIMIT),
        name="out_proj",
    )(x, a, w_o)


def _cache_view(c_t, b):
    w = c_t.shape[-1]
    return jnp.transpose(c_t.reshape(b, 2, HEADS_PER_GROUP, HEAD_DIM, w), (0, 4, 1, 2, 3))


def _conv_args(wts):
    return (wts["norms"][0, 1], wts["conv_w_pw1"][0], wts["conv_b_pw1"][0], wts["conv_w_dw"][0], wts["conv_b_dw"][0],
            wts["conv_ln_g"][0], wts["conv_ln_b"][0], wts["conv_w_pw2"][0], wts["conv_b_pw2"][0])


def _sample_until_attention(x, conv_prev, kv_bufs, wts):
    b, t, d = x.shape
    norms, w_in, w_out = wts["norms"], wts["ffn_w_in"], wts["ffn_w_out"]
    xf = _ffn(x.reshape(b * t, d), norms[0, 0], w_in, w_out, (0, 0))
    x3, conv_state = _conv_module(xf.reshape(b, t, d), conv_prev, *_conv_args(wts), min(SAMPLE_CONV_BATCH, b), t)
    xf = _ffn(x3.reshape(b * t, d), norms[0, 2], w_in, w_out, (0, 1))
    kv_new = _proj(xf, wts["norm_kv"], wts["w_kv"], F32).reshape(b, t, -1)
    xf = _ffn(xf, norms[1, 0], w_in, w_out, (1, 0))
    q = _proj(xf, norms[1, 1], wts["attn_w_q"][0], BF16, scale=HEAD_DIM ** -0.5).reshape(b, t, -1)
    caches_t = [jnp.transpose(buf, (0, 2, 3, 4, 1)).reshape(b, 2, ATTN_WIDTH, buf.shape[1]) for buf in kv_bufs]
    return xf, conv_state, (q, kv_new, caches_t, wts["bias_gs"])


def _sample_after_attention(xf, attn, wts):
    xf = _out_proj(xf, attn.reshape(xf.shape[0], ATTN_WIDTH), wts["attn_w_o"][0])
    return _ffn(xf, wts["norms"][1, 2], wts["ffn_w_in"], wts["ffn_w_out"], (1, 1), g_final=wts["norm_final"])


def _prompt_trunk(x, wts, sample_attn):
    b, t, d = x.shape
    norms, w_in, w_out = wts["norms"], wts["ffn_w_in"], wts["ffn_w_out"]
    xf, *sample_res = _ffn(x.reshape(b * t, d), norms[0, 0], w_in, w_out, (0, 0), sample_attn=sample_attn)
    x3, conv_state = _conv_module_pipelined(xf.reshape(b, t, d), *_conv_args(wts))
    xf = _ffn(x3.reshape(b * t, d), norms[0, 2], w_in, w_out, (0, 1))
    shared, *kv_t = _kv_prompt(xf.reshape(b, t, d), wts["norm_kv"], wts["w_kv"])
    q_proj = (norms[1, 1], wts["attn_w_q"][0], HEAD_DIM ** -0.5)
    xf, q_sl = _ffn(xf, norms[1, 0], w_in, w_out, (1, 0), q_proj=q_proj)
    outs, lses = [], []
    for g in range(N_GROUPS):
        o, lse = _attn_prompt(q_sl, shared, _prompt_bias_tiles(wts["bias_gs"][g]), g, b, t)
        outs.append(o)
        lses.append(lse)
    xf = _ffn(xf, norms[1, 2], w_in, w_out, (1, 1), merge=(outs, lses, wts["attn_w_o"][0]),
              g_final=wts["norm_final"])
    return xf.reshape(b, t, d), conv_state, [_cache_view(c, b) for c in kv_t], sample_res


def kernel(x_prompt, x_sample, state_conv, cache_kv_w128, cache_kv_w512, cache_kv_w2048, norms, ffn_w_in, ffn_w_out,
           conv_w_pw1, conv_b_pw1, conv_w_dw, conv_b_dw, conv_ln_g, conv_ln_b, conv_w_pw2, conv_b_pw2, norm_kv, w_kv,
           attn_w_q, attn_w_o, rel_bias, norm_final):
    wts = dict(
        norms=norms, ffn_w_in=ffn_w_in.astype(BF16), ffn_w_out=ffn_w_out.astype(BF16),
        conv_w_pw1=conv_w_pw1.astype(BF16), conv_b_pw1=conv_b_pw1, conv_w_dw=conv_w_dw, conv_b_dw=conv_b_dw,
        conv_ln_g=conv_ln_g, conv_ln_b=conv_ln_b, conv_w_pw2=conv_w_pw2.astype(BF16), conv_b_pw2=conv_b_pw2,
        norm_kv=norm_kv, w_kv=w_kv.astype(BF16), attn_w_q=attn_w_q.astype(BF16), attn_w_o=attn_w_o.astype(BF16),
        rel_bias=rel_bias, norm_final=norm_final)
    wts["bias_gs"] = [_group_bias(rel_bias, g) for g in range(N_GROUPS)]
    bs, ts, d = x_sample.shape
    xs, conv_sample, sample_attn = _sample_until_attention(
        x_sample, state_conv[0], (cache_kv_w128, cache_kv_w512, cache_kv_w2048), wts)
    y_prompt, conv_prompt, kv_prompt, (attn_s, *caches_new) = _prompt_trunk(x_prompt, wts, sample_attn)
    y_sample = _sample_after_attention(xs, attn_s, wts).reshape(bs, ts, d)
    kv_sample = [_cache_view(c, bs) for c in caches_new]
    return (y_prompt, y_sample, conv_prompt[None], conv_sample[None], kv_prompt[0], kv_sample[0],
            kv_prompt[1], kv_sample[1], kv_prompt[2], kv_sample[2])
```

```python
import functools
import math

import jax
import jax.numpy as jnp
from jax import lax
from jax.experimental import pallas as pl
from jax.experimental.pallas import tpu as pltpu

F32 = jnp.float32
BF16 = jnp.bfloat16

EPS = 1e-6
NEG_INF = -1e30
GROUPS = ((128, 1), (512, 4), (2048, 16))
N_GROUPS = len(GROUPS)
HEADS_PER_GROUP = 8
HEAD_DIM = 64
ATTN_WIDTH = HEADS_PER_GROUP * HEAD_DIM
ROW_WIDTH = 2 * ATTN_WIDTH
N_KEYS = 129
N_BUCKETS = 32
MAX_DISTANCE = 2048
PAST_LEN = 8192
CONV_WIDTH = 31
CONV_HIST = 32
LANES = 128
SUBLANES = 8
HALF = 256
HEADS_PER_HALF = HALF // HEAD_DIM
QBLOCK = 128
ATTN_QUERIES = (4096, 1024, 256)
TOKEN_TILE = 512
PLAIN_FFN_TILE = 1024
SAMPLE_CONV_BATCH = 16
CONV_ROWS = 64
VMEM_LIMIT = 56 * 1024 * 1024


def _params(n_axes):
    return pltpu.CompilerParams(dimension_semantics=("arbitrary",) * n_axes, vmem_limit_bytes=VMEM_LIMIT)


def _const_spec(shape):
    zeros = (0,) * len(shape)
    return pl.BlockSpec(shape, lambda *_: zeros, pipeline_mode=pl.Buffered(1))


def _rms(x, g):
    return x * lax.rsqrt(jnp.mean(x * x, axis=-1, keepdims=True) + EPS) * g


def _silu(x):
    return x * jax.nn.sigmoid(x)


def _ffn_kernel(*refs, d_ff, n_chunk, has_merge, has_final, q_scale, has_sample_attn):
    refs = list(refs)
    x_ref = refs.pop(0)
    if has_merge:
        attn_refs = [refs.pop(0) for _ in range(2 * N_GROUPS)]
        e_ref, wo_ref = refs.pop(0), refs.pop(0)
    g_ref, win_ref, wout_ref = refs.pop(0), refs.pop(0), refs.pop(0)
    if has_final:
        gf_ref = refs.pop(0)
    if q_scale is not None:
        gq_ref, wq_ref = refs.pop(0), refs.pop(0)
    if has_sample_attn:
        sample_in = [refs.pop(0) for _ in range(N_SAMPLE_ATTN_INPUTS)]
    o_ref = refs.pop(0)
    if q_scale is not None:
        q_ref = refs.pop(0)
    if has_sample_attn:
        sample_out = [refs.pop(0) for _ in range(N_SAMPLE_ATTN_OUTPUTS)]
    act_ref = refs.pop(0)

    if has_sample_attn:
        _attn_sample_unit(sample_in, sample_out, pl.program_id(0) % (ATTN_WIDTH // HALF))

    x = x_ref[...]
    if has_merge:
        x = x + _merged_attention(attn_refs[:N_GROUPS], attn_refs[N_GROUPS:], e_ref, wo_ref)
    h = _rms(x, g_ref[...]).astype(BF16)
    for c in range(d_ff // n_chunk):
        lo = c * n_chunk
        gate = jnp.dot(h, win_ref[:, lo:lo + n_chunk], preferred_element_type=F32)
        up = jnp.dot(h, win_ref[:, d_ff + lo:d_ff + lo + n_chunk], preferred_element_type=F32)
        act_ref[:, lo:lo + n_chunk] = (_silu(gate) * up).astype(BF16)
    y = jnp.dot(act_ref[...], wout_ref[...], preferred_element_type=F32)
    o = x + 0.5 * y
    if q_scale is not None:
        hq = _rms(o, gq_ref[...]).astype(BF16)
        _store_slabs(q_ref, jnp.dot(hq, wq_ref[...], preferred_element_type=F32) * q_scale)
    if has_final:
        o = _rms(o, gf_ref[...])
    o_ref[...] = o


def _ffn(x, g, w_in, w_out, idx, *, merge=None, g_final=None, q_proj=None, sample_attn=None):
    n, d = x.shape
    d_ff = w_out.shape[-2]
    plain = merge is None and q_proj is None and sample_attn is None
    if plain and n <= TOKEN_TILE:
        return _ffn_single_tile(x, g, w_in, w_out, idx, g_final)
    tm = min(PLAIN_FFN_TILE if plain else TOKEN_TILE, n)
    if sample_attn is not None:
        n_units = sample_attn[0].shape[0] * (ATTN_WIDTH // HALF)
        tm = n // n_units
        assert tm * n_units == n and tm % SUBLANES == 0, (n, n_units)
    tok = lambda width: pl.BlockSpec((tm, width), lambda i: (i, 0))
    slabs = lambda count: pl.BlockSpec((count, tm, LANES), lambda i: (0, i, 0))
    stacked = lambda rows, cols: pl.BlockSpec((None, None, rows, cols), lambda i: (*idx, 0, 0),
                                              pipeline_mode=pl.Buffered(1))
    in_specs = [tok(d)]
    args = [x]
    if merge is not None:
        outs, lses, w_o = merge
        in_specs += [slabs(ATTN_WIDTH // LANES)] * N_GROUPS + [slabs(ATTN_WIDTH // HALF)] * N_GROUPS
        in_specs += [_const_spec((2 * LANES, ATTN_WIDTH)), _const_spec((ATTN_WIDTH, d))]
        args += [*outs, *lses, _head_expander(), w_o]
    in_specs += [_const_spec((1, d)), stacked(d, 2 * d_ff), stacked(d_ff, d)]
    args += [g.reshape(1, d), w_in, w_out]
    if g_final is not None:
        in_specs.append(_const_spec((1, d)))
        args.append(g_final.reshape(1, d))
    out_specs = [tok(d)]
    out_shape = [jax.ShapeDtypeStruct((n, d), F32)]
    q_scale = None
    if q_proj is not None:
        g_q, w_q, q_scale = q_proj
        n_q = w_q.shape[1]
        in_specs += [_const_spec((1, d)), _const_spec((d, n_q))]
        args += [g_q.reshape(1, d), w_q]
        out_specs.append(slabs(n_q // LANES))
        out_shape.append(jax.ShapeDtypeStruct((n_q // LANES, n, LANES), F32))
    if sample_attn is not None:
        s_in_specs, s_args, s_out_specs, s_out_shape = _attn_sample_operands(*sample_attn)
        in_specs += s_in_specs
        args += s_args
        out_specs += s_out_specs
        out_shape += s_out_shape
    res = pl.pallas_call(
        functools.partial(_ffn_kernel, d_ff=d_ff, n_chunk=HALF, has_merge=merge is not None,
                          has_final=g_final is not None, q_scale=q_scale, has_sample_attn=sample_attn is not None),
        grid=(n // tm,),
        in_specs=in_specs,
        out_specs=out_specs,
        out_shape=out_shape,
        scratch_shapes=[pltpu.VMEM((tm, d_ff), BF16)],
        compiler_params=_params(1),
        name="ffn",
    )(*args)
    return res[0] if len(res) == 1 else res


def _ffn_single_tile_kernel(*refs, has_final):
    if has_final:
        x_ref, g_ref, wg_ref, wu_ref, wo_ref, gf_ref, o_ref, h_ref, acc_ref = refs
    else:
        x_ref, g_ref, wg_ref, wu_ref, wo_ref, o_ref, h_ref, acc_ref = refs
    c = pl.program_id(0)

    @pl.when(c == 0)
    def _():
        h_ref[...] = _rms(x_ref[...], g_ref[...]).astype(BF16)

    h = h_ref[...]
    gate = jnp.dot(h, wg_ref[...], preferred_element_type=F32)
    up = jnp.dot(h, wu_ref[...], preferred_element_type=F32)
    part = jnp.dot((_silu(gate) * up).astype(BF16), wo_ref[...], preferred_element_type=F32)

    @pl.when(c == 0)
    def _():
        acc_ref[...] = part

    @pl.when(c == pl.num_programs(0) - 1)
    def _():
        o = x_ref[...] + 0.5 * (acc_ref[...] + part)
        if has_final:
            o = _rms(o, gf_ref[...])
        o_ref[...] = o


def _ffn_single_tile(x, g, w_in, w_out, idx, g_final=None):
    n, d = x.shape
    d_ff = w_out.shape[-2]
    n_c = 2
    dc = d_ff // n_c
    has_final = g_final is not None
    whole = lambda shape: pl.BlockSpec(shape, lambda c: (0,) * len(shape))
    in_specs = [whole((n, d)), whole((1, d)),
                pl.BlockSpec((None, None, d, dc), lambda c: (*idx, 0, c)),
                pl.BlockSpec((None, None, d, dc), lambda c: (*idx, 0, n_c + c)),
                pl.BlockSpec((None, None, dc, d), lambda c: (*idx, c, 0))]
    args = [x, g.reshape(1, d), w_in, w_in, w_out]
    if has_final:
        in_specs.append(whole((1, d)))
        args.append(g_final.reshape(1, d))
    return pl.pallas_call(
        functools.partial(_ffn_single_tile_kernel, has_final=has_final),
        grid=(n_c,),
        in_specs=in_specs,
        out_specs=whole((n, d)),
        out_shape=jax.ShapeDtypeStruct((n, d), F32),
        scratch_shapes=[pltpu.VMEM((n, d), BF16), pltpu.VMEM((n, d), F32)],
        compiler_params=_params(1),
        name="ffn_single_tile",
    )(*args)


def _proj_kernel(x_ref, g_ref, w_ref, o_ref, *, scale):
    h = _rms(x_ref[...], g_ref[...]).astype(BF16)
    y = jnp.dot(h, w_ref[...], preferred_element_type=F32)
    if scale != 1.0:
        y = y * scale
    o_ref[...] = y.astype(o_ref.dtype)


def _proj(x, g, w, out_dtype, scale=1.0):
    n, d = x.shape
    n_out = w.shape[1]
    tm = min(TOKEN_TILE, n)
    return pl.pallas_call(
        functools.partial(_proj_kernel, scale=scale),
        grid=(n // tm,),
        in_specs=[pl.BlockSpec((tm, d), lambda i: (i, 0)), _const_spec((1, d)), _const_spec((d, n_out))],
        out_specs=pl.BlockSpec((tm, n_out), lambda i: (i, 0)),
        out_shape=jax.ShapeDtypeStruct((n, n_out), out_dtype),
        compiler_params=_params(1),
        name="proj",
    )(x, g.reshape(1, d), w)


def _store_slabs(slab_ref, y):
    for sl in range(slab_ref.shape[0]):
        slab_ref[sl] = y[:, sl * LANES:(sl + 1) * LANES]


def _kv_prompt_kernel(x_ref, g_ref, w_ref, sh_ref, t0_ref, t1_ref, t2_ref, *, n_t, tm):
    h = _rms(x_ref[0], g_ref[...]).astype(BF16)
    y = jnp.dot(h, w_ref[...], preferred_element_type=F32)
    _store_slabs(sh_ref, y)
    for g, t_ref in enumerate((t0_ref, t1_ref, t2_ref)):
        rows = min(GROUPS[g][0], n_t * tm, tm)
        t_ref[0] = y[tm - rows:, g * ROW_WIDTH:(g + 1) * ROW_WIDTH].T


def _kv_prompt(x, g, w):
    b, s, d = x.shape
    n_out = w.shape[1]
    tm = min(TOKEN_TILE, s)
    n_t = s // tm
    out_specs = [pl.BlockSpec((n_out // LANES, tm, LANES), lambda i, j: (0, i * n_t + j, 0))]
    out_shape = [jax.ShapeDtypeStruct((n_out // LANES, b * s, LANES), F32)]
    for grp in range(N_GROUPS):
        keep = min(GROUPS[grp][0], s)
        first = n_t - max(keep // tm, 1)
        out_specs.append(pl.BlockSpec((1, ROW_WIDTH, min(keep, tm)),
                                      lambda i, j, first=first: (i, 0, jnp.maximum(j - first, 0))))
        out_shape.append(jax.ShapeDtypeStruct((b, ROW_WIDTH, keep), F32))
    return pl.pallas_call(
        functools.partial(_kv_prompt_kernel, n_t=n_t, tm=tm),
        grid=(b, n_t),
        in_specs=[pl.BlockSpec((1, tm, d), lambda i, j: (i, j, 0)), _const_spec((1, d)), _const_spec((d, n_out))],
        out_specs=out_specs,
        out_shape=out_shape,
        compiler_params=_params(2),
        name="kv_prompt",
    )(x, g.reshape(1, d), w)


def _conv_kernel(x_ref, prev_ref, g_ref, w1_ref, b1_ref, wdw_ref, bdw_ref, lng_ref, lnb_ref, w2_ref, b2_ref,
                 o_ref, st_ref, uext_ref, ush_ref, conv_ref, *, bb, ts, n_t, rc):
    it = pl.program_id(1)
    d = x_ref.shape[-1]
    c_in = w2_ref.shape[0]
    x = x_ref[...].reshape(bb * ts, d)
    h = _rms(x, g_ref[...]).astype(BF16)
    ag = jnp.dot(h, w1_ref[...], preferred_element_type=F32) + b1_ref[...]
    u = ag[:, :c_in] * jax.nn.sigmoid(ag[:, c_in:])

    @pl.when(it == 0)
    def _():
        uext_ref[:, 0:CONV_HIST, :] = prev_ref[...]

    uext_ref[:, CONV_HIST:CONV_HIST + ts, :] = u.reshape(bb, ts, c_in)

    lead = CONV_HIST - (CONV_WIDTH - 1)
    n_lb = c_in // LANES
    for s in range(SUBLANES):
        n_sh = ts + SUBLANES * ((CONV_WIDTH - 1 - s) // SUBLANES)
        for lb in range(n_lb):
            ush_ref[s, lb, :, 0:n_sh, :] = uext_ref[:, lead + s:lead + s + n_sh, lb * LANES:(lb + 1) * LANES]

    reps = rc // SUBLANES
    for lb in range(n_lb):
        lanes = slice(lb * LANES, (lb + 1) * LANES)
        taps = [wdw_ref[k, :, lanes] for k in range(CONV_WIDTH)]
        bias = bdw_ref[:, lanes]

        def chunk(ci, carry, lb=lb, lanes=lanes, taps=taps, bias=bias):
            r0 = pl.multiple_of(ci * rc, rc)
            acc = jnp.broadcast_to(bias.reshape(1, 1, LANES), (bb, rc, LANES))
            for k in range(CONV_WIDTH):
                win = ush_ref[k % SUBLANES, lb, :, pl.ds(r0 + SUBLANES * (k // SUBLANES), rc), :]
                wk = jnp.concatenate([taps[k]] * reps, axis=0) if reps > 1 else taps[k]
                acc = acc + win * wk[None]
            conv_ref[:, pl.ds(r0, rc), lanes] = acc
            return carry

        lax.fori_loop(0, ts // rc, chunk, 0)

    acc = conv_ref[...].reshape(bb * ts, c_in)
    mu = jnp.mean(acc, axis=-1, keepdims=True)
    cen = acc - mu
    var = jnp.mean(cen * cen, axis=-1, keepdims=True)
    c = _silu(cen * lax.rsqrt(var + EPS) * lng_ref[...] + lnb_ref[...]).astype(BF16)
    y = jnp.dot(c, w2_ref[...], preferred_element_type=F32) + b2_ref[...]
    o_ref[...] = (x + y).reshape(bb, ts, d)

    @pl.when(it == n_t - 1)
    def _():
        st_ref[...] = uext_ref[:, ts:ts + CONV_HIST, :]

    if n_t > 1:
        uext_ref[:, 0:CONV_HIST, :] = uext_ref[:, ts:ts + CONV_HIST, :]


def _conv_module(x, prev, g, w1, b1, wdw, bdw, lng, lnb, w2, b2, bb, ts):
    b, t, d = x.shape
    c_in = w2.shape[0]
    n_t = t // ts
    rc = min(ts, CONV_ROWS)
    row = lambda v: v.reshape(1, -1)
    in_specs = [pl.BlockSpec((bb, ts, d), lambda i, j: (i, j, 0)),
                pl.BlockSpec((bb, CONV_HIST, c_in), lambda i, j: (i, 0, 0)),
                _const_spec((1, d)), _const_spec((d, 2 * c_in)), _const_spec((1, 2 * c_in)),
                _const_spec((CONV_WIDTH, SUBLANES, c_in)), _const_spec((1, c_in)), _const_spec((1, c_in)),
                _const_spec((1, c_in)), _const_spec((c_in, d)), _const_spec((1, d))]
    wdw_rep = jnp.broadcast_to(wdw[:, None, :], (CONV_WIDTH, SUBLANES, c_in))
    prev_rows = jnp.pad(prev, ((0, 0), (CONV_HIST - prev.shape[1], 0), (0, 0)))
    args = [x, prev_rows, row(g), w1, row(b1), wdw_rep, row(bdw), row(lng), row(lnb), w2, row(b2)]
    out, st = pl.pallas_call(
        functools.partial(_conv_kernel, bb=bb, ts=ts, n_t=n_t, rc=rc),
        grid=(b // bb, n_t),
        in_specs=in_specs,
        out_specs=[pl.BlockSpec((bb, ts, d), lambda i, j: (i, j, 0)),
                   pl.BlockSpec((bb, CONV_HIST, c_in), lambda i, j: (i, 0, 0))],
        out_shape=[jax.ShapeDtypeStruct((b, t, d), F32), jax.ShapeDtypeStruct((b, CONV_HIST, c_in), F32)],
        scratch_shapes=[pltpu.VMEM((bb, ts + CONV_HIST, c_in), F32),
                        pltpu.VMEM((SUBLANES, c_in // LANES, bb, ts + 24, LANES), F32),
                        pltpu.VMEM((bb, ts, c_in), F32)],
        compiler_params=_params(2),
        name="conv_module",
    )(*args)
    return out, st[:, CONV_HIST - (CONV_WIDTH - 1):]


def _conv_pipe_kernel(xc_ref, xp_ref, g_ref, w1_ref, b1_ref, wdw_ref, bdw_ref, lng_ref, lnb_ref, w2_ref, b2_ref,
                      o_ref, st_ref, h_ref, u0_ref, u1_ref, ush_ref, conv_ref, *, ts, n_t, rc):
    i = pl.program_id(0)
    n_lb = conv_ref.shape[0]
    lead = CONV_HIST - (CONV_WIDTH - 1)
    reps = rc // SUBLANES

    @pl.when(i == 0)
    def _():
        u1_ref[...] = jnp.zeros(u1_ref.shape, F32)

    def step(ua_ref, ub_ref):
        starts_sequence = (i % n_t) == 0
        h_ref[...] = _rms(xc_ref[...], g_ref[...]).astype(BF16)
        for lb in range(n_lb):
            lanes = slice(lb * LANES, (lb + 1) * LANES)
            ag = jnp.dot(h_ref[...], w1_ref[lb], preferred_element_type=F32) + b1_ref[lb]
            u = ag[:, :LANES] * jax.nn.sigmoid(ag[:, LANES:])
            ua_ref[lb, 0:CONV_HIST, :] = jnp.where(starts_sequence, 0.0, ub_ref[lb, ts:ts + CONV_HIST, :])
            ua_ref[lb, CONV_HIST:CONV_HIST + ts, :] = u
            buf = lb % 2
            for s in range(SUBLANES):
                n_sh = ts + SUBLANES * ((CONV_WIDTH - 1 - s) // SUBLANES)
                ush_ref[buf, s, 0:n_sh, :] = ub_ref[lb, lead + s:lead + s + n_sh, :]
            taps = [wdw_ref[k, :, lanes] for k in range(CONV_WIDTH)]
            bias = jnp.broadcast_to(bdw_ref[:, lanes], (rc, LANES))
            for ci in range(ts // rc):
                r0 = ci * rc
                acc = bias
                for k in range(CONV_WIDTH):
                    lo = r0 + SUBLANES * (k // SUBLANES)
                    wk = jnp.concatenate([taps[k]] * reps, axis=0) if reps > 1 else taps[k]
                    acc = acc + ush_ref[buf, k % SUBLANES, lo:lo + rc, :] * wk
                conv_ref[lb, r0:r0 + rc, :] = acc

        acc = jnp.concatenate([conv_ref[lb] for lb in range(n_lb)], axis=1)
        mu = jnp.mean(acc, axis=-1, keepdims=True)
        cen = acc - mu
        var = jnp.mean(cen * cen, axis=-1, keepdims=True)
        c = _silu(cen * lax.rsqrt(var + EPS) * lng_ref[...] + lnb_ref[...]).astype(BF16)
        y = jnp.dot(c, w2_ref[...], preferred_element_type=F32) + b2_ref[...]
        o_ref[...] = xp_ref[...] + y

        @pl.when((i % n_t) == n_t - 1)
        def _():
            for lb in range(n_lb):
                st_ref[0, :, lb * LANES:(lb + 1) * LANES] = ua_ref[lb, ts:ts + CONV_HIST, :]

    @pl.when(i % 2 == 0)
    def _():
        step(u0_ref, u1_ref)

    @pl.when(i % 2 == 1)
    def _():
        step(u1_ref, u0_ref)


def _conv_module_pipelined(x, g, w1, b1, wdw, bdw, lng, lnb, w2, b2):
    b, t, d = x.shape
    c_in = w2.shape[0]
    ts = TOKEN_TILE
    n_t = t // ts
    n_tiles = b * n_t
    n_lb = c_in // LANES
    row = lambda v: v.reshape(1, -1)
    cols = lambda v, lb: v[..., lb * LANES:(lb + 1) * LANES]
    w1_blocks = jnp.stack([jnp.concatenate([cols(w1, lb), cols(w1, n_lb + lb)], axis=-1) for lb in range(n_lb)])
    b1_blocks = jnp.stack([jnp.concatenate([cols(row(b1), lb), cols(row(b1), n_lb + lb)], axis=-1)
                           for lb in range(n_lb)])
    wdw_rep = jnp.broadcast_to(wdw[:, None, :], (CONV_WIDTH, SUBLANES, c_in))
    last = n_tiles - 1
    out, st = pl.pallas_call(
        functools.partial(_conv_pipe_kernel, ts=ts, n_t=n_t, rc=CONV_ROWS // 2),
        grid=(n_tiles + 1,),
        in_specs=[pl.BlockSpec((ts, d), lambda i: (jnp.minimum(i, last), 0)),
                  pl.BlockSpec((ts, d), lambda i: (jnp.maximum(i - 1, 0), 0)),
                  _const_spec((1, d)), _const_spec((n_lb, d, 2 * LANES)), _const_spec((n_lb, 1, 2 * LANES)),
                  _const_spec((CONV_WIDTH, SUBLANES, c_in)), _const_spec((1, c_in)), _const_spec((1, c_in)),
                  _const_spec((1, c_in)), _const_spec((c_in, d)), _const_spec((1, d))],
        out_specs=[pl.BlockSpec((ts, d), lambda i: (jnp.maximum(i - 1, 0), 0)),
                   pl.BlockSpec((1, CONV_HIST, c_in), lambda i: (jnp.minimum(i, last) // n_t, 0, 0))],
        out_shape=[jax.ShapeDtypeStruct((b * t, d), F32), jax.ShapeDtypeStruct((b, CONV_HIST, c_in), F32)],
        scratch_shapes=[pltpu.VMEM((ts, d), BF16),
                        pltpu.VMEM((n_lb, ts + CONV_HIST, LANES), F32),
                        pltpu.VMEM((n_lb, ts + CONV_HIST, LANES), F32),
                        pltpu.VMEM((2, SUBLANES, ts + 24, LANES), F32),
                        pltpu.VMEM((n_lb, ts, LANES), F32)],
        compiler_params=_params(1),
        name="conv_pipelined",
    )(x.reshape(b * t, d), x.reshape(b * t, d), row(g), w1_blocks, b1_blocks, wdw_rep, row(bdw), row(lng), row(lnb),
      w2, row(b2))
    return out.reshape(b, t, d), st[:, CONV_HIST - (CONV_WIDTH - 1):]


def _t5_bucket(dist):
    max_exact = N_BUCKETS // 2
    d_f = jnp.maximum(dist, 1).astype(F32)
    large = max_exact + (jnp.log(d_f / max_exact) / math.log(MAX_DISTANCE / max_exact)
                         * (N_BUCKETS - max_exact)).astype(jnp.int32)
    large = jnp.minimum(large, N_BUCKETS - 1)
    return jnp.where(dist < max_exact, dist, large)


def _group_bias(rel_bias, g):
    _, d = GROUPS[g]
    dist = d * jnp.arange(N_KEYS, dtype=jnp.int32)
    b = jnp.take(rel_bias, _t5_bucket(dist), axis=0)
    return b[:, g * HEADS_PER_GROUP:(g + 1) * HEADS_PER_GROUP].T.astype(F32)


def _toeplitz(p, n_rows, n_cols):
    length = p.shape[1]
    stride = length - 1
    flat = jnp.tile(p, (1, n_rows))[:, :n_rows * stride]
    return flat.reshape(p.shape[0], n_rows, stride)[:, :, :n_cols]


def _stack_rows(t):
    return t.reshape(2, HEADS_PER_HALF * t.shape[1], t.shape[2])


def _prompt_bias_tiles(bias_g):
    neg = jnp.full((bias_g.shape[0], 2 * QBLOCK + 1 - N_KEYS), NEG_INF, F32)
    p = jnp.concatenate([bias_g[:, ::-1], neg], axis=1)
    regular = _toeplitz(p, QBLOCK, 2 * QBLOCK)
    c = jnp.arange(2 * QBLOCK, dtype=jnp.int32)[None, None, :]
    first = jnp.where(c >= QBLOCK, regular, NEG_INF)
    return jnp.stack([_stack_rows(regular), _stack_rows(first)], axis=0)


def _sample_bias_tiles(bias_g, g, t_new):
    w, d = GROUPS[g]
    n_heads = bias_g.shape[0]
    vals = bias_g[:, N_KEYS - 1:0:-1]
    strided = jnp.concatenate([vals[:, :, None], jnp.full((n_heads, N_KEYS - 1, d - 1), NEG_INF, F32)], axis=2)
    p = jnp.concatenate([strided.reshape(n_heads, w), jnp.full((n_heads, t_new), NEG_INF, F32)], axis=1)
    cached = _toeplitz(p, t_new, w)
    row = jnp.arange(w, dtype=jnp.int32)[None, None, :]
    min_valid = 2 * w - PAST_LEN
    cached = jnp.where(w + row >= min_valid, cached, NEG_INF)
    t = jnp.arange(t_new, dtype=jnp.int32)[None, :, None]
    i = jnp.arange(LANES, dtype=jnp.int32)[None, None, :] - (LANES - t_new)
    dist = t - i
    new = jnp.full((n_heads, t_new, LANES), NEG_INF, F32)
    for j in range((t_new - 1) // d + 1):
        new = jnp.where((i >= 0) & (dist == d * j), bias_g[:, j][:, None, None], new)
    return _stack_rows(cached), _stack_rows(new)


def _stack_heads(x, lane_head):
    return jnp.concatenate([jnp.where(lane_head == h, x, jnp.zeros_like(x)) for h in range(HEADS_PER_HALF)], axis=0)


def _unstack_heads(x, lane_head, q):
    out = jnp.where(lane_head == 0, x[0:q], 0.0)
    for h in range(1, HEADS_PER_HALF):
        out = out + jnp.where(lane_head == h, x[h * q:(h + 1) * q], 0.0)
    return out


def _residue_rows(start, d):
    return pl.ds(start, QBLOCK, stride=d) if d > 1 else pl.ds(start, QBLOCK)


def _gather_rows(ref, start, d):
    return jnp.concatenate([ref[sl, _residue_rows(start, d), :] for sl in range(ref.shape[0])], axis=1)


def _attn_prompt_kernel(q_ref, kp_ref, kc_ref, vp_ref, vc_ref, bias_ref, o_ref, lse_ref, *, d, n_sub):
    qb = QBLOCK
    j = pl.program_id(1)
    half = pl.program_id(2)
    lane_head = lax.broadcasted_iota(jnp.int32, (qb, HALF), 1) // HEAD_DIM
    lane = lax.broadcasted_iota(jnp.int32, (qb, LANES), 1)
    first = jnp.where(j == 0, 1, 0)
    for r in range(d):
        k_prev = _gather_rows(kp_ref, r, d).astype(BF16)
        v_prev = _gather_rows(vp_ref, r, d).astype(BF16)
        for sub in range(n_sub):
            start = r + d * sub * qb
            q = _gather_rows(q_ref, start, d).astype(BF16)
            k_cur = _gather_rows(kc_ref, start, d).astype(BF16)
            v_cur = _gather_rows(vc_ref, start, d).astype(BF16)
            k = jnp.concatenate([k_prev, k_cur], axis=0)
            v = jnp.concatenate([v_prev, v_cur], axis=0)
            qs = _stack_heads(q, lane_head)
            s = lax.dot_general(qs, k, (((1,), (1,)), ((), ())), preferred_element_type=F32)
            s = s + (bias_ref[first, half] if sub == 0 else bias_ref[0, half])
            m = jnp.max(s, axis=-1, keepdims=True)
            p = jnp.exp(s - m)
            den = jnp.sum(p, axis=-1, keepdims=True)
            ov = jnp.dot((p / den).astype(BF16), v, preferred_element_type=F32)
            o = _unstack_heads(ov, lane_head, qb)
            rows = _residue_rows(start, d)
            for sl in range(o_ref.shape[0]):
                o_ref[sl, rows, :] = o[:, sl * LANES:(sl + 1) * LANES]
            lse = m + jnp.log(den)
            lse_tile = jnp.zeros((qb, LANES), F32)
            for h in range(HEADS_PER_HALF):
                lse_tile = jnp.where(lane == half * HEADS_PER_HALF + h, lse[h * qb:(h + 1) * qb], lse_tile)
            lse_ref[0, rows, :] = lse_tile
            k_prev, v_prev = k_cur, v_cur


def _attn_prompt(q_sl, kv_sl, bias_tiles, g, b, s):
    w, d = GROUPS[g]
    tb = d * ATTN_QUERIES[g]
    pb = d * QBLOCK
    n_tb = s // tb
    sph = HALF // LANES
    n_half = ATTN_WIDTH // HALF
    cur = lambda i, j: i * n_tb + j
    prv = lambda i, j: jnp.maximum(i * (s // pb) + j * (tb // pb) - 1, 0)
    return pl.pallas_call(
        functools.partial(_attn_prompt_kernel, d=d, n_sub=ATTN_QUERIES[g] // QBLOCK),
        grid=(b, n_tb, n_half),
        in_specs=[
            pl.BlockSpec((sph, tb, LANES), lambda i, j, hf: (g * n_half + hf, cur(i, j), 0)),
            pl.BlockSpec((sph, pb, LANES), lambda i, j, hf: (g * 2 * n_half + hf, prv(i, j), 0)),
            pl.BlockSpec((sph, tb, LANES), lambda i, j, hf: (g * 2 * n_half + hf, cur(i, j), 0)),
            pl.BlockSpec((sph, pb, LANES), lambda i, j, hf: (g * 2 * n_half + n_half + hf, prv(i, j), 0)),
            pl.BlockSpec((sph, tb, LANES), lambda i, j, hf: (g * 2 * n_half + n_half + hf, cur(i, j), 0)),
            _const_spec(bias_tiles.shape),
        ],
        out_specs=[pl.BlockSpec((sph, tb, LANES), lambda i, j, hf: (hf, cur(i, j), 0)),
                   pl.BlockSpec((1, tb, LANES), lambda i, j, hf: (hf, cur(i, j), 0))],
        out_shape=[jax.ShapeDtypeStruct((n_half * sph, b * s, LANES), F32),
                   jax.ShapeDtypeStruct((n_half, b * s, LANES), F32)],
        compiler_params=_params(3),
        name=f"attn_prompt_g{g}",
    )(q_sl, kv_sl, kv_sl, kv_sl, kv_sl, bias_tiles)


def _expand_heads(w, e_ref):
    hi = w.astype(BF16)
    lo = (w - hi.astype(F32)).astype(BF16)
    return jnp.dot(jnp.concatenate([hi, lo], axis=1), e_ref[...], preferred_element_type=F32)


def _merged_attention(o_refs, l_refs, e_ref, wo_ref):
    ls = [l_ref[0] + l_ref[1] for l_ref in l_refs]
    mx = jnp.maximum(jnp.maximum(ls[0], ls[1]), ls[2])
    es = [jnp.exp(l - mx) for l in ls]
    tot = es[0] + es[1] + es[2]
    merged = None
    for e, o_ref in zip(es, o_refs):
        o = jnp.concatenate([o_ref[sl] for sl in range(o_ref.shape[0])], axis=1)
        term = _expand_heads(e / tot, e_ref) * o
        merged = term if merged is None else merged + term
    return jnp.dot(merged.astype(BF16), wo_ref[...], preferred_element_type=F32)


def _head_expander():
    head = jnp.arange(2 * LANES, dtype=jnp.int32)[:, None] % LANES
    lane = jnp.arange(ATTN_WIDTH, dtype=jnp.int32)[None, :]
    return (lane // HEAD_DIM == head).astype(BF16)


N_SAMPLE_ATTN_INPUTS = 6 * N_GROUPS
N_SAMPLE_ATTN_OUTPUTS = 1 + N_GROUPS


def _attn_sample_unit(in_refs, out_refs, half):
    (q0_ref, q1_ref, q2_ref, kn0_ref, kn1_ref, kn2_ref, vn0_ref, vn1_ref, vn2_ref, c0_ref, c1_ref, c2_ref,
     bc0_ref, bc1_ref, bc2_ref, bn0_ref, bn1_ref, bn2_ref) = in_refs
    o_ref, n0_ref, n1_ref, n2_ref = out_refs
    t = q0_ref.shape[1]
    lane_head = lax.broadcasted_iota(jnp.int32, (t, HALF), 1) // HEAD_DIM
    is_new = lax.broadcasted_iota(jnp.int32, (HALF, LANES), 1) >= LANES - t
    zpad = jnp.zeros((LANES - t, HALF), F32)
    outs, lses = [], []
    for q_ref, kn_ref, vn_ref, c_ref, bc_ref, bn_ref, n_ref in (
            (q0_ref, kn0_ref, vn0_ref, c0_ref, bc0_ref, bn0_ref, n0_ref),
            (q1_ref, kn1_ref, vn1_ref, c1_ref, bc1_ref, bn1_ref, n1_ref),
            (q2_ref, kn2_ref, vn2_ref, c2_ref, bc2_ref, bn2_ref, n2_ref)):
        w = c_ref.shape[3]
        qs = _stack_heads(q_ref[0], lane_head)
        k_t = c_ref[0, 0]
        v_t = c_ref[0, 1]
        k_new = jnp.concatenate([zpad, kn_ref[0]], axis=0)
        v_new = jnp.concatenate([zpad, vn_ref[0]], axis=0)
        k_new_t = k_new.T
        v_new_t = v_new.T
        s_c = jnp.dot(qs, k_t.astype(BF16), preferred_element_type=F32) + bc_ref[half]
        s_n = jnp.dot(qs, k_new_t.astype(BF16), preferred_element_type=F32) + bn_ref[half]
        m = jnp.maximum(s_c.max(axis=-1, keepdims=True), s_n.max(axis=-1, keepdims=True))
        p_c = jnp.exp(s_c - m)
        p_n = jnp.exp(s_n - m)
        den = p_c.sum(axis=-1, keepdims=True) + p_n.sum(axis=-1, keepdims=True)
        ov = lax.dot_general((p_c / den).astype(BF16), v_t.astype(BF16), (((1,), (1,)), ((), ())),
                             preferred_element_type=F32)
        ov = ov + jnp.dot((p_n / den).astype(BF16), v_new.astype(BF16), preferred_element_type=F32)
        outs.append(ov)
        lses.append(m + jnp.log(den))
        for kv, (old_t, new_t) in enumerate(((k_t, k_new_t), (v_t, v_new_t))):
            rolled = pltpu.roll(old_t, w - t, axis=1)
            if w > LANES:
                n_ref[0, kv, :, 0:w - LANES] = rolled[:, 0:w - LANES]
            n_ref[0, kv, :, w - LANES:w] = jnp.where(is_new, new_t, rolled[:, w - LANES:w])
    mx = jnp.maximum(jnp.maximum(lses[0], lses[1]), lses[2])
    es = [jnp.exp(l - mx) for l in lses]
    tot = es[0] + es[1] + es[2]
    merged = (es[0] / tot) * outs[0] + (es[1] / tot) * outs[1] + (es[2] / tot) * outs[2]
    o_ref[0] = _unstack_heads(merged, lane_head, t).astype(o_ref.dtype)


def _attn_sample_operands(q, kv_new, caches_t, bias_gs):
    b, t, _ = q.shape
    n_half = ATTN_WIDTH // HALF
    tok = lambda col: pl.BlockSpec((1, t, HALF), lambda i, col=col: (i // n_half, 0, col + i % n_half))
    in_specs = [tok(g * n_half) for g in range(N_GROUPS)]
    in_specs += [tok(g * 2 * n_half) for g in range(N_GROUPS)]
    in_specs += [tok(g * 2 * n_half + n_half) for g in range(N_GROUPS)]
    args = [q] * N_GROUPS + [kv_new] * (2 * N_GROUPS)
    cache_spec = lambda g: pl.BlockSpec((1, 2, HALF, GROUPS[g][0]), lambda i: (i // n_half, 0, i % n_half, 0))
    in_specs += [cache_spec(g) for g in range(N_GROUPS)]
    args += list(caches_t)
    tiles = [_sample_bias_tiles(bias_gs[g], g, t) for g in range(N_GROUPS)]
    for part in range(2):
        for g in range(N_GROUPS):
            in_specs.append(_const_spec(tiles[g][part].shape))
            args.append(tiles[g][part])
    out_specs = [pl.BlockSpec((1, t, HALF), lambda i: (i // n_half, 0, i % n_half))]
    out_specs += [cache_spec(g) for g in range(N_GROUPS)]
    out_shape = [jax.ShapeDtypeStruct((b, t, ATTN_WIDTH), BF16)] + [jax.ShapeDtypeStruct(c.shape, F32) for c in caches_t]
    return in_specs, args, out_specs, out_shape


def _out_proj_kernel(x_ref, a_ref, wo_ref, out_ref):
    out_ref[...] = x_ref[...] + jnp.dot(a_ref[...], wo_ref[...], preferred_element_type=F32)


def _out_proj(x, a, w_o):
    n, d = x.shape
    return pl.pallas_call(
        _out_proj_kernel,
        out_shape=jax.ShapeDtypeStruct((n, d), F32),
        compiler_params=pltpu.CompilerParams(vmem_limit_bytes=VMEM_LIMIT),
        name="out_proj",
    )(x, a, w_o)


def _cache_view(c_t, b):
    w = c_t.shape[-1]
    return jnp.transpose(c_t.reshape(b, 2, HEADS_PER_GROUP, HEAD_DIM, w), (0, 4, 1, 2, 3))


def _conv_args(wts):
    return (wts["norms"][0, 1], wts["conv_w_pw1"][0], wts["conv_b_pw1"][0], wts["conv_w_dw"][0], wts["conv_b_dw"][0],
            wts["conv_ln_g"][0], wts["conv_ln_b"][0], wts["conv_w_pw2"][0], wts["conv_b_pw2"][0])


def _sample_until_attention(x, conv_prev, kv_bufs, wts):
    b, t, d = x.shape
    norms, w_in, w_out = wts["norms"], wts["ffn_w_in"], wts["ffn_w_out"]
    xf = _ffn(x.reshape(b * t, d), norms[0, 0], w_in, w_out, (0, 0))
    x3, conv_state = _conv_module(xf.reshape(b, t, d), conv_prev, *_conv_args(wts), min(SAMPLE_CONV_BATCH, b), t)
    xf = _ffn(x3.reshape(b * t, d), norms[0, 2], w_in, w_out, (0, 1))
    kv_new = _proj(xf, wts["norm_kv"], wts["w_kv"], F32).reshape(b, t, -1)
    xf = _ffn(xf, norms[1, 0], w_in, w_out, (1, 0))
    q = _proj(xf, norms[1, 1], wts["attn_w_q"][0], BF16, scale=HEAD_DIM ** -0.5).reshape(b, t, -1)
    caches_t = [jnp.transpose(buf, (0, 2, 3, 4, 1)).reshape(b, 2, ATTN_WIDTH, buf.shape[1]) for buf in kv_bufs]
    return xf, conv_state, (q, kv_new, caches_t, wts["bias_gs"])


def _sample_after_attention(xf, attn, wts):
    xf = _out_proj(xf, attn.reshape(xf.shape[0], ATTN_WIDTH), wts["attn_w_o"][0])
    return _ffn(xf, wts["norms"][1, 2], wts["ffn_w_in"], wts["ffn_w_out"], (1, 1), g_final=wts["norm_final"])


def _prompt_trunk(x, wts, sample_attn):
    b, t, d = x.shape
    norms, w_in, w_out = wts["norms"], wts["ffn_w_in"], wts["ffn_w_out"]
    xf, *sample_res = _ffn(x.reshape(b * t, d), norms[0, 0], w_in, w_out, (0, 0), sample_attn=sample_attn)
    x3, conv_state = _conv_module_pipelined(xf.reshape(b, t, d), *_conv_args(wts))
    xf = _ffn(x3.reshape(b * t, d), norms[0, 2], w_in, w_out, (0, 1))
    shared, *kv_t = _kv_prompt(xf.reshape(b, t, d), wts["norm_kv"], wts["w_kv"])
    q_proj = (norms[1, 1], wts["attn_w_q"][0], HEAD_DIM ** -0.5)
    xf, q_sl = _ffn(xf, norms[1, 0], w_in, w_out, (1, 0), q_proj=q_proj)
    outs, lses = [], []
    for g in range(N_GROUPS):
        o, lse = _attn_prompt(q_sl, shared, _prompt_bias_tiles(wts["bias_gs"][g]), g, b, t)
        outs.append(o)
        lses.append(lse)
    xf = _ffn(xf, norms[1, 2], w_in, w_out, (1, 1), merge=(outs, lses, wts["attn_w_o"][0]),
              g_final=wts["norm_final"])
    return xf.reshape(b, t, d), conv_state, [_cache_view(c, b) for c in kv_t], sample_res


def kernel(x_prompt, x_sample, state_conv, cache_kv_w128, cache_kv_w512, cache_kv_w2048, norms, ffn_w_in, ffn_w_out,
           conv_w_pw1, conv_b_pw1, conv_w_dw, conv_b_dw, conv_ln_g, conv_ln_b, conv_w_pw2, conv_b_pw2, norm_kv, w_kv,
           attn_w_q, attn_w_o, rel_bias, norm_final):
    wts = dict(
        norms=norms, ffn_w_in=ffn_w_in.astype(BF16), ffn_w_out=ffn_w_out.astype(BF16),
        conv_w_pw1=conv_w_pw1.astype(BF16), conv_b_pw1=conv_b_pw1, conv_w_dw=conv_w_dw, conv_b_dw=conv_b_dw,
        conv_ln_g=conv_ln_g, conv_ln_b=conv_ln_b, conv_w_pw2=conv_w_pw2.astype(BF16), conv_b_pw2=conv_b_pw2,
        norm_kv=norm_kv, w_kv=w_kv.astype(BF16), attn_w_q=attn_w_q.astype(BF16), attn_w_o=attn_w_o.astype(BF16),
        rel_bias=rel_bias, norm_final=norm_final)
    wts["bias_gs"] = [_group_bias(rel_bias, g) for g in range(N_GROUPS)]
    bs, ts, d = x_sample.shape
    xs, conv_sample, sample_attn = _sample_until_attention(
        x_sample, state_conv[0], (cache_kv_w128, cache_kv_w512, cache_kv_w2048), wts)
    y_prompt, conv_prompt, kv_prompt, (attn_s, *caches_new) = _prompt_trunk(x_prompt, wts, sample_attn)
    y_sample = _sample_after_attention(xs, attn_s, wts).reshape(bs, ts, d)
    kv_sample = [_cache_view(c, bs) for c in caches_new]
    return (y_prompt, y_sample, conv_prompt[None], conv_sample[None], kv_prompt[0], kv_sample[0],
            kv_prompt[1], kv_sample[1], kv_prompt[2], kv_sample[2])
```

```python
import functools
import math

import jax
import jax.numpy as jnp
from jax import lax
from jax.experimental import pallas as pl
from jax.experimental.pallas import tpu as pltpu

F32 = jnp.float32
BF16 = jnp.bfloat16

EPS = 1e-6
NEG_INF = -1e30
GROUPS = ((128, 1), (512, 4), (2048, 16))
N_GROUPS = len(GROUPS)
HEADS_PER_GROUP = 8
HEAD_DIM = 64
ATTN_WIDTH = HEADS_PER_GROUP * HEAD_DIM
ROW_WIDTH = 2 * ATTN_WIDTH
N_KEYS = 129
N_BUCKETS = 32
MAX_DISTANCE = 2048
PAST_LEN = 8192
CONV_WIDTH = 31
CONV_HIST = 32
LANES = 128
SUBLANES = 8
HALF = 256
HEADS_PER_HALF = HALF // HEAD_DIM
QBLOCK = 128
ATTN_QUERIES = (4096, 1024, 256)
TOKEN_TILE = 512
PLAIN_FFN_TILE = 1024
SAMPLE_CONV_BATCH = 16
CONV_ROWS = 64
VMEM_LIMIT = 56 * 1024 * 1024


def _params(n_axes):
    return pltpu.CompilerParams(dimension_semantics=("arbitrary",) * n_axes, vmem_limit_bytes=VMEM_LIMIT)


def _const_spec(shape):
    zeros = (0,) * len(shape)
    return pl.BlockSpec(shape, lambda *_: zeros, pipeline_mode=pl.Buffered(1))


def _rms(x, g):
    return x * lax.rsqrt(jnp.mean(x * x, axis=-1, keepdims=True) + EPS) * g


def _silu(x):
    return x * jax.nn.sigmoid(x)


def _ffn_kernel(*refs, d_ff, n_chunk, has_merge, has_final, q_scale, has_sample_attn):
    refs = list(refs)
    x_ref = refs.pop(0)
    if has_merge:
        attn_refs = [refs.pop(0) for _ in range(2 * N_GROUPS)]
        e_ref, wo_ref = refs.pop(0), refs.pop(0)
    g_ref, win_ref, wout_ref = refs.pop(0), refs.pop(0), refs.pop(0)
    if has_final:
        gf_ref = refs.pop(0)
    if q_scale is not None:
        gq_ref, wq_ref = refs.pop(0), refs.pop(0)
    if has_sample_attn:
        sample_in = [refs.pop(0) for _ in range(N_SAMPLE_ATTN_INPUTS)]
    o_ref = refs.pop(0)
    if q_scale is not None:
        q_ref = refs.pop(0)
    if has_sample_attn:
        sample_out = [refs.pop(0) for _ in range(N_SAMPLE_ATTN_OUTPUTS)]
    act_ref = refs.pop(0)

    if has_sample_attn:
        _attn_sample_unit(sample_in, sample_out, pl.program_id(0) % (ATTN_WIDTH // HALF))

    x = x_ref[...]
    if has_merge:
        x = x + _merged_attention(attn_refs[:N_GROUPS], attn_refs[N_GROUPS:], e_ref, wo_ref)
    h = _rms(x, g_ref[...]).astype(BF16)
    for c in range(d_ff // n_chunk):
        lo = c * n_chunk
        gate = jnp.dot(h, win_ref[:, lo:lo + n_chunk], preferred_element_type=F32)
        up = jnp.dot(h, win_ref[:, d_ff + lo:d_ff + lo + n_chunk], preferred_element_type=F32)
        act_ref[:, lo:lo + n_chunk] = (_silu(gate) * up).astype(BF16)
    y = jnp.dot(act_ref[...], wout_ref[...], preferred_element_type=F32)
    o = x + 0.5 * y
    if q_scale is not None:
        hq = _rms(o, gq_ref[...]).astype(BF16)
        _store_slabs(q_ref, jnp.dot(hq, wq_ref[...], preferred_element_type=F32) * q_scale)
    if has_final:
        o = _rms(o, gf_ref[...])
    o_ref[...] = o


def _ffn(x, g, w_in, w_out, idx, *, merge=None, g_final=None, q_proj=None, sample_attn=None):
    n, d = x.shape
    d_ff = w_out.shape[-2]
    plain = merge is None and q_proj is None and sample_attn is None
    tm = min(PLAIN_FFN_TILE if plain else TOKEN_TILE, n)
    if sample_attn is not None:
        n_units = sample_attn[0].shape[0] * (ATTN_WIDTH // HALF)
        tm = n // n_units
        assert tm * n_units == n and tm % SUBLANES == 0, (n, n_units)
    tok = lambda width: pl.BlockSpec((tm, width), lambda i: (i, 0))
    slabs = lambda count: pl.BlockSpec((count, tm, LANES), lambda i: (0, i, 0))
    stacked = lambda rows, cols: pl.BlockSpec((None, None, rows, cols), lambda i: (*idx, 0, 0),
                                              pipeline_mode=pl.Buffered(1))
    in_specs = [tok(d)]
    args = [x]
    if merge is not None:
        outs, lses, w_o = merge
        in_specs += [slabs(ATTN_WIDTH // LANES)] * N_GROUPS + [slabs(ATTN_WIDTH // HALF)] * N_GROUPS
        in_specs += [_const_spec((2 * LANES, ATTN_WIDTH)), _const_spec((ATTN_WIDTH, d))]
        args += [*outs, *lses, _head_expander(), w_o]
    in_specs += [_const_spec((1, d)), stacked(d, 2 * d_ff), stacked(d_ff, d)]
    args += [g.reshape(1, d), w_in, w_out]
    if g_final is not None:
        in_specs.append(_const_spec((1, d)))
        args.append(g_final.reshape(1, d))
    out_specs = [tok(d)]
    out_shape = [jax.ShapeDtypeStruct((n, d), F32)]
    q_scale = None
    if q_proj is not None:
        g_q, w_q, q_scale = q_proj
        n_q = w_q.shape[1]
        in_specs += [_const_spec((1, d)), _const_spec((d, n_q))]
        args += [g_q.reshape(1, d), w_q]
        out_specs.append(slabs(n_q // LANES))
        out_shape.append(jax.ShapeDtypeStruct((n_q // LANES, n, LANES), F32))
    if sample_attn is not None:
        s_in_specs, s_args, s_out_specs, s_out_shape = _attn_sample_operands(*sample_attn)
        in_specs += s_in_specs
        args += s_args
        out_specs += s_out_specs
        out_shape += s_out_shape
    res = pl.pallas_call(
        functools.partial(_ffn_kernel, d_ff=d_ff, n_chunk=HALF, has_merge=merge is not None,
                          has_final=g_final is not None, q_scale=q_scale, has_sample_attn=sample_attn is not None),
        grid=(n // tm,),
        in_specs=in_specs,
        out_specs=out_specs,
        out_shape=out_shape,
        scratch_shapes=[pltpu.VMEM((tm, d_ff), BF16)],
        compiler_params=_params(1),
        name="ffn",
    )(*args)
    return res[0] if len(res) == 1 else res


def _proj_kernel(x_ref, g_ref, w_ref, o_ref, *, scale):
    h = _rms(x_ref[...], g_ref[...]).astype(BF16)
    y = jnp.dot(h, w_ref[...], preferred_element_type=F32)
    if scale != 1.0:
        y = y * scale
    o_ref[...] = y.astype(o_ref.dtype)


def _proj(x, g, w, out_dtype, scale=1.0):
    n, d = x.shape
    n_out = w.shape[1]
    tm = min(TOKEN_TILE, n)
    return pl.pallas_call(
        functools.partial(_proj_kernel, scale=scale),
        grid=(n // tm,),
        in_specs=[pl.BlockSpec((tm, d), lambda i: (i, 0)), _const_spec((1, d)), _const_spec((d, n_out))],
        out_specs=pl.BlockSpec((tm, n_out), lambda i: (i, 0)),
        out_shape=jax.ShapeDtypeStruct((n, n_out), out_dtype),
        compiler_params=_params(1),
        name="proj",
    )(x, g.reshape(1, d), w)


def _store_slabs(slab_ref, y):
    for sl in range(slab_ref.shape[0]):
        slab_ref[sl] = y[:, sl * LANES:(sl + 1) * LANES]


def _kv_prompt_kernel(x_ref, g_ref, w_ref, sh_ref, t0_ref, t1_ref, t2_ref, *, n_t, tm):
    h = _rms(x_ref[0], g_ref[...]).astype(BF16)
    y = jnp.dot(h, w_ref[...], preferred_element_type=F32)
    _store_slabs(sh_ref, y)
    for g, t_ref in enumerate((t0_ref, t1_ref, t2_ref)):
        rows = min(GROUPS[g][0], n_t * tm, tm)
        t_ref[0] = y[tm - rows:, g * ROW_WIDTH:(g + 1) * ROW_WIDTH].T


def _kv_prompt(x, g, w):
    b, s, d = x.shape
    n_out = w.shape[1]
    tm = min(TOKEN_TILE, s)
    n_t = s // tm
    out_specs = [pl.BlockSpec((n_out // LANES, tm, LANES), lambda i, j: (0, i * n_t + j, 0))]
    out_shape = [jax.ShapeDtypeStruct((n_out // LANES, b * s, LANES), F32)]
    for grp in range(N_GROUPS):
        keep = min(GROUPS[grp][0], s)
        first = n_t - max(keep // tm, 1)
        out_specs.append(pl.BlockSpec((1, ROW_WIDTH, min(keep, tm)),
                                      lambda i, j, first=first: (i, 0, jnp.maximum(j - first, 0))))
        out_shape.append(jax.ShapeDtypeStruct((b, ROW_WIDTH, keep), F32))
    return pl.pallas_call(
        functools.partial(_kv_prompt_kernel, n_t=n_t, tm=tm),
        grid=(b, n_t),
        in_specs=[pl.BlockSpec((1, tm, d), lambda i, j: (i, j, 0)), _const_spec((1, d)), _const_spec((d, n_out))],
        out_specs=out_specs,
        out_shape=out_shape,
        compiler_params=_params(2),
        name="kv_prompt",
    )(x, g.reshape(1, d), w)


def _conv_kernel(x_ref, prev_ref, g_ref, w1_ref, b1_ref, wdw_ref, bdw_ref, lng_ref, lnb_ref, w2_ref, b2_ref,
                 o_ref, st_ref, uext_ref, ush_ref, conv_ref, *, bb, ts, n_t, rc):
    it = pl.program_id(1)
    d = x_ref.shape[-1]
    c_in = w2_ref.shape[0]
    x = x_ref[...].reshape(bb * ts, d)
    h = _rms(x, g_ref[...]).astype(BF16)
    ag = jnp.dot(h, w1_ref[...], preferred_element_type=F32) + b1_ref[...]
    u = ag[:, :c_in] * jax.nn.sigmoid(ag[:, c_in:])

    @pl.when(it == 0)
    def _():
        uext_ref[:, 0:CONV_HIST, :] = prev_ref[...]

    uext_ref[:, CONV_HIST:CONV_HIST + ts, :] = u.reshape(bb, ts, c_in)

    lead = CONV_HIST - (CONV_WIDTH - 1)
    n_lb = c_in // LANES
    for s in range(SUBLANES):
        n_sh = ts + SUBLANES * ((CONV_WIDTH - 1 - s) // SUBLANES)
        for lb in range(n_lb):
            ush_ref[s, lb, :, 0:n_sh, :] = uext_ref[:, lead + s:lead + s + n_sh, lb * LANES:(lb + 1) * LANES]

    reps = rc // SUBLANES
    for lb in range(n_lb):
        lanes = slice(lb * LANES, (lb + 1) * LANES)
        taps = [wdw_ref[k, :, lanes] for k in range(CONV_WIDTH)]
        bias = bdw_ref[:, lanes]

        def chunk(ci, carry, lb=lb, lanes=lanes, taps=taps, bias=bias):
            r0 = pl.multiple_of(ci * rc, rc)
            acc = jnp.broadcast_to(bias.reshape(1, 1, LANES), (bb, rc, LANES))
            for k in range(CONV_WIDTH):
                win = ush_ref[k % SUBLANES, lb, :, pl.ds(r0 + SUBLANES * (k // SUBLANES), rc), :]
                wk = jnp.concatenate([taps[k]] * reps, axis=0) if reps > 1 else taps[k]
                acc = acc + win * wk[None]
            conv_ref[:, pl.ds(r0, rc), lanes] = acc
            return carry

        lax.fori_loop(0, ts // rc, chunk, 0)

    acc = conv_ref[...].reshape(bb * ts, c_in)
    mu = jnp.mean(acc, axis=-1, keepdims=True)
    cen = acc - mu
    var = jnp.mean(cen * cen, axis=-1, keepdims=True)
    c = _silu(cen * lax.rsqrt(var + EPS) * lng_ref[...] + lnb_ref[...]).astype(BF16)
    y = jnp.dot(c, w2_ref[...], preferred_element_type=F32) + b2_ref[...]
    o_ref[...] = (x + y).reshape(bb, ts, d)

    @pl.when(it == n_t - 1)
    def _():
        st_ref[...] = uext_ref[:, ts:ts + CONV_HIST, :]

    if n_t > 1:
        uext_ref[:, 0:CONV_HIST, :] = uext_ref[:, ts:ts + CONV_HIST, :]


def _conv_module(x, prev, g, w1, b1, wdw, bdw, lng, lnb, w2, b2, bb, ts):
    b, t, d = x.shape
    c_in = w2.shape[0]
    n_t = t // ts
    rc = min(ts, CONV_ROWS)
    row = lambda v: v.reshape(1, -1)
    in_specs = [pl.BlockSpec((bb, ts, d), lambda i, j: (i, j, 0)),
                pl.BlockSpec((bb, CONV_HIST, c_in), lambda i, j: (i, 0, 0)),
                _const_spec((1, d)), _const_spec((d, 2 * c_in)), _const_spec((1, 2 * c_in)),
                _const_spec((CONV_WIDTH, SUBLANES, c_in)), _const_spec((1, c_in)), _const_spec((1, c_in)),
                _const_spec((1, c_in)), _const_spec((c_in, d)), _const_spec((1, d))]
    wdw_rep = jnp.broadcast_to(wdw[:, None, :], (CONV_WIDTH, SUBLANES, c_in))
    prev_rows = jnp.pad(prev, ((0, 0), (CONV_HIST - prev.shape[1], 0), (0, 0)))
    args = [x, prev_rows, row(g), w1, row(b1), wdw_rep, row(bdw), row(lng), row(lnb), w2, row(b2)]
    out, st = pl.pallas_call(
        functools.partial(_conv_kernel, bb=bb, ts=ts, n_t=n_t, rc=rc),
        grid=(b // bb, n_t),
        in_specs=in_specs,
        out_specs=[pl.BlockSpec((bb, ts, d), lambda i, j: (i, j, 0)),
                   pl.BlockSpec((bb, CONV_HIST, c_in), lambda i, j: (i, 0, 0))],
        out_shape=[jax.ShapeDtypeStruct((b, t, d), F32), jax.ShapeDtypeStruct((b, CONV_HIST, c_in), F32)],
        scratch_shapes=[pltpu.VMEM((bb, ts + CONV_HIST, c_in), F32),
                        pltpu.VMEM((SUBLANES, c_in // LANES, bb, ts + 24, LANES), F32),
                        pltpu.VMEM((bb, ts, c_in), F32)],
        compiler_params=_params(2),
        name="conv_module",
    )(*args)
    return out, st[:, CONV_HIST - (CONV_WIDTH - 1):]


def _conv_pipe_kernel(xc_ref, xp_ref, g_ref, w1_ref, b1_ref, wdw_ref, bdw_ref, lng_ref, lnb_ref, w2_ref, b2_ref,
                      o_ref, st_ref, h_ref, u0_ref, u1_ref, ush_ref, conv_ref, *, ts, n_t, rc):
    i = pl.program_id(0)
    n_lb = conv_ref.shape[0]
    lead = CONV_HIST - (CONV_WIDTH - 1)
    reps = rc // SUBLANES

    @pl.when(i == 0)
    def _():
        u1_ref[...] = jnp.zeros(u1_ref.shape, F32)

    def step(ua_ref, ub_ref):
        starts_sequence = (i % n_t) == 0
        h_ref[...] = _rms(xc_ref[...], g_ref[...]).astype(BF16)
        for lb in range(n_lb):
            lanes = slice(lb * LANES, (lb + 1) * LANES)
            ag = jnp.dot(h_ref[...], w1_ref[lb], preferred_element_type=F32) + b1_ref[lb]
            u = ag[:, :LANES] * jax.nn.sigmoid(ag[:, LANES:])
            ua_ref[lb, 0:CONV_HIST, :] = jnp.where(starts_sequence, 0.0, ub_ref[lb, ts:ts + CONV_HIST, :])
            ua_ref[lb, CONV_HIST:CONV_HIST + ts, :] = u
            buf = lb % 2
            for s in range(SUBLANES):
                n_sh = ts + SUBLANES * ((CONV_WIDTH - 1 - s) // SUBLANES)
                ush_ref[buf, s, 0:n_sh, :] = ub_ref[lb, lead + s:lead + s + n_sh, :]
            taps = [wdw_ref[k, :, lanes] for k in range(CONV_WIDTH)]
            bias = jnp.broadcast_to(bdw_ref[:, lanes], (rc, LANES))
            for ci in range(ts // rc):
                r0 = ci * rc
                acc = bias
                for k in range(CONV_WIDTH):
                    lo = r0 + SUBLANES * (k // SUBLANES)
                    wk = jnp.concatenate([taps[k]] * reps, axis=0) if reps > 1 else taps[k]
                    acc = acc + ush_ref[buf, k % SUBLANES, lo:lo + rc, :] * wk
                conv_ref[lb, r0:r0 + rc, :] = acc

        acc = jnp.concatenate([conv_ref[lb] for lb in range(n_lb)], axis=1)
        mu = jnp.mean(acc, axis=-1, keepdims=True)
        cen = acc - mu
        var = jnp.mean(cen * cen, axis=-1, keepdims=True)
        c = _silu(cen * lax.rsqrt(var + EPS) * lng_ref[...] + lnb_ref[...]).astype(BF16)
        y = jnp.dot(c, w2_ref[...], preferred_element_type=F32) + b2_ref[...]
        o_ref[...] = xp_ref[...] + y

        @pl.when((i % n_t) == n_t - 1)
        def _():
            for lb in range(n_lb):
                st_ref[0, :, lb * LANES:(lb + 1) * LANES] = ua_ref[lb, ts:ts + CONV_HIST, :]

    @pl.when(i % 2 == 0)
    def _():
        step(u0_ref, u1_ref)

    @pl.when(i % 2 == 1)
    def _():
        step(u1_ref, u0_ref)


def _conv_module_pipelined(x, g, w1, b1, wdw, bdw, lng, lnb, w2, b2):
    b, t, d = x.shape
    c_in = w2.shape[0]
    ts = TOKEN_TILE
    n_t = t // ts
    n_tiles = b * n_t
    n_lb = c_in // LANES
    row = lambda v: v.reshape(1, -1)
    cols = lambda v, lb: v[..., lb * LANES:(lb + 1) * LANES]
    w1_blocks = jnp.stack([jnp.concatenate([cols(w1, lb), cols(w1, n_lb + lb)], axis=-1) for lb in range(n_lb)])
    b1_blocks = jnp.stack([jnp.concatenate([cols(row(b1), lb), cols(row(b1), n_lb + lb)], axis=-1)
                           for lb in range(n_lb)])
    wdw_rep = jnp.broadcast_to(wdw[:, None, :], (CONV_WIDTH, SUBLANES, c_in))
    last = n_tiles - 1
    out, st = pl.pallas_call(
        functools.partial(_conv_pipe_kernel, ts=ts, n_t=n_t, rc=CONV_ROWS // 2),
        grid=(n_tiles + 1,),
        in_specs=[pl.BlockSpec((ts, d), lambda i: (jnp.minimum(i, last), 0)),
                  pl.BlockSpec((ts, d), lambda i: (jnp.maximum(i - 1, 0), 0)),
                  _const_spec((1, d)), _const_spec((n_lb, d, 2 * LANES)), _const_spec((n_lb, 1, 2 * LANES)),
                  _const_spec((CONV_WIDTH, SUBLANES, c_in)), _const_spec((1, c_in)), _const_spec((1, c_in)),
                  _const_spec((1, c_in)), _const_spec((c_in, d)), _const_spec((1, d))],
        out_specs=[pl.BlockSpec((ts, d), lambda i: (jnp.maximum(i - 1, 0), 0)),
                   pl.BlockSpec((1, CONV_HIST, c_in), lambda i: (jnp.minimum(i, last) // n_t, 0, 0))],
        out_shape=[jax.ShapeDtypeStruct((b * t, d), F32), jax.ShapeDtypeStruct((b, CONV_HIST, c_in), F32)],
        scratch_shapes=[pltpu.VMEM((ts, d), BF16),
                        pltpu.VMEM((n_lb, ts + CONV_HIST, LANES), F32),
                        pltpu.VMEM((n_lb, ts + CONV_HIST, LANES), F32),
                        pltpu.VMEM((2, SUBLANES, ts + 24, LANES), F32),
                        pltpu.VMEM((n_lb, ts, LANES), F32)],
        compiler_params=_params(1),
        name="conv_pipelined",
    )(x.reshape(b * t, d), x.reshape(b * t, d), row(g), w1_blocks, b1_blocks, wdw_rep, row(bdw), row(lng), row(lnb),
      w2, row(b2))
    return out.reshape(b, t, d), st[:, CONV_HIST - (CONV_WIDTH - 1):]


def _t5_bucket(dist):
    max_exact = N_BUCKETS // 2
    d_f = jnp.maximum(dist, 1).astype(F32)
    large = max_exact + (jnp.log(d_f / max_exact) / math.log(MAX_DISTANCE / max_exact)
                         * (N_BUCKETS - max_exact)).astype(jnp.int32)
    large = jnp.minimum(large, N_BUCKETS - 1)
    return jnp.where(dist < max_exact, dist, large)


def _group_bias(rel_bias, g):
    _, d = GROUPS[g]
    dist = d * jnp.arange(N_KEYS, dtype=jnp.int32)
    b = jnp.take(rel_bias, _t5_bucket(dist), axis=0)
    return b[:, g * HEADS_PER_GROUP:(g + 1) * HEADS_PER_GROUP].T.astype(F32)


def _toeplitz(p, n_rows, n_cols):
    length = p.shape[1]
    stride = length - 1
    flat = jnp.tile(p, (1, n_rows))[:, :n_rows * stride]
    return flat.reshape(p.shape[0], n_rows, stride)[:, :, :n_cols]


def _stack_rows(t):
    return t.reshape(2, HEADS_PER_HALF * t.shape[1], t.shape[2])


def _prompt_bias_tiles(bias_g):
    neg = jnp.full((bias_g.shape[0], 2 * QBLOCK + 1 - N_KEYS), NEG_INF, F32)
    p = jnp.concatenate([bias_g[:, ::-1], neg], axis=1)
    regular = _toeplitz(p, QBLOCK, 2 * QBLOCK)
    c = jnp.arange(2 * QBLOCK, dtype=jnp.int32)[None, None, :]
    first = jnp.where(c >= QBLOCK, regular, NEG_INF)
    return jnp.stack([_stack_rows(regular), _stack_rows(first)], axis=0)


def _sample_bias_tiles(bias_g, g, t_new):
    w, d = GROUPS[g]
    n_heads = bias_g.shape[0]
    vals = bias_g[:, N_KEYS - 1:0:-1]
    strided = jnp.concatenate([vals[:, :, None], jnp.full((n_heads, N_KEYS - 1, d - 1), NEG_INF, F32)], axis=2)
    p = jnp.concatenate([strided.reshape(n_heads, w), jnp.full((n_heads, t_new), NEG_INF, F32)], axis=1)
    cached = _toeplitz(p, t_new, w)
    row = jnp.arange(w, dtype=jnp.int32)[None, None, :]
    min_valid = 2 * w - PAST_LEN
    cached = jnp.where(w + row >= min_valid, cached, NEG_INF)
    t = jnp.arange(t_new, dtype=jnp.int32)[None, :, None]
    i = jnp.arange(LANES, dtype=jnp.int32)[None, None, :] - (LANES - t_new)
    dist = t - i
    new = jnp.full((n_heads, t_new, LANES), NEG_INF, F32)
    for j in range((t_new - 1) // d + 1):
        new = jnp.where((i >= 0) & (dist == d * j), bias_g[:, j][:, None, None], new)
    return _stack_rows(cached), _stack_rows(new)


def _stack_heads(x, lane_head):
    return jnp.concatenate([jnp.where(lane_head == h, x, jnp.zeros_like(x)) for h in range(HEADS_PER_HALF)], axis=0)


def _unstack_heads(x, lane_head, q):
    out = jnp.where(lane_head == 0, x[0:q], 0.0)
    for h in range(1, HEADS_PER_HALF):
        out = out + jnp.where(lane_head == h, x[h * q:(h + 1) * q], 0.0)
    return out


def _residue_rows(start, d):
    return pl.ds(start, QBLOCK, stride=d) if d > 1 else pl.ds(start, QBLOCK)


def _gather_rows(ref, start, d):
    return jnp.concatenate([ref[sl, _residue_rows(start, d), :] for sl in range(ref.shape[0])], axis=1)


def _attn_prompt_kernel(*refs, d, n_sub, has_prev):
    if has_prev:
        q_ref, kp_ref, kc_ref, vp_ref, vc_ref, bias_ref, o_ref, lse_ref = refs
    else:
        q_ref, kc_ref, vc_ref, bias_ref, o_ref, lse_ref = refs
    qb = QBLOCK
    j = pl.program_id(1)
    half = pl.program_id(2)
    lane_head = lax.broadcasted_iota(jnp.int32, (qb, HALF), 1) // HEAD_DIM
    lane = lax.broadcasted_iota(jnp.int32, (qb, LANES), 1)
    first = jnp.where(j == 0, 1, 0)
    for r in range(d):
        k_prev = v_prev = None
        if has_prev:
            k_prev = _gather_rows(kp_ref, r, d).astype(BF16)
            v_prev = _gather_rows(vp_ref, r, d).astype(BF16)
        for sub in range(n_sub):
            start = r + d * sub * qb
            q = _gather_rows(q_ref, start, d).astype(BF16)
            k_cur = _gather_rows(kc_ref, start, d).astype(BF16)
            v_cur = _gather_rows(vc_ref, start, d).astype(BF16)
            if k_prev is None:
                k, v = k_cur, v_cur
                bias = bias_ref[1, half, :, qb:]
            else:
                k = jnp.concatenate([k_prev, k_cur], axis=0)
                v = jnp.concatenate([v_prev, v_cur], axis=0)
                bias = bias_ref[first, half] if sub == 0 else bias_ref[0, half]
            qs = _stack_heads(q, lane_head)
            s = lax.dot_general(qs, k, (((1,), (1,)), ((), ())), preferred_element_type=F32) + bias
            m = jnp.max(s, axis=-1, keepdims=True)
            p = jnp.exp(s - m)
            den = jnp.sum(p, axis=-1, keepdims=True)
            ov = jnp.dot((p / den).astype(BF16), v, preferred_element_type=F32)
            o = _unstack_heads(ov, lane_head, qb)
            rows = _residue_rows(start, d)
            for sl in range(o_ref.shape[0]):
                o_ref[sl, rows, :] = o[:, sl * LANES:(sl + 1) * LANES]
            lse = m + jnp.log(den)
            lse_tile = jnp.zeros((qb, LANES), F32)
            for h in range(HEADS_PER_HALF):
                lse_tile = jnp.where(lane == half * HEADS_PER_HALF + h, lse[h * qb:(h + 1) * qb], lse_tile)
            lse_ref[0, rows, :] = lse_tile
            k_prev, v_prev = k_cur, v_cur


def _attn_prompt(q_sl, kv_sl, bias_tiles, g, b, s):
    w, d = GROUPS[g]
    tb = d * ATTN_QUERIES[g]
    pb = d * QBLOCK
    n_tb = s // tb
    sph = HALF // LANES
    n_half = ATTN_WIDTH // HALF
    has_prev = n_tb > 1
    cur = lambda i, j: i * n_tb + j
    prv = lambda i, j: jnp.maximum(i * (s // pb) + j * (tb // pb) - 1, 0)
    cur_spec = lambda slab: pl.BlockSpec((sph, tb, LANES), lambda i, j, hf: (slab + hf, cur(i, j), 0))
    prv_spec = lambda slab: pl.BlockSpec((sph, pb, LANES), lambda i, j, hf: (slab + hf, prv(i, j), 0))
    k_slab, v_slab = g * 2 * n_half, g * 2 * n_half + n_half
    in_specs = [cur_spec(g * n_half)]
    in_specs += [prv_spec(k_slab), cur_spec(k_slab), prv_spec(v_slab), cur_spec(v_slab)] if has_prev else \
        [cur_spec(k_slab), cur_spec(v_slab)]
    in_specs.append(_const_spec(bias_tiles.shape))
    return pl.pallas_call(
        functools.partial(_attn_prompt_kernel, d=d, n_sub=ATTN_QUERIES[g] // QBLOCK, has_prev=has_prev),
        grid=(b, n_tb, n_half),
        in_specs=in_specs,
        out_specs=[pl.BlockSpec((sph, tb, LANES), lambda i, j, hf: (hf, cur(i, j), 0)),
                   pl.BlockSpec((1, tb, LANES), lambda i, j, hf: (hf, cur(i, j), 0))],
        out_shape=[jax.ShapeDtypeStruct((n_half * sph, b * s, LANES), F32),
                   jax.ShapeDtypeStruct((n_half, b * s, LANES), F32)],
        compiler_params=_params(3),
        name=f"attn_prompt_g{g}",
    )(q_sl, *([kv_sl] * (len(in_specs) - 2)), bias_tiles)


def _expand_heads(w, e_ref):
    hi = w.astype(BF16)
    lo = (w - hi.astype(F32)).astype(BF16)
    return jnp.dot(jnp.concatenate([hi, lo], axis=1), e_ref[...], preferred_element_type=F32)


def _merged_attention(o_refs, l_refs, e_ref, wo_ref):
    ls = [l_ref[0] + l_ref[1] for l_ref in l_refs]
    mx = jnp.maximum(jnp.maximum(ls[0], ls[1]), ls[2])
    es = [jnp.exp(l - mx) for l in ls]
    tot = es[0] + es[1] + es[2]
    merged = None
    for e, o_ref in zip(es, o_refs):
        o = jnp.concatenate([o_ref[sl] for sl in range(o_ref.shape[0])], axis=1)
        term = _expand_heads(e / tot, e_ref) * o
        merged = term if merged is None else merged + term
    return jnp.dot(merged.astype(BF16), wo_ref[...], preferred_element_type=F32)


def _head_expander():
    head = jnp.arange(2 * LANES, dtype=jnp.int32)[:, None] % LANES
    lane = jnp.arange(ATTN_WIDTH, dtype=jnp.int32)[None, :]
    return (lane // HEAD_DIM == head).astype(BF16)


N_SAMPLE_ATTN_INPUTS = 6 * N_GROUPS
N_SAMPLE_ATTN_OUTPUTS = 1 + N_GROUPS


def _attn_sample_unit(in_refs, out_refs, half):
    (q0_ref, q1_ref, q2_ref, kn0_ref, kn1_ref, kn2_ref, vn0_ref, vn1_ref, vn2_ref, c0_ref, c1_ref, c2_ref,
     bc0_ref, bc1_ref, bc2_ref, bn0_ref, bn1_ref, bn2_ref) = in_refs
    o_ref, n0_ref, n1_ref, n2_ref = out_refs
    t = q0_ref.shape[1]
    lane_head = lax.broadcasted_iota(jnp.int32, (t, HALF), 1) // HEAD_DIM
    is_new = lax.broadcasted_iota(jnp.int32, (HALF, LANES), 1) >= LANES - t
    zpad = jnp.zeros((LANES - t, HALF), F32)
    outs, lses = [], []
    for q_ref, kn_ref, vn_ref, c_ref, bc_ref, bn_ref, n_ref in (
            (q0_ref, kn0_ref, vn0_ref, c0_ref, bc0_ref, bn0_ref, n0_ref),
            (q1_ref, kn1_ref, vn1_ref, c1_ref, bc1_ref, bn1_ref, n1_ref),
            (q2_ref, kn2_ref, vn2_ref, c2_ref, bc2_ref, bn2_ref, n2_ref)):
        w = c_ref.shape[3]
        qs = _stack_heads(q_ref[0], lane_head)
        k_t = c_ref[0, 0]
        v_t = c_ref[0, 1]
        k_new = jnp.concatenate([zpad, kn_ref[0]], axis=0)
        v_new = jnp.concatenate([zpad, vn_ref[0]], axis=0)
        k_new_t = k_new.T
        v_new_t = v_new.T
        s_c = jnp.dot(qs, k_t.astype(BF16), preferred_element_type=F32) + bc_ref[half]
        s_n = jnp.dot(qs, k_new_t.astype(BF16), preferred_element_type=F32) + bn_ref[half]
        m = jnp.maximum(s_c.max(axis=-1, keepdims=True), s_n.max(axis=-1, keepdims=True))
        p_c = jnp.exp(s_c - m)
        p_n = jnp.exp(s_n - m)
        den = p_c.sum(axis=-1, keepdims=True) + p_n.sum(axis=-1, keepdims=True)
        ov = lax.dot_general((p_c / den).astype(BF16), v_t.astype(BF16), (((1,), (1,)), ((), ())),
                             preferred_element_type=F32)
        ov = ov + jnp.dot((p_n / den).astype(BF16), v_new.astype(BF16), preferred_element_type=F32)
        outs.append(ov)
        lses.append(m + jnp.log(den))
        for kv, (old_t, new_t) in enumerate(((k_t, k_new_t), (v_t, v_new_t))):
            rolled = pltpu.roll(old_t, w - t, axis=1)
            if w > LANES:
                n_ref[0, kv, :, 0:w - LANES] = rolled[:, 0:w - LANES]
            n_ref[0, kv, :, w - LANES:w] = jnp.where(is_new, new_t, rolled[:, w - LANES:w])
    mx = jnp.maximum(jnp.maximum(lses[0], lses[1]), lses[2])
    es = [jnp.exp(l - mx) for l in lses]
    tot = es[0] + es[1] + es[2]
    merged = (es[0] / tot) * outs[0] + (es[1] / tot) * outs[1] + (es[2] / tot) * outs[2]
    o_ref[0] = _unstack_heads(merged, lane_head, t).astype(o_ref.dtype)


def _attn_sample_operands(q, kv_new, caches_t, bias_gs):
    b, t, _ = q.shape
    n_half = ATTN_WIDTH // HALF
    tok = lambda col: pl.BlockSpec((1, t, HALF), lambda i, col=col: (i // n_half, 0, col + i % n_half))
    in_specs = [tok(g * n_half) for g in range(N_GROUPS)]
    in_specs += [tok(g * 2 * n_half) for g in range(N_GROUPS)]
    in_specs += [tok(g * 2 * n_half + n_half) for g in range(N_GROUPS)]
    args = [q] * N_GROUPS + [kv_new] * (2 * N_GROUPS)
    cache_spec = lambda g: pl.BlockSpec((1, 2, HALF, GROUPS[g][0]), lambda i: (i // n_half, 0, i % n_half, 0))
    in_specs += [cache_spec(g) for g in range(N_GROUPS)]
    args += list(caches_t)
    tiles = [_sample_bias_tiles(bias_gs[g], g, t) for g in range(N_GROUPS)]
    for part in range(2):
        for g in range(N_GROUPS):
            in_specs.append(_const_spec(tiles[g][part].shape))
            args.append(tiles[g][part])
    out_specs = [pl.BlockSpec((1, t, HALF), lambda i: (i // n_half, 0, i % n_half))]
    out_specs += [cache_spec(g) for g in range(N_GROUPS)]
    out_shape = [jax.ShapeDtypeStruct((b, t, ATTN_WIDTH), BF16)] + [jax.ShapeDtypeStruct(c.shape, F32) for c in caches_t]
    return in_specs, args, out_specs, out_shape


def _out_proj_kernel(x_ref, a_ref, wo_ref, out_ref):
    out_ref[...] = x_ref[...] + jnp.dot(a_ref[...], wo_ref[...], preferred_element_type=F32)


def _out_proj(x, a, w_o):
    n, d = x.shape
    return pl.pallas_call(
        _out_proj_kernel,
        out_shape=jax.ShapeDtypeStruct((n, d), F32),
        compiler_params=pltpu.CompilerParams(vmem_limit_bytes=VMEM_LIMIT),
        name="out_proj",
    )(x, a, w_o)


def _cache_view(c_t, b):
    w = c_t.shape[-1]
    return jnp.transpose(c_t.reshape(b, 2, HEADS_PER_GROUP, HEAD_DIM, w), (0, 4, 1, 2, 3))


def _conv_args(wts):
    return (wts["norms"][0, 1], wts["conv_w_pw1"][0], wts["conv_b_pw1"][0], wts["conv_w_dw"][0], wts["conv_b_dw"][0],
            wts["conv_ln_g"][0], wts["conv_ln_b"][0], wts["conv_w_pw2"][0], wts["conv_b_pw2"][0])


def _sample_until_attention(x, conv_prev, kv_bufs, wts):
    b, t, d = x.shape
    norms, w_in, w_out = wts["norms"], wts["ffn_w_in"], wts["ffn_w_out"]
    xf = _ffn(x.reshape(b * t, d), norms[0, 0], w_in, w_out, (0, 0))
    x3, conv_state = _conv_module(xf.reshape(b, t, d), conv_prev, *_conv_args(wts), min(SAMPLE_CONV_BATCH, b), t)
    xf = _ffn(x3.reshape(b * t, d), norms[0, 2], w_in, w_out, (0, 1))
    kv_new = _proj(xf, wts["norm_kv"], wts["w_kv"], F32).reshape(b, t, -1)
    xf = _ffn(xf, norms[1, 0], w_in, w_out, (1, 0))
    q = _proj(xf, norms[1, 1], wts["attn_w_q"][0], BF16, scale=HEAD_DIM ** -0.5).reshape(b, t, -1)
    caches_t = [jnp.transpose(buf, (0, 2, 3, 4, 1)).reshape(b, 2, ATTN_WIDTH, buf.shape[1]) for buf in kv_bufs]
    return xf, conv_state, (q, kv_new, caches_t, wts["bias_gs"])


def _sample_after_attention(xf, attn, wts):
    xf = _out_proj(xf, attn.reshape(xf.shape[0], ATTN_WIDTH), wts["attn_w_o"][0])
    return _ffn(xf, wts["norms"][1, 2], wts["ffn_w_in"], wts["ffn_w_out"], (1, 1), g_final=wts["norm_final"])


def _prompt_trunk(x, wts, sample_attn):
    b, t, d = x.shape
    norms, w_in, w_out = wts["norms"], wts["ffn_w_in"], wts["ffn_w_out"]
    xf, *sample_res = _ffn(x.reshape(b * t, d), norms[0, 0], w_in, w_out, (0, 0), sample_attn=sample_attn)
    x3, conv_state = _conv_module_pipelined(xf.reshape(b, t, d), *_conv_args(wts))
    xf = _ffn(x3.reshape(b * t, d), norms[0, 2], w_in, w_out, (0, 1))
    shared, *kv_t = _kv_prompt(xf.reshape(b, t, d), wts["norm_kv"], wts["w_kv"])
    q_proj = (norms[1, 1], wts["attn_w_q"][0], HEAD_DIM ** -0.5)
    xf, q_sl = _ffn(xf, norms[1, 0], w_in, w_out, (1, 0), q_proj=q_proj)
    outs, lses = [], []
    for g in range(N_GROUPS):
        o, lse = _attn_prompt(q_sl, shared, _prompt_bias_tiles(wts["bias_gs"][g]), g, b, t)
        outs.append(o)
        lses.append(lse)
    xf = _ffn(xf, norms[1, 2], w_in, w_out, (1, 1), merge=(outs, lses, wts["attn_w_o"][0]),
              g_final=wts["norm_final"])
    return xf.reshape(b, t, d), conv_state, [_cache_view(c, b) for c in kv_t], sample_res


def kernel(x_prompt, x_sample, state_conv, cache_kv_w128, cache_kv_w512, cache_kv_w2048, norms, ffn_w_in, ffn_w_out,
           conv_w_pw1, conv_b_pw1, conv_w_dw, conv_b_dw, conv_ln_g, conv_ln_b, conv_w_pw2, conv_b_pw2, norm_kv, w_kv,
           attn_w_q, attn_w_o, rel_bias, norm_final):
    wts = dict(
        norms=norms, ffn_w_in=ffn_w_in.astype(BF16), ffn_w_out=ffn_w_out.astype(BF16),
        conv_w_pw1=conv_w_pw1.astype(BF16), conv_b_pw1=conv_b_pw1, conv_w_dw=conv_w_dw, conv_b_dw=conv_b_dw,
        conv_ln_g=conv_ln_g, conv_ln_b=conv_ln_b, conv_w_pw2=conv_w_pw2.astype(BF16), conv_b_pw2=conv_b_pw2,
        norm_kv=norm_kv, w_kv=w_kv.astype(BF16), attn_w_q=attn_w_q.astype(BF16), attn_w_o=attn_w_o.astype(BF16),
        rel_bias=rel_bias, norm_final=norm_final)
    wts["bias_gs"] = [_group_bias(rel_bias, g) for g in range(N_GROUPS)]
    bs, ts, d = x_sample.shape
    xs, conv_sample, sample_attn = _sample_until_attention(
        x_sample, state_conv[0], (cache_kv_w128, cache_kv_w512, cache_kv_w2048), wts)
    y_prompt, conv_prompt, kv_prompt, (attn_s, *caches_new) = _prompt_trunk(x_prompt, wts, sample_attn)
    y_sample = _sample_after_attention(xs, attn_s, wts).reshape(bs, ts, d)
    kv_sample = [_cache_view(c, bs) for c in caches_new]
    return (y_prompt, y_sample, conv_prompt[None], conv_sample[None], kv_prompt[0], kv_sample[0],
            kv_prompt[1], kv_sample[1], kv_prompt[2], kv_sample[2])
```

```python
import functools
import math

import jax
import jax.numpy as jnp
from jax import lax
from jax.experimental import pallas as pl
from jax.experimental.pallas import tpu as pltpu

F32 = jnp.float32
BF16 = jnp.bfloat16

EPS = 1e-6
NEG_INF = -1e30
GROUPS = ((128, 1), (512, 4), (2048, 16))
N_GROUPS = len(GROUPS)
HEADS_PER_GROUP = 8
HEAD_DIM = 64
ATTN_WIDTH = HEADS_PER_GROUP * HEAD_DIM
ROW_WIDTH = 2 * ATTN_WIDTH
N_KEYS = 129
N_BUCKETS = 32
MAX_DISTANCE = 2048
PAST_LEN = 8192
CONV_WIDTH = 31
CONV_HIST = 32
LANES = 128
SUBLANES = 8
HALF = 256
HEADS_PER_HALF = HALF // HEAD_DIM
QBLOCK = 128
ATTN_QUERIES = (4096, 1024, 256)
TOKEN_TILE = 512
PLAIN_FFN_TILE = 1024
SAMPLE_CONV_BATCH = 16
CONV_ROWS = 64
VMEM_LIMIT = 56 * 1024 * 1024


def _params(n_axes):
    return pltpu.CompilerParams(dimension_semantics=("arbitrary",) * n_axes, vmem_limit_bytes=VMEM_LIMIT)


def _const_spec(shape):
    zeros = (0,) * len(shape)
    return pl.BlockSpec(shape, lambda *_: zeros, pipeline_mode=pl.Buffered(1))


def _rms(x, g):
    return x * lax.rsqrt(jnp.mean(x * x, axis=-1, keepdims=True) + EPS) * g


def _silu(x):
    return x * jax.nn.sigmoid(x)


def _ffn_kernel(*refs, d_ff, n_chunk, has_merge, has_final, q_scale, has_sample_attn):
    refs = list(refs)
    x_ref = refs.pop(0)
    if has_merge:
        attn_refs = [refs.pop(0) for _ in range(2 * N_GROUPS)]
        e_ref, wo_ref = refs.pop(0), refs.pop(0)
    g_ref, win_ref, wout_ref = refs.pop(0), refs.pop(0), refs.pop(0)
    if has_final:
        gf_ref = refs.pop(0)
    if q_scale is not None:
        gq_ref, wq_ref = refs.pop(0), refs.pop(0)
    if has_sample_attn:
        sample_in = [refs.pop(0) for _ in range(N_SAMPLE_ATTN_INPUTS)]
    o_ref = refs.pop(0)
    if q_scale is not None:
        q_ref = refs.pop(0)
    if has_sample_attn:
        sample_out = [refs.pop(0) for _ in range(N_SAMPLE_ATTN_OUTPUTS)]
    act_ref = refs.pop(0)

    if has_sample_attn:
        _attn_sample_unit(sample_in, sample_out, pl.program_id(0) % (ATTN_WIDTH // HALF))

    x = x_ref[...]
    if has_merge:
        x = x + _merged_attention(attn_refs[:N_GROUPS], attn_refs[N_GROUPS:], e_ref, wo_ref)
    h = _rms(x, g_ref[...]).astype(BF16)
    for c in range(d_ff // n_chunk):
        lo = c * n_chunk
        gate = jnp.dot(h, win_ref[:, lo:lo + n_chunk], preferred_element_type=F32)
        up = jnp.dot(h, win_ref[:, d_ff + lo:d_ff + lo + n_chunk], preferred_element_type=F32)
        act_ref[:, lo:lo + n_chunk] = (_silu(gate) * up).astype(BF16)
    y = jnp.dot(act_ref[...], wout_ref[...], preferred_element_type=F32)
    o = x + 0.5 * y
    if q_scale is not None:
        hq = _rms(o, gq_ref[...]).astype(BF16)
        _store_slabs(q_ref, jnp.dot(hq, wq_ref[...], preferred_element_type=F32) * q_scale)
    if has_final:
        o = _rms(o, gf_ref[...])
    o_ref[...] = o


def _ffn(x, g, w_in, w_out, idx, *, merge=None, g_final=None, q_proj=None, sample_attn=None):
    n, d = x.shape
    d_ff = w_out.shape[-2]
    plain = merge is None and q_proj is None and sample_attn is None
    tm = min(PLAIN_FFN_TILE if plain else TOKEN_TILE, n)
    if sample_attn is not None:
        n_units = sample_attn[0].shape[0] * (ATTN_WIDTH // HALF)
        tm = n // n_units
        assert tm * n_units == n and tm % SUBLANES == 0, (n, n_units)
    tok = lambda width: pl.BlockSpec((tm, width), lambda i: (i, 0))
    slabs = lambda count: pl.BlockSpec((count, tm, LANES), lambda i: (0, i, 0))
    stacked = lambda rows, cols: pl.BlockSpec((None, None, rows, cols), lambda i: (*idx, 0, 0),
                                              pipeline_mode=pl.Buffered(1))
    in_specs = [tok(d)]
    args = [x]
    if merge is not None:
        outs, lses, w_o = merge
        in_specs += [slabs(ATTN_WIDTH // LANES)] * N_GROUPS + [slabs(ATTN_WIDTH // HALF)] * N_GROUPS
        in_specs += [_const_spec((2 * LANES, ATTN_WIDTH)), _const_spec((ATTN_WIDTH, d))]
        args += [*outs, *lses, _head_expander(), w_o]
    in_specs += [_const_spec((1, d)), stacked(d, 2 * d_ff), stacked(d_ff, d)]
    args += [g.reshape(1, d), w_in, w_out]
    if g_final is not None:
        in_specs.append(_const_spec((1, d)))
        args.append(g_final.reshape(1, d))
    out_specs = [tok(d)]
    out_shape = [jax.ShapeDtypeStruct((n, d), F32)]
    q_scale = None
    if q_proj is not None:
        g_q, w_q, q_scale = q_proj
        n_q = w_q.shape[1]
        in_specs += [_const_spec((1, d)), _const_spec((d, n_q))]
        args += [g_q.reshape(1, d), w_q]
        out_specs.append(slabs(n_q // LANES))
        out_shape.append(jax.ShapeDtypeStruct((n_q // LANES, n, LANES), F32))
    if sample_attn is not None:
        s_in_specs, s_args, s_out_specs, s_out_shape = _attn_sample_operands(*sample_attn)
        in_specs += s_in_specs
        args += s_args
        out_specs += s_out_specs
        out_shape += s_out_shape
    res = pl.pallas_call(
        functools.partial(_ffn_kernel, d_ff=d_ff, n_chunk=HALF, has_merge=merge is not None,
                          has_final=g_final is not None, q_scale=q_scale, has_sample_attn=sample_attn is not None),
        grid=(n // tm,),
        in_specs=in_specs,
        out_specs=out_specs,
        out_shape=out_shape,
        scratch_shapes=[pltpu.VMEM((tm, d_ff), BF16)],
        compiler_params=_params(1),
        name="ffn",
    )(*args)
    return res[0] if len(res) == 1 else res


def _proj_kernel(x_ref, g_ref, w_ref, o_ref, *, scale):
    h = _rms(x_ref[...], g_ref[...]).astype(BF16)
    y = jnp.dot(h, w_ref[...], preferred_element_type=F32)
    if scale != 1.0:
        y = y * scale
    o_ref[...] = y.astype(o_ref.dtype)


def _proj(x, g, w, out_dtype, scale=1.0):
    n, d = x.shape
    n_out = w.shape[1]
    tm = min(TOKEN_TILE, n)
    return pl.pallas_call(
        functools.partial(_proj_kernel, scale=scale),
        grid=(n // tm,),
        in_specs=[pl.BlockSpec((tm, d), lambda i: (i, 0)), _const_spec((1, d)), _const_spec((d, n_out))],
        out_specs=pl.BlockSpec((tm, n_out), lambda i: (i, 0)),
        out_shape=jax.ShapeDtypeStruct((n, n_out), out_dtype),
        compiler_params=_params(1),
        name="proj",
    )(x, g.reshape(1, d), w)


def _store_slabs(slab_ref, y):
    for sl in range(slab_ref.shape[0]):
        slab_ref[sl] = y[:, sl * LANES:(sl + 1) * LANES]


def _kv_prompt_kernel(x_ref, g_ref, w_ref, sh_ref, t0_ref, t1_ref, t2_ref, *, n_t, tm):
    h = _rms(x_ref[0], g_ref[...]).astype(BF16)
    y = jnp.dot(h, w_ref[...], preferred_element_type=F32)
    _store_slabs(sh_ref, y)
    for g, t_ref in enumerate((t0_ref, t1_ref, t2_ref)):
        rows = min(GROUPS[g][0], n_t * tm, tm)
        t_ref[0] = y[tm - rows:, g * ROW_WIDTH:(g + 1) * ROW_WIDTH].T


def _kv_prompt(x, g, w):
    b, s, d = x.shape
    n_out = w.shape[1]
    tm = min(TOKEN_TILE, s)
    n_t = s // tm
    out_specs = [pl.BlockSpec((n_out // LANES, tm, LANES), lambda i, j: (0, i * n_t + j, 0))]
    out_shape = [jax.ShapeDtypeStruct((n_out // LANES, b * s, LANES), F32)]
    for grp in range(N_GROUPS):
        keep = min(GROUPS[grp][0], s)
        first = n_t - max(keep // tm, 1)
        out_specs.append(pl.BlockSpec((1, ROW_WIDTH, min(keep, tm)),
                                      lambda i, j, first=first: (i, 0, jnp.maximum(j - first, 0))))
        out_shape.append(jax.ShapeDtypeStruct((b, ROW_WIDTH, keep), F32))
    return pl.pallas_call(
        functools.partial(_kv_prompt_kernel, n_t=n_t, tm=tm),
        grid=(b, n_t),
        in_specs=[pl.BlockSpec((1, tm, d), lambda i, j: (i, j, 0)), _const_spec((1, d)), _const_spec((d, n_out))],
        out_specs=out_specs,
        out_shape=out_shape,
        compiler_params=_params(2),
        name="kv_prompt",
    )(x, g.reshape(1, d), w)


def _conv_kernel(x_ref, prev_ref, g_ref, w1_ref, b1_ref, wdw_ref, bdw_ref, lng_ref, lnb_ref, w2_ref, b2_ref,
                 o_ref, st_ref, uext_ref, ush_ref, conv_ref, *, bb, ts, n_t, rc):
    it = pl.program_id(1)
    d = x_ref.shape[-1]
    c_in = w2_ref.shape[0]
    x = x_ref[...].reshape(bb * ts, d)
    h = _rms(x, g_ref[...]).astype(BF16)
    ag = jnp.dot(h, w1_ref[...], preferred_element_type=F32) + b1_ref[...]
    u = ag[:, :c_in] * jax.nn.sigmoid(ag[:, c_in:])

    @pl.when(it == 0)
    def _():
        uext_ref[:, 0:CONV_HIST, :] = prev_ref[...]

    uext_ref[:, CONV_HIST:CONV_HIST + ts, :] = u.reshape(bb, ts, c_in)

    lead = CONV_HIST - (CONV_WIDTH - 1)
    n_lb = c_in // LANES
    for s in range(SUBLANES):
        n_sh = ts + SUBLANES * ((CONV_WIDTH - 1 - s) // SUBLANES)
        for lb in range(n_lb):
            ush_ref[s, lb, :, 0:n_sh, :] = uext_ref[:, lead + s:lead + s + n_sh, lb * LANES:(lb + 1) * LANES]

    reps = rc // SUBLANES
    for lb in range(n_lb):
        lanes = slice(lb * LANES, (lb + 1) * LANES)
        taps = [wdw_ref[k, :, lanes] for k in range(CONV_WIDTH)]
        bias = bdw_ref[:, lanes]

        def chunk(ci, carry, lb=lb, lanes=lanes, taps=taps, bias=bias):
            r0 = pl.multiple_of(ci * rc, rc)
            acc = jnp.broadcast_to(bias.reshape(1, 1, LANES), (bb, rc, LANES))
            for k in range(CONV_WIDTH):
                win = ush_ref[k % SUBLANES, lb, :, pl.ds(r0 + SUBLANES * (k // SUBLANES), rc), :]
                wk = jnp.concatenate([taps[k]] * reps, axis=0) if reps > 1 else taps[k]
                acc = acc + win * wk[None]
            conv_ref[:, pl.ds(r0, rc), lanes] = acc
            return carry

        lax.fori_loop(0, ts // rc, chunk, 0)

    acc = conv_ref[...].reshape(bb * ts, c_in)
    mu = jnp.mean(acc, axis=-1, keepdims=True)
    cen = acc - mu
    var = jnp.mean(cen * cen, axis=-1, keepdims=True)
    c = _silu(cen * lax.rsqrt(var + EPS) * lng_ref[...] + lnb_ref[...]).astype(BF16)
    y = jnp.dot(c, w2_ref[...], preferred_element_type=F32) + b2_ref[...]
    o_ref[...] = (x + y).reshape(bb, ts, d)

    @pl.when(it == n_t - 1)
    def _():
        st_ref[...] = uext_ref[:, ts:ts + CONV_HIST, :]

    if n_t > 1:
        uext_ref[:, 0:CONV_HIST, :] = uext_ref[:, ts:ts + CONV_HIST, :]


def _conv_module(x, prev, g, w1, b1, wdw, bdw, lng, lnb, w2, b2, bb, ts):
    b, t, d = x.shape
    c_in = w2.shape[0]
    n_t = t // ts
    rc = min(ts, CONV_ROWS)
    row = lambda v: v.reshape(1, -1)
    in_specs = [pl.BlockSpec((bb, ts, d), lambda i, j: (i, j, 0)),
                pl.BlockSpec((bb, CONV_HIST, c_in), lambda i, j: (i, 0, 0)),
                _const_spec((1, d)), _const_spec((d, 2 * c_in)), _const_spec((1, 2 * c_in)),
                _const_spec((CONV_WIDTH, SUBLANES, c_in)), _const_spec((1, c_in)), _const_spec((1, c_in)),
                _const_spec((1, c_in)), _const_spec((c_in, d)), _const_spec((1, d))]
    wdw_rep = jnp.broadcast_to(wdw[:, None, :], (CONV_WIDTH, SUBLANES, c_in))
    prev_rows = jnp.pad(prev, ((0, 0), (CONV_HIST - prev.shape[1], 0), (0, 0)))
    args = [x, prev_rows, row(g), w1, row(b1), wdw_rep, row(bdw), row(lng), row(lnb), w2, row(b2)]
    out, st = pl.pallas_call(
        functools.partial(_conv_kernel, bb=bb, ts=ts, n_t=n_t, rc=rc),
        grid=(b // bb, n_t),
        in_specs=in_specs,
        out_specs=[pl.BlockSpec((bb, ts, d), lambda i, j: (i, j, 0)),
                   pl.BlockSpec((bb, CONV_HIST, c_in), lambda i, j: (i, 0, 0))],
        out_shape=[jax.ShapeDtypeStruct((b, t, d), F32), jax.ShapeDtypeStruct((b, CONV_HIST, c_in), F32)],
        scratch_shapes=[pltpu.VMEM((bb, ts + CONV_HIST, c_in), F32),
                        pltpu.VMEM((SUBLANES, c_in // LANES, bb, ts + 24, LANES), F32),
                        pltpu.VMEM((bb, ts, c_in), F32)],
        compiler_params=_params(2),
        name="conv_module",
    )(*args)
    return out, st[:, CONV_HIST - (CONV_WIDTH - 1):]


def _conv_pipe_kernel(xc_ref, xp_ref, g_ref, w1_ref, b1_ref, wdw_ref, bdw_ref, lng_ref, lnb_ref, w2_ref, b2_ref,
                      o_ref, st_ref, h_ref, u0_ref, u1_ref, ush_ref, conv_ref, *, ts, n_t, rc):
    i = pl.program_id(0)
    n_lb = conv_ref.shape[0]
    lead = CONV_HIST - (CONV_WIDTH - 1)
    reps = rc // SUBLANES

    @pl.when(i == 0)
    def _():
        u1_ref[...] = jnp.zeros(u1_ref.shape, F32)

    def step(ua_ref, ub_ref):
        starts_sequence = (i % n_t) == 0
        h_ref[...] = _rms(xc_ref[...], g_ref[...]).astype(BF16)
        for lb in range(n_lb):
            lanes = slice(lb * LANES, (lb + 1) * LANES)
            ag = jnp.dot(h_ref[...], w1_ref[lb], preferred_element_type=F32) + b1_ref[lb]
            u = ag[:, :LANES] * jax.nn.sigmoid(ag[:, LANES:])
            ua_ref[lb, 0:CONV_HIST, :] = jnp.where(starts_sequence, 0.0, ub_ref[lb, ts:ts + CONV_HIST, :])
            ua_ref[lb, CONV_HIST:CONV_HIST + ts, :] = u
            buf = lb % 2
            for s in range(SUBLANES):
                n_sh = ts + SUBLANES * ((CONV_WIDTH - 1 - s) // SUBLANES)
                ush_ref[buf, s, 0:n_sh, :] = ub_ref[lb, lead + s:lead + s + n_sh, :]
            taps = [wdw_ref[k, :, lanes] for k in range(CONV_WIDTH)]
            bias = jnp.broadcast_to(bdw_ref[:, lanes], (rc, LANES))
            for ci in range(ts // rc):
                r0 = ci * rc
                acc = bias
                for k in range(CONV_WIDTH):
                    lo = r0 + SUBLANES * (k // SUBLANES)
                    wk = jnp.concatenate([taps[k]] * reps, axis=0) if reps > 1 else taps[k]
                    acc = acc + ush_ref[buf, k % SUBLANES, lo:lo + rc, :] * wk
                conv_ref[lb, r0:r0 + rc, :] = acc

        acc = jnp.concatenate([conv_ref[lb] for lb in range(n_lb)], axis=1)
        mu = jnp.mean(acc, axis=-1, keepdims=True)
        cen = acc - mu
        var = jnp.mean(cen * cen, axis=-1, keepdims=True)
        c = _silu(cen * lax.rsqrt(var + EPS) * lng_ref[...] + lnb_ref[...]).astype(BF16)
        y = jnp.dot(c, w2_ref[...], preferred_element_type=F32) + b2_ref[...]
        o_ref[...] = xp_ref[...] + y

        @pl.when((i % n_t) == n_t - 1)
        def _():
            for lb in range(n_lb):
                st_ref[0, :, lb * LANES:(lb + 1) * LANES] = ua_ref[lb, ts:ts + CONV_HIST, :]

    @pl.when(i % 2 == 0)
    def _():
        step(u0_ref, u1_ref)

    @pl.when(i % 2 == 1)
    def _():
        step(u1_ref, u0_ref)


def _conv_module_pipelined(x, g, w1, b1, wdw, bdw, lng, lnb, w2, b2):
    b, t, d = x.shape
    c_in = w2.shape[0]
    ts = TOKEN_TILE
    n_t = t // ts
    n_tiles = b * n_t
    n_lb = c_in // LANES
    row = lambda v: v.reshape(1, -1)
    cols = lambda v, lb: v[..., lb * LANES:(lb + 1) * LANES]
    w1_blocks = jnp.stack([jnp.concatenate([cols(w1, lb), cols(w1, n_lb + lb)], axis=-1) for lb in range(n_lb)])
    b1_blocks = jnp.stack([jnp.concatenate([cols(row(b1), lb), cols(row(b1), n_lb + lb)], axis=-1)
                           for lb in range(n_lb)])
    wdw_rep = jnp.broadcast_to(wdw[:, None, :], (CONV_WIDTH, SUBLANES, c_in))
    last = n_tiles - 1
    out, st = pl.pallas_call(
        functools.partial(_conv_pipe_kernel, ts=ts, n_t=n_t, rc=CONV_ROWS // 2),
        grid=(n_tiles + 1,),
        in_specs=[pl.BlockSpec((ts, d), lambda i: (jnp.minimum(i, last), 0)),
                  pl.BlockSpec((ts, d), lambda i: (jnp.maximum(i - 1, 0), 0)),
                  _const_spec((1, d)), _const_spec((n_lb, d, 2 * LANES)), _const_spec((n_lb, 1, 2 * LANES)),
                  _const_spec((CONV_WIDTH, SUBLANES, c_in)), _const_spec((1, c_in)), _const_spec((1, c_in)),
                  _const_spec((1, c_in)), _const_spec((c_in, d)), _const_spec((1, d))],
        out_specs=[pl.BlockSpec((ts, d), lambda i: (jnp.maximum(i - 1, 0), 0)),
                   pl.BlockSpec((1, CONV_HIST, c_in), lambda i: (jnp.minimum(i, last) // n_t, 0, 0))],
        out_shape=[jax.ShapeDtypeStruct((b * t, d), F32), jax.ShapeDtypeStruct((b, CONV_HIST, c_in), F32)],
        scratch_shapes=[pltpu.VMEM((ts, d), BF16),
                        pltpu.VMEM((n_lb, ts + CONV_HIST, LANES), F32),
                        pltpu.VMEM((n_lb, ts + CONV_HIST, LANES), F32),
                        pltpu.VMEM((2, SUBLANES, ts + 24, LANES), F32),
                        pltpu.VMEM((n_lb, ts, LANES), F32)],
        compiler_params=_params(1),
        name="conv_pipelined",
    )(x.reshape(b * t, d), x.reshape(b * t, d), row(g), w1_blocks, b1_blocks, wdw_rep, row(bdw), row(lng), row(lnb),
      w2, row(b2))
    return out.reshape(b, t, d), st[:, CONV_HIST - (CONV_WIDTH - 1):]


def _t5_bucket(dist):
    max_exact = N_BUCKETS // 2
    d_f = jnp.maximum(dist, 1).astype(F32)
    large = max_exact + (jnp.log(d_f / max_exact) / math.log(MAX_DISTANCE / max_exact)
                         * (N_BUCKETS - max_exact)).astype(jnp.int32)
    large = jnp.minimum(large, N_BUCKETS - 1)
    return jnp.where(dist < max_exact, dist, large)


def _group_bias(rel_bias, g):
    _, d = GROUPS[g]
    dist = d * jnp.arange(N_KEYS, dtype=jnp.int32)
    b = jnp.take(rel_bias, _t5_bucket(dist), axis=0)
    return b[:, g * HEADS_PER_GROUP:(g + 1) * HEADS_PER_GROUP].T.astype(F32)


def _toeplitz(p, n_rows, n_cols):
    length = p.shape[1]
    stride = length - 1
    flat = jnp.tile(p, (1, n_rows))[:, :n_rows * stride]
    return flat.reshape(p.shape[0], n_rows, stride)[:, :, :n_cols]


def _stack_rows(t):
    return t.reshape(2, HEADS_PER_HALF * t.shape[1], t.shape[2])


def _prompt_bias_tiles(bias_g):
    neg = jnp.full((bias_g.shape[0], 2 * QBLOCK + 1 - N_KEYS), NEG_INF, F32)
    p = jnp.concatenate([bias_g[:, ::-1], neg], axis=1)
    regular = _toeplitz(p, QBLOCK, 2 * QBLOCK)
    c = jnp.arange(2 * QBLOCK, dtype=jnp.int32)[None, None, :]
    first = jnp.where(c >= QBLOCK, regular, NEG_INF)
    return jnp.stack([_stack_rows(regular), _stack_rows(first)], axis=0)


def _sample_bias_tiles(bias_g, g, t_new):
    w, d = GROUPS[g]
    n_heads = bias_g.shape[0]
    vals = bias_g[:, N_KEYS - 1:0:-1]
    strided = jnp.concatenate([vals[:, :, None], jnp.full((n_heads, N_KEYS - 1, d - 1), NEG_INF, F32)], axis=2)
    p = jnp.concatenate([strided.reshape(n_heads, w), jnp.full((n_heads, t_new), NEG_INF, F32)], axis=1)
    cached = _toeplitz(p, t_new, w)
    row = jnp.arange(w, dtype=jnp.int32)[None, None, :]
    min_valid = 2 * w - PAST_LEN
    cached = jnp.where(w + row >= min_valid, cached, NEG_INF)
    t = jnp.arange(t_new, dtype=jnp.int32)[None, :, None]
    i = jnp.arange(LANES, dtype=jnp.int32)[None, None, :] - (LANES - t_new)
    dist = t - i
    new = jnp.full((n_heads, t_new, LANES), NEG_INF, F32)
    for j in range((t_new - 1) // d + 1):
        new = jnp.where((i >= 0) & (dist == d * j), bias_g[:, j][:, None, None], new)
    return _stack_rows(cached), _stack_rows(new)


def _stack_heads(x, lane_head):
    return jnp.concatenate([jnp.where(lane_head == h, x, jnp.zeros_like(x)) for h in range(HEADS_PER_HALF)], axis=0)


def _unstack_heads(x, lane_head, q):
    out = jnp.where(lane_head == 0, x[0:q], 0.0)
    for h in range(1, HEADS_PER_HALF):
        out = out + jnp.where(lane_head == h, x[h * q:(h + 1) * q], 0.0)
    return out


def _residue_rows(start, d):
    return pl.ds(start, QBLOCK, stride=d) if d > 1 else pl.ds(start, QBLOCK)


def _gather_rows(ref, start, d):
    return jnp.concatenate([ref[sl, _residue_rows(start, d), :] for sl in range(ref.shape[0])], axis=1)


def _attn_prompt_kernel(*refs, d, n_sub, has_prev):
    if has_prev:
        q_ref, kp_ref, kc_ref, vp_ref, vc_ref, bias_ref, o_ref, lse_ref = refs
    else:
        q_ref, kc_ref, vc_ref, bias_ref, o_ref, lse_ref = refs
    qb = QBLOCK
    j = pl.program_id(1)
    half = pl.program_id(2)
    lane_head = lax.broadcasted_iota(jnp.int32, (qb, HALF), 1) // HEAD_DIM
    lane = lax.broadcasted_iota(jnp.int32, (qb, LANES), 1)
    first = jnp.where(j == 0, 1, 0)
    for r in range(d):
        k_prev = v_prev = None
        if has_prev:
            k_prev = _gather_rows(kp_ref, r, d).astype(BF16)
            v_prev = _gather_rows(vp_ref, r, d).astype(BF16)
        for sub in range(n_sub):
            start = r + d * sub * qb
            q = _gather_rows(q_ref, start, d).astype(BF16)
            k_cur = _gather_rows(kc_ref, start, d).astype(BF16)
            v_cur = _gather_rows(vc_ref, start, d).astype(BF16)
            if k_prev is None:
                k, v = k_cur, v_cur
                bias = bias_ref[1, half, :, qb:]
            else:
                k = jnp.concatenate([k_prev, k_cur], axis=0)
                v = jnp.concatenate([v_prev, v_cur], axis=0)
                bias = bias_ref[first, half] if sub == 0 else bias_ref[0, half]
            qs = _stack_heads(q, lane_head)
            s = lax.dot_general(qs, k, (((1,), (1,)), ((), ())), preferred_element_type=F32) + bias
            m = jnp.max(s, axis=-1, keepdims=True)
            p = jnp.exp(s - m)
            den = jnp.sum(p, axis=-1, keepdims=True)
            ov = jnp.dot((p / den).astype(BF16), v, preferred_element_type=F32)
            o = _unstack_heads(ov, lane_head, qb)
            rows = _residue_rows(start, d)
            for sl in range(o_ref.shape[0]):
                o_ref[sl, rows, :] = o[:, sl * LANES:(sl + 1) * LANES]
            lse = m + jnp.log(den)
            lse_tile = jnp.zeros((qb, LANES), F32)
            for h in range(HEADS_PER_HALF):
                lse_tile = jnp.where(lane == half * HEADS_PER_HALF + h, lse[h * qb:(h + 1) * qb], lse_tile)
            lse_ref[0, rows, :] = lse_tile
            k_prev, v_prev = k_cur, v_cur


def _attn_prompt(q_sl, kv_sl, bias_tiles, g, b, s):
    w, d = GROUPS[g]
    tb = d * ATTN_QUERIES[g]
    pb = d * QBLOCK
    n_tb = s // tb
    sph = HALF // LANES
    n_half = ATTN_WIDTH // HALF
    has_prev = n_tb > 1 or ATTN_QUERIES[g] // QBLOCK > 2
    cur = lambda i, j: i * n_tb + j
    prv = lambda i, j: jnp.maximum(i * (s // pb) + j * (tb // pb) - 1, 0)
    cur_spec = lambda slab: pl.BlockSpec((sph, tb, LANES), lambda i, j, hf: (slab + hf, cur(i, j), 0))
    prv_spec = lambda slab: pl.BlockSpec((sph, pb, LANES), lambda i, j, hf: (slab + hf, prv(i, j), 0))
    k_slab, v_slab = g * 2 * n_half, g * 2 * n_half + n_half
    in_specs = [cur_spec(g * n_half)]
    in_specs += [prv_spec(k_slab), cur_spec(k_slab), prv_spec(v_slab), cur_spec(v_slab)] if has_prev else \
        [cur_spec(k_slab), cur_spec(v_slab)]
    in_specs.append(_const_spec(bias_tiles.shape))
    return pl.pallas_call(
        functools.partial(_attn_prompt_kernel, d=d, n_sub=ATTN_QUERIES[g] // QBLOCK, has_prev=has_prev),
        grid=(b, n_tb, n_half),
        in_specs=in_specs,
        out_specs=[pl.BlockSpec((sph, tb, LANES), lambda i, j, hf: (hf, cur(i, j), 0)),
                   pl.BlockSpec((1, tb, LANES), lambda i, j, hf: (hf, cur(i, j), 0))],
        out_shape=[jax.ShapeDtypeStruct((n_half * sph, b * s, LANES), F32),
                   jax.ShapeDtypeStruct((n_half, b * s, LANES), F32)],
        compiler_params=_params(3),
        name=f"attn_prompt_g{g}",
    )(q_sl, *([kv_sl] * (len(in_specs) - 2)), bias_tiles)


def _expand_heads(w, e_ref):
    hi = w.astype(BF16)
    lo = (w - hi.astype(F32)).astype(BF16)
    return jnp.dot(jnp.concatenate([hi, lo], axis=1), e_ref[...], preferred_element_type=F32)


def _merged_attention(o_refs, l_refs, e_ref, wo_ref):
    ls = [l_ref[0] + l_ref[1] for l_ref in l_refs]
    mx = jnp.maximum(jnp.maximum(ls[0], ls[1]), ls[2])
    es = [jnp.exp(l - mx) for l in ls]
    tot = es[0] + es[1] + es[2]
    merged = None
    for e, o_ref in zip(es, o_refs):
        o = jnp.concatenate([o_ref[sl] for sl in range(o_ref.shape[0])], axis=1)
        term = _expand_heads(e / tot, e_ref) * o
        merged = term if merged is None else merged + term
    return jnp.dot(merged.astype(BF16), wo_ref[...], preferred_element_type=F32)


def _head_expander():
    head = jnp.arange(2 * LANES, dtype=jnp.int32)[:, None] % LANES
    lane = jnp.arange(ATTN_WIDTH, dtype=jnp.int32)[None, :]
    return (lane // HEAD_DIM == head).astype(BF16)


N_SAMPLE_ATTN_INPUTS = 6 * N_GROUPS
N_SAMPLE_ATTN_OUTPUTS = 1 + N_GROUPS


def _attn_sample_unit(in_refs, out_refs, half):
    (q0_ref, q1_ref, q2_ref, kn0_ref, kn1_ref, kn2_ref, vn0_ref, vn1_ref, vn2_ref, c0_ref, c1_ref, c2_ref,
     bc0_ref, bc1_ref, bc2_ref, bn0_ref, bn1_ref, bn2_ref) = in_refs
    o_ref, n0_ref, n1_ref, n2_ref = out_refs
    t = q0_ref.shape[1]
    lane_head = lax.broadcasted_iota(jnp.int32, (t, HALF), 1) // HEAD_DIM
    is_new = lax.broadcasted_iota(jnp.int32, (HALF, LANES), 1) >= LANES - t
    zpad = jnp.zeros((LANES - t, HALF), F32)
    outs, lses = [], []
    for q_ref, kn_ref, vn_ref, c_ref, bc_ref, bn_ref, n_ref in (
            (q0_ref, kn0_ref, vn0_ref, c0_ref, bc0_ref, bn0_ref, n0_ref),
            (q1_ref, kn1_ref, vn1_ref, c1_ref, bc1_ref, bn1_ref, n1_ref),
            (q2_ref, kn2_ref, vn2_ref, c2_ref, bc2_ref, bn2_ref, n2_ref)):
        w = c_ref.shape[3]
        qs = _stack_heads(q_ref[0], lane_head)
        k_t = c_ref[0, 0]
        v_t = c_ref[0, 1]
        k_new = jnp.concatenate([zpad, kn_ref[0]], axis=0)
        v_new = jnp.concatenate([zpad, vn_ref[0]], axis=0)
        k_new_t = k_new.T
        v_new_t = v_new.T
        s_c = jnp.dot(qs, k_t.astype(BF16), preferred_element_type=F32) + bc_ref[half]
        s_n = jnp.dot(qs, k_new_t.astype(BF16), preferred_element_type=F32) + bn_ref[half]
        m = jnp.maximum(s_c.max(axis=-1, keepdims=True), s_n.max(axis=-1, keepdims=True))
        p_c = jnp.exp(s_c - m)
        p_n = jnp.exp(s_n - m)
        den = p_c.sum(axis=-1, keepdims=True) + p_n.sum(axis=-1, keepdims=True)
        ov = lax.dot_general((p_c / den).astype(BF16), v_t.astype(BF16), (((1,), (1,)), ((), ())),
                             preferred_element_type=F32)
        ov = ov + jnp.dot((p_n / den).astype(BF16), v_new.astype(BF16), preferred_element_type=F32)
        outs.append(ov)
        lses.append(m + jnp.log(den))
        for kv, (old_t, new_t) in enumerate(((k_t, k_new_t), (v_t, v_new_t))):
            rolled = pltpu.roll(old_t, w - t, axis=1)
            if w > LANES:
                n_ref[0, kv, :, 0:w - LANES] = rolled[:, 0:w - LANES]
            n_ref[0, kv, :, w - LANES:w] = jnp.where(is_new, new_t, rolled[:, w - LANES:w])
    mx = jnp.maximum(jnp.maximum(lses[0], lses[1]), lses[2])
    es = [jnp.exp(l - mx) for l in lses]
    tot = es[0] + es[1] + es[2]
    merged = (es[0] / tot) * outs[0] + (es[1] / tot) * outs[1] + (es[2] / tot) * outs[2]
    o_ref[0] = _unstack_heads(merged, lane_head, t).astype(o_ref.dtype)


def _attn_sample_operands(q, kv_new, caches_t, bias_gs):
    b, t, _ = q.shape
    n_half = ATTN_WIDTH // HALF
    tok = lambda col: pl.BlockSpec((1, t, HALF), lambda i, col=col: (i // n_half, 0, col + i % n_half))
    in_specs = [tok(g * n_half) for g in range(N_GROUPS)]
    in_specs += [tok(g * 2 * n_half) for g in range(N_GROUPS)]
    in_specs += [tok(g * 2 * n_half + n_half) for g in range(N_GROUPS)]
    args = [q] * N_GROUPS + [kv_new] * (2 * N_GROUPS)
    cache_spec = lambda g: pl.BlockSpec((1, 2, HALF, GROUPS[g][0]), lambda i: (i // n_half, 0, i % n_half, 0))
    in_specs += [cache_spec(g) for g in range(N_GROUPS)]
    args += list(caches_t)
    tiles = [_sample_bias_tiles(bias_gs[g], g, t) for g in range(N_GROUPS)]
    for part in range(2):
        for g in range(N_GROUPS):
            in_specs.append(_const_spec(tiles[g][part].shape))
            args.append(tiles[g][part])
    out_specs = [pl.BlockSpec((1, t, HALF), lambda i: (i // n_half, 0, i % n_half))]
    out_specs += [cache_spec(g) for g in range(N_GROUPS)]
    out_shape = [jax.ShapeDtypeStruct((b, t, ATTN_WIDTH), BF16)] + [jax.ShapeDtypeStruct(c.shape, F32) for c in caches_t]
    return in_specs, args, out_specs, out_shape


def _out_proj_kernel(x_ref, a_ref, wo_ref, out_ref):
    out_ref[...] = x_ref[...] + jnp.dot(a_ref[...], wo_ref[...], preferred_element_type=F32)


def _out_proj(x, a, w_o):
    n, d = x.shape
    return pl.pallas_call(
        _out_proj_kernel,
        out_shape=jax.ShapeDtypeStruct((n, d), F32),
        compiler_params=pltpu.CompilerParams(vmem_limit_bytes=VMEM_LIMIT),
        name="out_proj",
    )(x, a, w_o)


def _cache_view(c_t, b):
    w = c_t.shape[-1]
    return jnp.transpose(c_t.reshape(b, 2, HEADS_PER_GROUP, HEAD_DIM, w), (0, 4, 1, 2, 3))


def _conv_args(wts):
    return (wts["norms"][0, 1], wts["conv_w_pw1"][0], wts["conv_b_pw1"][0], wts["conv_w_dw"][0], wts["conv_b_dw"][0],
            wts["conv_ln_g"][0], wts["conv_ln_b"][0], wts["conv_w_pw2"][0], wts["conv_b_pw2"][0])


def _sample_until_attention(x, conv_prev, kv_bufs, wts):
    b, t, d = x.shape
    norms, w_in, w_out = wts["norms"], wts["ffn_w_in"], wts["ffn_w_out"]
    xf = _ffn(x.reshape(b * t, d), norms[0, 0], w_in, w_out, (0, 0))
    x3, conv_state = _conv_module(xf.reshape(b, t, d), conv_prev, *_conv_args(wts), min(SAMPLE_CONV_BATCH, b), t)
    xf = _ffn(x3.reshape(b * t, d), norms[0, 2], w_in, w_out, (0, 1))
    kv_new = _proj(xf, wts["norm_kv"], wts["w_kv"], F32).reshape(b, t, -1)
    xf = _ffn(xf, norms[1, 0], w_in, w_out, (1, 0))
    q = _proj(xf, norms[1, 1], wts["attn_w_q"][0], BF16, scale=HEAD_DIM ** -0.5).reshape(b, t, -1)
    caches_t = [jnp.transpose(buf, (0, 2, 3, 4, 1)).reshape(b, 2, ATTN_WIDTH, buf.shape[1]) for buf in kv_bufs]
    return xf, conv_state, (q, kv_new, caches_t, wts["bias_gs"])


def _sample_after_attention(xf, attn, wts):
    xf = _out_proj(xf, attn.reshape(xf.shape[0], ATTN_WIDTH), wts["attn_w_o"][0])
    return _ffn(xf, wts["norms"][1, 2], wts["ffn_w_in"], wts["ffn_w_out"], (1, 1), g_final=wts["norm_final"])


def _prompt_trunk(x, wts, sample_attn):
    b, t, d = x.shape
    norms, w_in, w_out = wts["norms"], wts["ffn_w_in"], wts["ffn_w_out"]
    xf, *sample_res = _ffn(x.reshape(b * t, d), norms[0, 0], w_in, w_out, (0, 0), sample_attn=sample_attn)
    x3, conv_state = _conv_module_pipelined(xf.reshape(b, t, d), *_conv_args(wts))
    xf = _ffn(x3.reshape(b * t, d), norms[0, 2], w_in, w_out, (0, 1))
    shared, *kv_t = _kv_prompt(xf.reshape(b, t, d), wts["norm_kv"], wts["w_kv"])
    q_proj = (norms[1, 1], wts["attn_w_q"][0], HEAD_DIM ** -0.5)
    xf, q_sl = _ffn(xf, norms[1, 0], w_in, w_out, (1, 0), q_proj=q_proj)
    outs, lses = [], []
    for g in range(N_GROUPS):
        o, lse = _attn_prompt(q_sl, shared, _prompt_bias_tiles(wts["bias_gs"][g]), g, b, t)
        outs.append(o)
        lses.append(lse)
    xf = _ffn(xf, norms[1, 2], w_in, w_out, (1, 1), merge=(outs, lses, wts["attn_w_o"][0]),
              g_final=wts["norm_final"])
    return xf.reshape(b, t, d), conv_state, [_cache_view(c, b) for c in kv_t], sample_res


def kernel(x_prompt, x_sample, state_conv, cache_kv_w128, cache_kv_w512, cache_kv_w2048, norms, ffn_w_in, ffn_w_out,
           conv_w_pw1, conv_b_pw1, conv_w_dw, conv_b_dw, conv_ln_g, conv_ln_b, conv_w_pw2, conv_b_pw2, norm_kv, w_kv,
           attn_w_q, attn_w_o, rel_bias, norm_final):
    wts = dict(
        norms=norms, ffn_w_in=ffn_w_in.astype(BF16), ffn_w_out=ffn_w_out.astype(BF16),
        conv_w_pw1=conv_w_pw1.astype(BF16), conv_b_pw1=conv_b_pw1, conv_w_dw=conv_w_dw, conv_b_dw=conv_b_dw,
        conv_ln_g=conv_ln_g, conv_ln_b=conv_ln_b, conv_w_pw2=conv_w_pw2.astype(BF16), conv_b_pw2=conv_b_pw2,
        norm_kv=norm_kv, w_kv=w_kv.astype(BF16), attn_w_q=attn_w_q.astype(BF16), attn_w_o=attn_w_o.astype(BF16),
        rel_bias=rel_bias, norm_final=norm_final)
    wts["bias_gs"] = [_group_bias(rel_bias, g) for g in range(N_GROUPS)]
    bs, ts, d = x_sample.shape
    xs, conv_sample, sample_attn = _sample_until_attention(
        x_sample, state_conv[0], (cache_kv_w128, cache_kv_w512, cache_kv_w2048), wts)
    y_prompt, conv_prompt, kv_prompt, (attn_s, *caches_new) = _prompt_trunk(x_prompt, wts, sample_attn)
    y_sample = _sample_after_attention(xs, attn_s, wts).reshape(bs, ts, d)
    kv_sample = [_cache_view(c, bs) for c in caches_new]
    return (y_prompt, y_sample, conv_prompt[None], conv_sample[None], kv_prompt[0], kv_sample[0],
            kv_prompt[1], kv_sample[1], kv_prompt[2], kv_sample[2])
```
